```python
import jax, jax.numpy as jnp
from jax import lax
import numpy as np

D_MODEL = 1024
BATCH = 16
SEQ = 2048
DEPTH = 2

GRID_W = 64
CTX_LEN = 256
HEAD_DIM = 64
ATT_WIDTH = D_MODEL // 2
ML_WIDTH = D_MODEL // 4
POOL_WIDTH = D_MODEL - ATT_WIDTH - ML_WIDTH
MIX_WIDTH = ATT_WIDTH + ML_WIDTH + POOL_WIDTH
ATT_HEADS = ATT_WIDTH // HEAD_DIM
ATT_KV_HEADS = ATT_HEADS // 4
KV_WIDTH = ATT_KV_HEADS * HEAD_DIM
ML_HEADS = ML_WIDTH // HEAD_DIM
POOL_WINDOWS = (2, 4, 8, 16)
POOL_GROUPS = len(POOL_WINDOWS)
POOL_GROUP_DIM = POOL_WIDTH // POOL_GROUPS
N_GATES = 4 * ML_HEADS
IN_SPLITS = (ATT_WIDTH, KV_WIDTH, KV_WIDTH, ML_WIDTH, ML_WIDTH, ML_WIDTH, ML_WIDTH, N_GATES, POOL_WIDTH)
IN_WIDTH = sum(IN_SPLITS)
IN_OFFSETS = tuple(int(o) for o in np.cumsum(IN_SPLITS)[:-1])
D_FF = -(-8 * D_MODEL // (3 * 256)) * 256
Q_BLOCK = 128
CHUNK = 128
ROPE_THETA = 10000.0
EPS = 1e-6

kernel_name = 'hymba_style_attn_mlstm_pool_prefix_dit'

F32 = jnp.float32


def rms_norm(x, g):
    xf = x.astype(F32)
    y = xf * lax.rsqrt(jnp.mean(xf * xf, -1, keepdims=True) + EPS)
    return (y * g.astype(F32)).astype(x.dtype)


def axial_rope_tables(rows):
    half = HEAD_DIM // 2
    inv_freq = 1.0 / (ROPE_THETA ** (jnp.arange(0, half, 2, dtype=F32) / half))
    row = jnp.repeat(jnp.arange(rows, dtype=F32), GRID_W)
    col = jnp.tile(jnp.arange(GRID_W, dtype=F32), rows)
    a_row = row[:, None] * inv_freq
    a_col = col[:, None] * inv_freq
    ang = jnp.concatenate([a_row, a_row, a_col, a_col], -1)
    return jnp.cos(ang), jnp.sin(ang)


def apply_rope(x, cos, sin):
    q = HEAD_DIM // 4
    xr = x.reshape(x.shape[:-1] + (2, 2, q))
    rot = jnp.stack([-xr[..., 1, :], xr[..., 0, :]], -2).reshape(x.shape)
    out = x.astype(F32) * cos[:, None, :] + rot.astype(F32) * sin[:, None, :]
    return out.astype(x.dtype)


def block_attention(q, k, v):
    B, T, Hq, d = q.shape
    Hk = k.shape[2]
    G = Hq // Hk
    nb = T // Q_BLOCK
    qb = q.reshape(B, nb, Q_BLOCK, Hk, G, d).swapaxes(0, 1)
    scale = d ** -0.5

    def one_block(qi):
        s = jnp.einsum('bqhgd,bkhd->bhgqk', qi, k, preferred_element_type=F32) * scale
        p = jax.nn.softmax(s, -1).astype(v.dtype)
        return jnp.einsum('bhgqk,bkhd->bqhgd', p, v)

    o = lax.map(one_block, qb)
    return o.swapaxes(0, 1).reshape(B, T, Hq * d)


def mlstm_scan(q, k, v, log_i, log_f, state):
    B, H, T, d = q.shape
    nc = T // CHUNK

    def split(a):
        a = a.astype(F32)
        return jnp.moveaxis(a.reshape(a.shape[:2] + (nc, CHUNK) + a.shape[3:]), 2, 0)

    xs = (split(q), split(k * d ** -0.5), split(v), split(log_i), split(log_f))
    tril = jnp.tril(jnp.ones((CHUNK, CHUNK), bool))

    def step(carry, inp):
        C, n, m = carry
        qc, kc, vc, li, lf = inp
        b = jnp.cumsum(lf, -1)
        dmat = jnp.where(tril, b[..., :, None] - b[..., None, :] + li[..., None, :], -jnp.inf)
        inter = b + m[..., None]
        m_t = jnp.maximum(inter, jnp.max(dmat, -1))
        w_inter = jnp.exp(inter - m_t)
        s = jnp.einsum('bhtd,bhsd->bhts', qc, kc) * jnp.exp(dmat - m_t[..., None])
        num = w_inter[..., None] * jnp.einsum('bhvk,bhtk->bhtv', C, qc) + jnp.einsum('bhts,bhsv->bhtv', s, vc)
        den = w_inter * jnp.einsum('bhk,bhtk->bht', n, qc) + jnp.sum(s, -1)
        h = num / jnp.maximum(jnp.abs(den), jnp.exp(-m_t))[..., None]
        b_last = b[..., -1]
        g = b_last[..., None] - b + li
        m_new = jnp.maximum(b_last + m, jnp.max(g, -1))
        a = jnp.exp(b_last + m - m_new)
        wk = jnp.exp(g - m_new[..., None])
        C_new = a[..., None, None] * C + jnp.einsum('bhs,bhsv,bhsk->bhvk', wk, vc, kc)
        n_new = a[..., None] * n + jnp.einsum('bhs,bhsk->bhk', wk, kc)
        return (C_new, n_new, m_new), h

    state, hs = lax.scan(step, state, xs)
    return jnp.moveaxis(hs, 0, 2).reshape(B, H, T, d), state


def bidir_mlstm(q_c, k_c, v_c, g_c, q_l, k_l, v_l, g_l):
    B, H, _, d = q_l.shape
    zero = (jnp.zeros((B, H, d, d), F32), jnp.zeros((B, H, d), F32), jnp.zeros((B, H), F32))
    outs_c, outs_l = [], []
    for direc in range(2):
        rev = (lambda a: jnp.flip(a, 2)) if direc else (lambda a: a)
        li_c, lf_c = g_c[2 * direc], jax.nn.log_sigmoid(g_c[2 * direc + 1])
        li_l, lf_l = g_l[2 * direc], jax.nn.log_sigmoid(g_l[2 * direc + 1])
        h_c, st = mlstm_scan(rev(q_c), rev(k_c), rev(v_c), rev(li_c), rev(lf_c), zero)
        h_l, _ = mlstm_scan(rev(q_l), rev(k_l), rev(v_l), rev(li_l), rev(lf_l), st)
        outs_c.append(rev(h_c))
        outs_l.append(rev(h_l))
    return outs_c[0] + outs_c[1], outs_l[0] + outs_l[1]


def mlstm_readout(h, o_pre, ml_norm):
    B, H, T, d = h.shape
    hn = rms_norm(h.transpose(0, 2, 1, 3), ml_norm.reshape(H, d)).reshape(B, T, H * d)
    return (hn * jax.nn.sigmoid(o_pre.astype(F32))).astype(o_pre.dtype)


def pool_mix(u, pool_w, pool_scale):
    B, T, _ = u.shape
    uf = u.astype(F32)
    cs = jnp.concatenate([jnp.zeros((B, 1, POOL_WIDTH), F32), jnp.cumsum(uf, 1)], 1)
    t = jnp.arange(T)
    outs = []
    for gi, w in enumerate(POOL_WINDOWS):
        lo = jnp.maximum(t - w // 2, 0)
        hi = jnp.minimum(t + w // 2, T)
        sl = slice(gi * POOL_GROUP_DIM, (gi + 1) * POOL_GROUP_DIM)
        csg = cs[:, :, sl]
        mean = (csg[:, hi] - csg[:, lo]) / (hi - lo).astype(F32)[None, :, None]
        outs.append(mean - uf[:, :, sl])
    p = jnp.stack(outs, 2)
    y = jnp.einsum('btgi,gio->btgo', p, pool_w.astype(F32)).reshape(B, T, POOL_WIDTH)
    return (y * pool_scale.astype(F32)).astype(u.dtype)


def swiglu(h, w_g, w_u, w_d):
    return (jax.nn.silu(h @ w_g) * (h @ w_u)) @ w_d


def hybrid_layer(xl, xc, c, c_ctx, w_ada, b_ada, norm_mix, w_in, b_gates, q_norm, k_norm, ml_norm,
                 pool_w, pool_scale, w_out, norm_ffn, w_ffn_gate, w_ffn_up, w_ffn_down, cos, sin, with_ctx_out):
    B, T, _ = xl.shape
    Tc = xc.shape[1]
    mod_l = (jax.nn.silu(c) @ w_ada + b_ada)[:, None, :]
    mod_c = jax.nn.silu(c_ctx) @ w_ada + b_ada
    sh1_l, sc1_l, g1_l, sh2_l, sc2_l, g2_l = jnp.split(mod_l, 6, -1)
    sh1_c, sc1_c, g1_c, sh2_c, sc2_c, g2_c = jnp.split(mod_c, 6, -1)

    hl = rms_norm(xl, norm_mix) * (1.0 + sc1_l) + sh1_l
    hc = rms_norm(xc, norm_mix) * (1.0 + sc1_c) + sh1_c
    aq_l, ak_l, av_l, mq_l, mk_l, mv_l, mo_l, mg_l, pp_l = jnp.split(hl @ w_in, IN_OFFSETS, -1)
    aq_c, ak_c, av_c, mq_c, mk_c, mv_c, mo_c, mg_c, pp_c = jnp.split(hc @ w_in, IN_OFFSETS, -1)

    q_l = apply_rope(rms_norm(aq_l.reshape(B, T, ATT_HEADS, HEAD_DIM), q_norm), cos, sin)
    k_l = apply_rope(rms_norm(ak_l.reshape(B, T, ATT_KV_HEADS, HEAD_DIM), k_norm), cos, sin)
    v_l = av_l.reshape(B, T, ATT_KV_HEADS, HEAD_DIM)
    k_c = rms_norm(ak_c.reshape(B, Tc, ATT_KV_HEADS, HEAD_DIM), k_norm)
    v_c = av_c.reshape(B, Tc, ATT_KV_HEADS, HEAD_DIM)
    att_l = block_attention(q_l, jnp.concatenate([k_l, k_c], 1), jnp.concatenate([v_l, v_c], 1))

    def ml_heads(a, n):
        return a.reshape(B, n, ML_HEADS, HEAD_DIM).transpose(0, 2, 1, 3)

    def ml_gates(gp, n):
        return (gp + b_gates).astype(F32).reshape(B, n, 4, ML_HEADS).transpose(2, 0, 3, 1)

    hm_c, hm_l = bidir_mlstm(ml_heads(mq_c, Tc), ml_heads(mk_c, Tc), ml_heads(mv_c, Tc), ml_gates(mg_c, Tc),
                             ml_heads(mq_l, T), ml_heads(mk_l, T), ml_heads(mv_l, T), ml_gates(mg_l, T))
    ml_l = mlstm_readout(hm_l, mo_l, ml_norm)

    pool_l = pool_mix(pp_l, pool_w, pool_scale)

    xl = xl + g1_l * (jnp.concatenate([att_l, ml_l, pool_l], -1) @ w_out)
    h2l = rms_norm(xl, norm_ffn) * (1.0 + sc2_l) + sh2_l
    xl = xl + g2_l * swiglu(h2l, w_ffn_gate, w_ffn_up, w_ffn_down)

    if with_ctx_out:
        q_c = rms_norm(aq_c.reshape(B, Tc, ATT_HEADS, HEAD_DIM), q_norm)
        att_c = block_attention(q_c, k_c, v_c)
        ml_c = mlstm_readout(hm_c, mo_c, ml_norm)
        pool_c = pool_mix(pp_c, pool_w, pool_scale)
        xc = xc + g1_c * (jnp.concatenate([att_c, ml_c, pool_c], -1) @ w_out)
        h2c = rms_norm(xc, norm_ffn) * (1.0 + sc2_c) + sh2_c
        xc = xc + g2_c * swiglu(h2c, w_ffn_gate, w_ffn_up, w_ffn_down)
    return xl, xc


def setup_inputs(seed: int = 0) -> dict:
    key = jax.random.key(seed)
    ks = jax.random.split(key, 22)

    def dense(k, shape, fan_in, mult=1.0):
        return mult * jax.random.normal(k, shape, F32) * fan_in ** -0.5

    def gain(k, shape):
        return 1.0 + 0.05 * jax.random.normal(k, shape, F32)

    f_bias = jnp.linspace(3.0, 6.0, ML_HEADS, dtype=F32)
    gate_base = jnp.stack([jnp.zeros_like(f_bias), f_bias, jnp.zeros_like(f_bias), f_bias])
    b_gates = (gate_base[None] + 0.1 * jax.random.normal(ks[8], (DEPTH, 4, ML_HEADS), F32)).reshape(DEPTH, N_GATES)
    return {
        'x': jax.random.normal(ks[0], (BATCH, SEQ, D_MODEL), F32),
        'c': jax.random.normal(ks[1], (BATCH, D_MODEL), F32),
        'ctx': jax.random.normal(ks[2], (BATCH, CTX_LEN, D_MODEL), F32),
        'c_ctx': jax.random.normal(ks[3], (D_MODEL,), F32),
        'w_ada': dense(ks[4], (DEPTH, D_MODEL, 6 * D_MODEL), D_MODEL, 0.5),
        'b_ada': 0.02 * jax.random.normal(ks[5], (DEPTH, 6 * D_MODEL), F32),
        'norm_mix': gain(ks[6], (DEPTH, D_MODEL)),
        'w_in': dense(ks[7], (DEPTH, D_MODEL, IN_WIDTH), D_MODEL),
        'b_gates': b_gates,
        'q_norm': gain(ks[9], (DEPTH, HEAD_DIM)),
        'k_norm': gain(ks[10], (DEPTH, HEAD_DIM)),
        'ml_norm': gain(ks[11], (DEPTH, ML_WIDTH)),
        'pool_w': dense(ks[12], (DEPTH, POOL_GROUPS, POOL_GROUP_DIM, POOL_GROUP_DIM), POOL_GROUP_DIM),
        'pool_scale': 1.0 + 0.1 * jax.random.normal(ks[13], (DEPTH, POOL_WIDTH), F32),
        'w_out': dense(ks[14], (DEPTH, MIX_WIDTH, D_MODEL), MIX_WIDTH),
        'norm_ffn': gain(ks[15], (DEPTH, D_MODEL)),
        'w_ffn_gate': dense(ks[16], (DEPTH, D_MODEL, D_FF), D_MODEL),
        'w_ffn_up': dense(ks[17], (DEPTH, D_MODEL, D_FF), D_MODEL),
        'w_ffn_down': dense(ks[18], (DEPTH, D_FF, D_MODEL), D_FF),
        'final_norm': gain(ks[19], (D_MODEL,)),
    }


def reference(x, c, ctx, c_ctx, w_ada, b_ada, norm_mix, w_in, b_gates, q_norm, k_norm, ml_norm,
              pool_w, pool_scale, w_out, norm_ffn, w_ffn_gate, w_ffn_up, w_ffn_down, final_norm):
    ROWS = x.shape[1] // GRID_W
    cos, sin = axial_rope_tables(ROWS)
    xl, xc = x, ctx
    for layer in range(DEPTH):
        xl, xc = hybrid_layer(xl, xc, c, c_ctx, w_ada[layer], b_ada[layer], norm_mix[layer], w_in[layer],
                              b_gates[layer], q_norm[layer], k_norm[layer], ml_norm[layer], pool_w[layer],
                              pool_scale[layer], w_out[layer], norm_ffn[layer], w_ffn_gate[layer],
                              w_ffn_up[layer], w_ffn_down[layer], cos, sin, layer < DEPTH - 1)
    return rms_norm(xl, final_norm)
```

```python
import functools

import jax
import jax.numpy as jnp
import numpy as np
from jax import lax
from jax.experimental import pallas as pl
from jax.experimental.pallas import tpu as pltpu

F32 = jnp.float32
BF16 = jnp.bfloat16

D_MODEL = 1024
HEAD_DIM = 64
GRID_W = 64
ATT_WIDTH = 512
KV_WIDTH = 128
ML_WIDTH = 256
POOL_WIDTH = 256
ATT_HEADS = 8
ML_HEADS = 4
N_GATES = 16
POOL_WINDOWS = (2, 4, 8, 16)
POOL_GROUP_DIM = 64
D_FF = 2816
CHUNK = 128
ROPE_THETA = 10000.0
EPS = 1e-6

LANES = 128
ROW_TILE = 256
MOD_ROWS = 24
POOL_PAD = 16
VMEM_LIMIT = 52 * 1024 * 1024

OFF_Q, OFF_K, OFF_V, OFF_MQ, OFF_MK, OFF_MV, OFF_MO, OFF_PP, OFF_G = 0, 512, 640, 768, 1024, 1280, 1536, 1792, 2048
IN_COLS = OFF_G + LANES

HEAD_PERM = np.concatenate(
    [np.concatenate([np.arange(c * 64, (c + 1) * 64), np.arange((4 + c) * 64, (5 + c) * 64)]) for c in range(4)])


def _sigmoid(x):
    return 1.0 / (1.0 + jnp.exp(-x))


def _split3(a):
    hi = a.astype(BF16)
    r1 = a - hi.astype(F32)
    mid = r1.astype(BF16)
    lo = (r1 - mid.astype(F32)).astype(BF16)
    return hi, mid, lo


def _mod_kernel(c_ref, w_ref, b_ref, o_ref):
    c = c_ref[...]
    s = c * _sigmoid(c)
    s_hi, s_mid, s_lo = _split3(s)
    w_hi, w_mid, w_lo = _split3(w_ref[...])
    dot = functools.partial(jnp.dot, preferred_element_type=F32)
    acc = dot(s_lo, w_hi) + dot(s_hi, w_lo) + dot(s_mid, w_mid)
    acc = acc + dot(s_mid, w_hi) + dot(s_hi, w_mid)
    acc = acc + dot(s_hi, w_hi)
    o_ref[...] = acc + b_ref[...]


def _modulation(c_all, w_ada, b_ada):
    depth = w_ada.shape[0]
    tn = 1024
    return pl.pallas_call(
        _mod_kernel,
        out_shape=jax.ShapeDtypeStruct((depth, MOD_ROWS, 6 * D_MODEL), F32),
        grid=(depth, 6 * D_MODEL // tn),
        in_specs=[
            pl.BlockSpec((MOD_ROWS, D_MODEL), lambda l, j: (0, 0)),
            pl.BlockSpec((None, D_MODEL, tn), lambda l, j: (l, 0, j)),
            pl.BlockSpec((None, 1, tn), lambda l, j: (l, 0, j)),
        ],
        out_specs=pl.BlockSpec((None, MOD_ROWS, tn), lambda l, j: (l, 0, j)),
        name="adaln_mod",
    )(c_all, w_ada, b_ada.reshape(depth, 1, 6 * D_MODEL))


def _segment_ms(x, lo_mask):
    x2 = x * x
    s_lo = jnp.sum(jnp.where(lo_mask, x2, 0.0), axis=-1, keepdims=True)
    s_hi = jnp.sum(jnp.where(lo_mask, 0.0, x2), axis=-1, keepdims=True)
    return jnp.where(lo_mask, s_lo, s_hi) * (1.0 / HEAD_DIM)


def _inproj_kernel(x_ref, mod_ref, nm_ref, w_ref, bg_ref, qn_ref, kn_ref, cos_ref, sa_ref, sb_ref,
                   q_ref, k_ref, v_ref, mq_ref, mk_ref, mv_ref, mo_ref, pp_ref, g_ref):
    x = x_ref[...]
    ms = jnp.mean(x * x, axis=-1, keepdims=True)
    y = x * lax.rsqrt(ms + EPS) * nm_ref[...]
    h = y * (1.0 + mod_ref[1]) + mod_ref[0]
    acc = jnp.dot(h.astype(BF16), w_ref[...], preferred_element_type=F32)

    rows = x.shape[0]
    lo_mask = lax.broadcasted_iota(jnp.int32, (rows, LANES), 1) < HEAD_DIM
    cos, sa, sb = cos_ref[...], sa_ref[...], sb_ref[...]

    def norm_rope(xc, gain):
        yc = xc * lax.rsqrt(_segment_ms(xc, lo_mask) + EPS) * gain
        return yc * cos + pltpu.roll(yc, LANES - 16, 1) * sa + pltpu.roll(yc, 16, 1) * sb

    for c in range(ATT_WIDTH // LANES):
        qc = norm_rope(acc[:, OFF_Q + c * LANES:OFF_Q + (c + 1) * LANES], qn_ref[...])
        q_ref[:, c * LANES:(c + 1) * LANES] = (qc * (HEAD_DIM ** -0.5)).astype(BF16)
    k_ref[...] = norm_rope(acc[:, OFF_K:OFF_K + LANES], kn_ref[...]).astype(BF16)
    v_ref[...] = acc[:, OFF_V:OFF_V + KV_WIDTH].astype(BF16)
    mq_ref[...] = acc[:, OFF_MQ:OFF_MQ + ML_WIDTH].astype(BF16)
    mk_ref[...] = acc[:, OFF_MK:OFF_MK + ML_WIDTH].astype(BF16)
    mv_ref[...] = acc[:, OFF_MV:OFF_MV + ML_WIDTH].astype(BF16)
    mo_ref[...] = acc[:, OFF_MO:OFF_MO + ML_WIDTH].astype(BF16)
    pp_ref[...] = acc[:, OFF_PP:OFF_PP + POOL_WIDTH].astype(BF16)
    gates = acc[:, OFF_G:OFF_G + LANES] + bg_ref[...]
    g_ref[...] = gates.T[:N_GATES, :]


def _mod_spec(layer, ctx_tiles, ctx_row, first=0):
    def idx(b, i):
        return (layer, jnp.where(i + first < ctx_tiles, ctx_row, b), 0, 0, 0)
    return pl.BlockSpec((None, None, 6, 1, D_MODEL), idx)


def _in_proj(x_all, mod5, layer, norm_mix, w_in_r, b_gates_p, qn, kn, cos_t, sa_t, sb_t, ctx_tiles):
    B, TA, _ = x_all.shape
    nt = TA // ROW_TILE
    row = lambda w: pl.BlockSpec((None, ROW_TILE, w), lambda b, i: (b, i, 0))
    vec = lambda w: pl.BlockSpec((1, w), lambda b, i: (0, 0))
    tab = pl.BlockSpec((ROW_TILE, LANES), lambda b, i: (i, 0))
    out_shapes = [jax.ShapeDtypeStruct((B, TA, w), BF16)
                  for w in (ATT_WIDTH, KV_WIDTH, KV_WIDTH, ML_WIDTH, ML_WIDTH, ML_WIDTH, ML_WIDTH, POOL_WIDTH)]
    out_shapes.append(jax.ShapeDtypeStruct((B, N_GATES, TA), F32))
    out_specs = [row(w) for w in (ATT_WIDTH, KV_WIDTH, KV_WIDTH, ML_WIDTH, ML_WIDTH, ML_WIDTH, ML_WIDTH, POOL_WIDTH)]
    out_specs.append(pl.BlockSpec((None, N_GATES, ROW_TILE), lambda b, i: (b, 0, i)))
    return pl.pallas_call(
        _inproj_kernel,
        out_shape=out_shapes,
        grid=(B, nt),
        in_specs=[row(D_MODEL), _mod_spec(layer, ctx_tiles, B), vec(D_MODEL),
                  pl.BlockSpec((D_MODEL, IN_COLS), lambda b, i: (0, 0)),
                  vec(LANES), vec(LANES), vec(LANES), tab, tab, tab],
        out_specs=out_specs,
        compiler_params=pltpu.CompilerParams(vmem_limit_bytes=VMEM_LIMIT),
        name="in_proj",
    )(x_all, mod5, norm_mix, w_in_r, b_gates_p, qn, kn, cos_t, sa_t, sb_t)


def _attn_kernel(q_ref, k_ref, v_ref, o_ref, k0_ref, k1_ref, va_ref, *, ctx_len, first_tile):
    i = pl.program_id(1)
    total = k_ref.shape[0]

    @pl.when(i == 0)
    def _():
        k = k_ref[...]
        lo = lax.broadcasted_iota(jnp.int32, k.shape, 1) < HEAD_DIM
        zero = jnp.zeros_like(k)
        k0_ref[...] = jnp.where(lo, k, zero)
        k1_ref[...] = jnp.where(lo, zero, k)
        va_ref[:, 0:LANES] = v_ref[...]
        va_ref[:, LANES:2 * LANES] = jnp.ones((total, LANES), BF16)

    rows = q_ref.shape[0]
    lo_mask = lax.broadcasted_iota(jnp.int32, (rows, LANES), 1) < HEAD_DIM

    def block(nk):
        for c in range(ATT_WIDTH // LANES):
            qc = q_ref[:, c * LANES:(c + 1) * LANES]
            outs = []
            for kr in (k0_ref, k1_ref):
                s = lax.dot_general(qc, kr[0:nk, :], (((1,), (1,)), ((), ())), preferred_element_type=F32)
                m = jnp.max(s, axis=-1, keepdims=True)
                p = jnp.exp(s - m).astype(BF16)
                a = jnp.dot(p, va_ref[0:nk, :], preferred_element_type=F32)
                outs.append(a[:, 0:LANES] / a[:, LANES:2 * LANES])
            o_ref[:, c * LANES:(c + 1) * LANES] = jnp.where(lo_mask, outs[0], outs[1]).astype(BF16)

    if first_tile == 0:
        pl.when(i == 0)(lambda: block(ctx_len))
        pl.when(i > 0)(lambda: block(total))
    else:
        block(total)


def _attention(q, k, v, ctx_len, with_ctx):
    B, TA, _ = q.shape
    first = 0 if with_ctx else ctx_len // ROW_TILE
    nt = TA // ROW_TILE - first
    kern = functools.partial(_attn_kernel, ctx_len=ctx_len, first_tile=first)
    return pl.pallas_call(
        kern,
        out_shape=jax.ShapeDtypeStruct((B, nt * ROW_TILE, ATT_WIDTH), BF16),
        grid=(B, nt),
        in_specs=[pl.BlockSpec((None, ROW_TILE, ATT_WIDTH), lambda b, i: (b, i + first, 0)),
                  pl.BlockSpec((None, TA, KV_WIDTH), lambda b, i: (b, 0, 0)),
                  pl.BlockSpec((None, TA, KV_WIDTH), lambda b, i: (b, 0, 0))],
        out_specs=pl.BlockSpec((None, ROW_TILE, ATT_WIDTH), lambda b, i: (b, i, 0)),
        scratch_shapes=[pltpu.VMEM((TA, KV_WIDTH), BF16), pltpu.VMEM((TA, KV_WIDTH), BF16),
                        pltpu.VMEM((TA, 2 * LANES), BF16)],
        compiler_params=pltpu.CompilerParams(vmem_limit_bytes=VMEM_LIMIT),
        name="attention",
    )(q, k, v)


def _mlstm_kernel(qf_ref, kf_ref, vf_ref, gf_ref, qb_ref, kb_ref, vb_ref, gb_ref, hf_ref, hb_ref,
                  st_ref, m_ref, r_ref):
    i = pl.program_id(1)

    @pl.when(i == 0)
    def _():
        st_ref[...] = jnp.zeros_like(st_ref)
        m_ref[...] = jnp.zeros_like(m_ref)
        r_ref[...] = jnp.zeros_like(r_ref)

    L = CHUNK
    lane8 = lax.broadcasted_iota(jnp.int32, (8, L), 1)
    row8 = lax.broadcasted_iota(jnp.int32, (8, L), 0)
    t_idx = lax.broadcasted_iota(jnp.int32, (L, L), 0)
    s_idx = lax.broadcasted_iota(jnp.int32, (L, L), 1)
    lane_h = lax.broadcasted_iota(jnp.int32, (L, LANES - HEAD_DIM), 1)
    ones_col = jnp.where(lane_h == 0, 1.0, 0.0).astype(BF16)
    nt = (((1,), (1,)), ((), ()))
    tn = (((0,), (0,)), ((), ()))

    dirs = ((qf_ref, kf_ref, vf_ref, gf_ref, hf_ref), (qb_ref, kb_ref, vb_ref, gb_ref, hb_ref))
    for d, (q_r, k_r, v_r, g_r, h_r) in enumerate(dirs):
        g8 = g_r[8 * d:8 * d + 8, :]
        lf = jnp.minimum(g8, 0.0) - jnp.log1p(jnp.exp(-jnp.abs(g8)))
        b8 = lf
        for sh in (1, 2, 4, 8, 16, 32, 64):
            if d == 0:
                b8 = b8 + jnp.where(lane8 >= sh, pltpu.roll(b8, sh, 1), 0.0)
            else:
                b8 = b8 + jnp.where(lane8 < L - sh, pltpu.roll(b8, L - sh, 1), 0.0)
        r8 = jnp.where(row8 < ML_HEADS, g8, b8)
        r_ref[d, 0:8, :] = r8
        ct = r_ref[d].T
        causal = (s_idx <= t_idx) if d == 0 else (s_idx >= t_idx)
        for hd in range(ML_HEADS):
            u = d * ML_HEADS + hd
            li_row, b_row = r8[hd:hd + 1, :], r8[ML_HEADS + hd:ML_HEADS + hd + 1, :]
            li_col, b_col = ct[:, hd:hd + 1], ct[:, ML_HEADS + hd:ML_HEADS + hd + 1]
            m_prev = m_ref[u, 0:1, 0:1]
            dmat = jnp.where(causal, b_col - b_row + li_row, -jnp.inf)
            inter = b_col + m_prev
            m_t = jnp.maximum(inter, jnp.max(dmat, axis=-1, keepdims=True))
            w_inter = jnp.exp(inter - m_t)
            dexp = jnp.exp(dmat - m_t)
            qh = q_r[:, hd * HEAD_DIM:(hd + 1) * HEAD_DIM]
            kh = k_r[:, hd * HEAD_DIM:(hd + 1) * HEAD_DIM]
            vh = v_r[:, hd * HEAD_DIM:(hd + 1) * HEAD_DIM]
            v_aug = jnp.concatenate([vh, ones_col], axis=-1)
            s_mat = lax.dot_general(qh, kh, nt, preferred_element_type=F32) * (HEAD_DIM ** -0.5) * dexp
            state = st_ref[u]
            a1 = jnp.dot(qh, state.astype(BF16), preferred_element_type=F32)
            a2 = jnp.dot(s_mat.astype(BF16), v_aug, preferred_element_type=F32)
            nd = w_inter * a1 + a2
            den = nd[:, HEAD_DIM:HEAD_DIM + 1]
            hout = nd / jnp.maximum(jnp.abs(den), jnp.exp(-m_t))
            h_r[:, hd * HEAD_DIM:(hd + 1) * HEAD_DIM] = hout[:, 0:HEAD_DIM].astype(BF16)

            b_last = b_row[:, L - 1:L] if d == 0 else b_row[:, 0:1]
            g_row = b_last - b_row + li_row
            g_col = b_last - b_col + li_col
            m_new = jnp.maximum(b_last + m_prev, jnp.max(g_row, axis=-1, keepdims=True))
            a = jnp.exp(b_last + m_prev - m_new)
            wk = jnp.exp(g_col - m_new)
            kw = (wk * (HEAD_DIM ** -0.5) * kh.astype(F32)).astype(BF16)
            upd = lax.dot_general(kw, v_aug, tn, preferred_element_type=F32)
            st_ref[u] = a * state + upd
            m_ref[u] = jnp.broadcast_to(m_new, (8, LANES))


def _mlstm(mq, mk, mv, gates, ctx_len):
    B, TA, _ = mq.shape
    nc = TA // CHUNK
    ncc = ctx_len // CHUNK

    def fwd(i):
        return i

    def bwd(i):
        return jnp.where(i < ncc, ncc - 1 - i, nc - 1 - (i - ncc))

    tok = lambda f: pl.BlockSpec((None, CHUNK, ML_WIDTH), lambda b, i: (b, f(i), 0))
    gat = lambda f: pl.BlockSpec((None, N_GATES, CHUNK), lambda b, i: (b, 0, f(i)))
    n_units = 2 * ML_HEADS
    return pl.pallas_call(
        _mlstm_kernel,
        out_shape=[jax.ShapeDtypeStruct((B, TA, ML_WIDTH), BF16)] * 2,
        grid=(B, nc),
        in_specs=[tok(fwd), tok(fwd), tok(fwd), gat(fwd), tok(bwd), tok(bwd), tok(bwd), gat(bwd)],
        out_specs=[tok(fwd), tok(bwd)],
        scratch_shapes=[pltpu.VMEM((n_units, HEAD_DIM, LANES), F32), pltpu.VMEM((n_units, 8, LANES), F32),
                        pltpu.VMEM((2, CHUNK, LANES), F32)],
        compiler_params=pltpu.CompilerParams(vmem_limit_bytes=VMEM_LIMIT),
        name="mlstm",
    )(mq, mk, mv, gates, mq, mk, mv, gates)


def _pool_kernel(pp_ref, bd_ref, ps_ref, o_ref, p_ref, s2_ref, s4_ref, s8_ref, *, ctx_len):
    total = pp_ref.shape[0]
    seqs = ((0, ctx_len, POOL_PAD), (ctx_len, total - ctx_len, 2 * POOL_PAD + ctx_len))
    rows = p_ref.shape[0]
    for ref in (p_ref, s2_ref, s4_ref, s8_ref):
        ref[...] = jnp.zeros_like(ref)
    for src, n, dst in seqs:
        p_ref[dst:dst + n, :] = pp_ref[src:src + n, :].astype(F32)
    lo, m = 8, rows - 16
    s2_ref[lo:lo + m, :] = p_ref[lo - 1:lo - 1 + m, :] + p_ref[lo:lo + m, :]
    s4_ref[lo:lo + m, :] = s2_ref[lo - 1:lo - 1 + m, :] + s2_ref[lo + 1:lo + 1 + m, :]
    s8_ref[lo:lo + m, :] = s4_ref[lo - 2:lo - 2 + m, :] + s4_ref[lo + 2:lo + 2 + m, :]
    for src, n, dst in seqs:
        lane = lax.broadcasted_iota(jnp.int32, (n, POOL_WIDTH), 1)
        t = lax.broadcasted_iota(jnp.int32, (n, POOL_WIDTH), 0)
        g0, g1, g2 = lane < POOL_GROUP_DIM, lane < 2 * POOL_GROUP_DIM, lane < 3 * POOL_GROUP_DIM
        half = jnp.where(g0, 1, jnp.where(g1, 2, jnp.where(g2, 4, 8)))
        cnt = (jnp.minimum(t + half, n) - jnp.maximum(t - half, 0)).astype(F32)
        s16 = s8_ref[dst - 4:dst - 4 + n, :] + s8_ref[dst + 4:dst + 4 + n, :]
        wsum = jnp.where(g0, s2_ref[dst:dst + n, :],
                         jnp.where(g1, s4_ref[dst:dst + n, :], jnp.where(g2, s8_ref[dst:dst + n, :], s16)))
        p = wsum / cnt - p_ref[dst:dst + n, :]
        y = jnp.dot(p.astype(BF16), bd_ref[...], preferred_element_type=F32) * ps_ref[...]
        o_ref[src:src + n, :] = y.astype(BF16)


def _pool(pp, pool_bd, pool_scale, ctx_len):
    B, TA, _ = pp.shape
    rows = TA + 3 * POOL_PAD
    kern = functools.partial(_pool_kernel, ctx_len=ctx_len)
    full = pl.BlockSpec((None, TA, POOL_WIDTH), lambda b: (b, 0, 0))
    return pl.pallas_call(
        kern,
        out_shape=jax.ShapeDtypeStruct((B, TA, POOL_WIDTH), BF16),
        grid=(B,),
        in_specs=[full, pl.BlockSpec((POOL_WIDTH, POOL_WIDTH), lambda b: (0, 0)),
                  pl.BlockSpec((1, POOL_WIDTH), lambda b: (0, 0))],
        out_specs=full,
        scratch_shapes=[pltpu.VMEM((rows, POOL_WIDTH), F32)] * 4,
        compiler_params=pltpu.CompilerParams(vmem_limit_bytes=VMEM_LIMIT),
        name="pool_mix",
    )(pp, pool_bd, pool_scale)


def _mixffn_kernel(x_ref, att_ref, hf_ref, hb_ref, mo_ref, pool_ref, mod_ref, mln_ref, wo_ref, nf_ref,
                   wg_ref, wu_ref, wd_ref, fn_ref, o_ref, *, final):
    rows = x_ref.shape[0]
    lo_mask = lax.broadcasted_iota(jnp.int32, (rows, LANES), 1) < HEAD_DIM
    ml = []
    for c in range(ML_WIDTH // LANES):
        sl = slice(c * LANES, (c + 1) * LANES)
        hs = hf_ref[:, sl].astype(F32) + hb_ref[:, sl].astype(F32)
        hn = hs * lax.rsqrt(_segment_ms(hs, lo_mask) + EPS) * mln_ref[:, sl]
        ml.append((hn * _sigmoid(mo_ref[:, sl].astype(F32))).astype(BF16))
    cat = jnp.concatenate([att_ref[...]] + ml + [pool_ref[...]], axis=-1)
    mix = jnp.dot(cat, wo_ref[...], preferred_element_type=F32)
    x1 = x_ref[...] + mod_ref[2] * mix
    ms = jnp.mean(x1 * x1, axis=-1, keepdims=True)
    h2 = (x1 * lax.rsqrt(ms + EPS) * nf_ref[...] * (1.0 + mod_ref[4]) + mod_ref[3]).astype(BF16)
    gate = jnp.dot(h2, wg_ref[...], preferred_element_type=F32)
    up = jnp.dot(h2, wu_ref[...], preferred_element_type=F32)
    act = (gate * _sigmoid(gate) * up).astype(BF16)
    out = x1 + mod_ref[5] * jnp.dot(act, wd_ref[...], preferred_element_type=F32)
    if final:
        ms = jnp.mean(out * out, axis=-1, keepdims=True)
        out = out * lax.rsqrt(ms + EPS) * fn_ref[...]
    o_ref[...] = out


def _mix_ffn(x_all, att, hf, hb, mo, pool, mod5, layer, ml_norm, w_out_r, norm_ffn, wg, wu, wd, final_norm,
             ctx_tiles, final):
    B, TA, _ = x_all.shape
    first = ctx_tiles if final else 0
    nt = TA // ROW_TILE - first
    row = lambda w: pl.BlockSpec((None, ROW_TILE, w), lambda b, i: (b, i + first, 0))
    vec = lambda w: pl.BlockSpec((1, w), lambda b, i: (0, 0))
    wspec = lambda r, c: pl.BlockSpec((r, c), lambda b, i: (0, 0), pipeline_mode=pl.Buffered(1))

    kern = functools.partial(_mixffn_kernel, final=final)
    return pl.pallas_call(
        kern,
        out_shape=jax.ShapeDtypeStruct((B, nt * ROW_TILE, D_MODEL), F32),
        grid=(B, nt),
        in_specs=[row(D_MODEL), pl.BlockSpec((None, ROW_TILE, ATT_WIDTH), lambda b, i: (b, i, 0)),
                  row(ML_WIDTH), row(ML_WIDTH), row(ML_WIDTH), row(POOL_WIDTH),
                  _mod_spec(layer, ctx_tiles, B, first), vec(ML_WIDTH),
                  wspec(D_MODEL, D_MODEL), vec(D_MODEL), wspec(D_MODEL, D_FF), wspec(D_MODEL, D_FF),
                  wspec(D_FF, D_MODEL), vec(D_MODEL)],
        out_specs=pl.BlockSpec((None, ROW_TILE, D_MODEL), lambda b, i: (b, i, 0)),
        compiler_params=pltpu.CompilerParams(vmem_limit_bytes=VMEM_LIMIT),
        name="mix_ffn",
    )(x_all, att, hf, hb, mo, pool, mod5, ml_norm, w_out_r, norm_ffn, wg, wu, wd, final_norm)


def _rope_tables(t_latent, ctx_len):
    half = HEAD_DIM // 2
    inv_freq = 1.0 / (ROPE_THETA ** (jnp.arange(0, half, 2, dtype=F32) / half))
    rows = t_latent // GRID_W
    row = jnp.repeat(jnp.arange(rows, dtype=F32), GRID_W)
    col = jnp.tile(jnp.arange(GRID_W, dtype=F32), rows)
    a_row = row[:, None] * inv_freq
    a_col = col[:, None] * inv_freq
    ang = jnp.concatenate([a_row, a_row, a_col, a_col], -1)
    cos, sin = jnp.cos(ang), jnp.sin(ang)
    first_half = (jnp.arange(HEAD_DIM) % 32) < 16
    sa = jnp.where(first_half, -sin, 0.0)
    sb = jnp.where(first_half, 0.0, sin)
    pad = lambda a, fill: jnp.tile(jnp.concatenate([jnp.full((ctx_len, HEAD_DIM), fill, F32), a], 0), (1, 2))
    return pad(cos, 1.0), pad(sa, 0.0), pad(sb, 0.0)


def _rearrange_w_in(w_in):
    aq, ak, av, mq, mk, mv, mo, mg, pp = jnp.split(w_in, (512, 640, 768, 1024, 1280, 1536, 1792, 1808), axis=-1)
    mg = jnp.pad(mg, ((0, 0), (0, LANES - N_GATES)))
    return jnp.concatenate([aq[:, HEAD_PERM], ak, av, mq, mk, mv, mo, pp, mg], axis=-1).astype(BF16)


def kernel(x, c, ctx, c_ctx, w_ada, b_ada, norm_mix, w_in, b_gates, q_norm, k_norm, ml_norm, pool_w, pool_scale,
           w_out, norm_ffn, w_ffn_gate, w_ffn_up, w_ffn_down, final_norm):
    B, T, _ = x.shape
    ctx_len = ctx.shape[1]
    depth = w_ada.shape[0]
    assert ctx_len % ROW_TILE == 0 and T % ROW_TILE == 0 and B + 1 <= MOD_ROWS
    ctx_tiles = ctx_len // ROW_TILE

    c_all = jnp.concatenate([c, c_ctx[None, :], jnp.zeros((MOD_ROWS - B - 1, D_MODEL), F32)], 0)
    mod = _modulation(c_all, w_ada, b_ada)
    mod5 = mod.reshape(depth, MOD_ROWS, 6, 1, D_MODEL)
    cos_t, sa_t, sb_t = _rope_tables(T, ctx_len)
    vec = lambda a: a.reshape(1, -1)

    x_all = jnp.concatenate([ctx, x], axis=1)
    for l in range(depth):
        final = l == depth - 1
        w_in_r = _rearrange_w_in(w_in[l])
        b_gates_p = jnp.pad(b_gates[l], (0, LANES - N_GATES)).reshape(1, LANES)
        qn = jnp.tile(q_norm[l], 2).reshape(1, LANES)
        kn = jnp.tile(k_norm[l], 2).reshape(1, LANES)
        pool_bd = jax.scipy.linalg.block_diag(*[pool_w[l, g] for g in range(len(POOL_WINDOWS))]).astype(BF16)
        w_out_r = jnp.concatenate([w_out[l, :ATT_WIDTH][HEAD_PERM], w_out[l, ATT_WIDTH:]], 0).astype(BF16)

        q, k, v, mq, mk, mv, mo, pp, gates = _in_proj(
            x_all, mod5, l, vec(norm_mix[l]), w_in_r, b_gates_p, qn, kn, cos_t, sa_t, sb_t, ctx_tiles)
        att = _attention(q, k, v, ctx_len, with_ctx=not final)
        hf, hb = _mlstm(mq, mk, mv, gates, ctx_len)
        pool = _pool(pp, pool_bd, vec(pool_scale[l]), ctx_len)
        x_all = _mix_ffn(x_all, att, hf, hb, mo, pool, mod5, l, vec(ml_norm[l]), w_out_r, vec(norm_ffn[l]),
                         w_ffn_gate[l].astype(BF16), w_ffn_up[l].astype(BF16), w_ffn_down[l].astype(BF16),
                         vec(final_norm), ctx_tiles, final)
    return x_all
```

```python
import functools

import jax
import jax.numpy as jnp
import numpy as np
from jax import lax
from jax.experimental import pallas as pl
from jax.experimental.pallas import tpu as pltpu

F32 = jnp.float32
BF16 = jnp.bfloat16

D_MODEL = 1024
HEAD_DIM = 64
GRID_W = 64
ATT_WIDTH = 512
KV_WIDTH = 128
ML_WIDTH = 256
POOL_WIDTH = 256
ATT_HEADS = 8
ML_HEADS = 4
N_GATES = 16
POOL_WINDOWS = (2, 4, 8, 16)
POOL_GROUP_DIM = 64
D_FF = 2816
CHUNK = 128
ROPE_THETA = 10000.0
EPS = 1e-6

LANES = 128
ROW_TILE = 256
MOD_ROWS = 24
POOL_PAD = 16
VMEM_LIMIT = 52 * 1024 * 1024

OFF_Q, OFF_K, OFF_V, OFF_MK, OFF_MO, OFF_PP = 0, 512, 640, 768, 1024, 1280
IN_COLS = OFF_PP + POOL_WIDTH
ROW_MQ, ROW_MV, ROW_G = 0, 256, 512
T_ROWS = ROW_G + N_GATES

HEAD_PERM = np.concatenate(
    [np.concatenate([np.arange(c * 64, (c + 1) * 64), np.arange((4 + c) * 64, (5 + c) * 64)]) for c in range(4)])


def _sigmoid(x):
    return 1.0 / (1.0 + jnp.exp(-x))


def _split3(a):
    hi = a.astype(BF16)
    r1 = a - hi.astype(F32)
    mid = r1.astype(BF16)
    lo = (r1 - mid.astype(F32)).astype(BF16)
    return hi, mid, lo


def _mod_kernel(c_ref, w_ref, b_ref, o_ref):
    c = c_ref[...]
    s = c * _sigmoid(c)
    s_hi, s_mid, s_lo = _split3(s)
    w_hi, w_mid, w_lo = _split3(w_ref[...])
    dot = functools.partial(jnp.dot, preferred_element_type=F32)
    acc = dot(s_lo, w_hi) + dot(s_hi, w_lo) + dot(s_mid, w_mid)
    acc = acc + dot(s_mid, w_hi) + dot(s_hi, w_mid)
    acc = acc + dot(s_hi, w_hi)
    o_ref[...] = acc + b_ref[...]


def _modulation(c_all, w_ada, b_ada):
    depth = w_ada.shape[0]
    tn = 1024
    return pl.pallas_call(
        _mod_kernel,
        out_shape=jax.ShapeDtypeStruct((depth, MOD_ROWS, 6 * D_MODEL), F32),
        grid=(depth, 6 * D_MODEL // tn),
        in_specs=[
            pl.BlockSpec((MOD_ROWS, D_MODEL), lambda l, j: (0, 0)),
            pl.BlockSpec((None, D_MODEL, tn), lambda l, j: (l, 0, j)),
            pl.BlockSpec((None, 1, tn), lambda l, j: (l, 0, j)),
        ],
        out_specs=pl.BlockSpec((None, MOD_ROWS, tn), lambda l, j: (l, 0, j)),
        name="adaln_mod",
    )(c_all, w_ada, b_ada.reshape(depth, 1, 6 * D_MODEL))


def _segment_ms(x, lo_mask):
    x2 = x * x
    s_lo = jnp.sum(jnp.where(lo_mask, x2, 0.0), axis=-1, keepdims=True)
    s_hi = jnp.sum(jnp.where(lo_mask, 0.0, x2), axis=-1, keepdims=True)
    return jnp.where(lo_mask, s_lo, s_hi) * (1.0 / HEAD_DIM)


def _inproj_kernel(x_ref, mod_ref, nm_ref, w_ref, wt_ref, bg_ref, qn_ref, kn_ref, cos_ref, sa_ref, sb_ref,
                   q_ref, k_ref, v_ref, mk_ref, mo_ref, pp_ref, mqt_ref, mvt_ref, g_ref):
    x = x_ref[...]
    ms = jnp.mean(x * x, axis=-1, keepdims=True)
    y = x * lax.rsqrt(ms + EPS) * nm_ref[...]
    h = (y * (1.0 + mod_ref[1]) + mod_ref[0]).astype(BF16)
    acc = jnp.dot(h, w_ref[...], preferred_element_type=F32)
    acc_t = lax.dot_general(wt_ref[...], h, (((1,), (1,)), ((), ())), preferred_element_type=F32)

    rows = x.shape[0]
    lo_mask = lax.broadcasted_iota(jnp.int32, (rows, LANES), 1) < HEAD_DIM
    cos, sa, sb = cos_ref[...], sa_ref[...], sb_ref[...]

    def norm_rope(xc, gain):
        yc = xc * lax.rsqrt(_segment_ms(xc, lo_mask) + EPS) * gain
        return yc * cos + pltpu.roll(yc, LANES - 16, 1) * sa + pltpu.roll(yc, 16, 1) * sb

    for c in range(ATT_WIDTH // LANES):
        qc = norm_rope(acc[:, OFF_Q + c * LANES:OFF_Q + (c + 1) * LANES], qn_ref[...])
        q_ref[:, c * LANES:(c + 1) * LANES] = (qc * (HEAD_DIM ** -0.5)).astype(BF16)
    k_ref[...] = norm_rope(acc[:, OFF_K:OFF_K + LANES], kn_ref[...]).astype(BF16)
    v_ref[...] = acc[:, OFF_V:OFF_V + KV_WIDTH].astype(BF16)
    mk_ref[...] = acc[:, OFF_MK:OFF_MK + ML_WIDTH].astype(BF16)
    mo_ref[...] = acc[:, OFF_MO:OFF_MO + ML_WIDTH].astype(BF16)
    pp_ref[...] = acc[:, OFF_PP:OFF_PP + POOL_WIDTH].astype(BF16)
    mqt_ref[...] = acc_t[ROW_MQ:ROW_MQ + ML_WIDTH, :].astype(BF16)
    mvt_ref[...] = acc_t[ROW_MV:ROW_MV + ML_WIDTH, :].astype(BF16)
    g_ref[...] = acc_t[ROW_G:ROW_G + N_GATES, :] + bg_ref[...]


def _mod_spec(layer, ctx_tiles, ctx_row, first=0):
    def idx(b, i):
        return (layer, jnp.where(i + first < ctx_tiles, ctx_row, b), 0, 0, 0)
    return pl.BlockSpec((None, None, 6, 1, D_MODEL), idx)


def _in_proj(x_all, mod5, layer, norm_mix, w_in_r, w_in_t, b_gates_b, qn, kn, cos_t, sa_t, sb_t, ctx_tiles):
    B, TA, _ = x_all.shape
    nt = TA // ROW_TILE
    row = lambda w: pl.BlockSpec((None, ROW_TILE, w), lambda b, i: (b, i, 0))
    col = lambda r: pl.BlockSpec((None, r, ROW_TILE), lambda b, i: (b, 0, i))
    vec = lambda w: pl.BlockSpec((1, w), lambda b, i: (0, 0))
    full = lambda r, c: pl.BlockSpec((r, c), lambda b, i: (0, 0))
    tab = pl.BlockSpec((ROW_TILE, LANES), lambda b, i: (i, 0))
    row_widths = (ATT_WIDTH, KV_WIDTH, KV_WIDTH, ML_WIDTH, ML_WIDTH, POOL_WIDTH)
    out_shapes = [jax.ShapeDtypeStruct((B, TA, w), BF16) for w in row_widths]
    out_shapes += [jax.ShapeDtypeStruct((B, ML_WIDTH, TA), BF16)] * 2
    out_shapes.append(jax.ShapeDtypeStruct((B, N_GATES, TA), F32))
    out_specs = [row(w) for w in row_widths] + [col(ML_WIDTH), col(ML_WIDTH), col(N_GATES)]
    return pl.pallas_call(
        _inproj_kernel,
        out_shape=out_shapes,
        grid=(B, nt),
        in_specs=[row(D_MODEL), _mod_spec(layer, ctx_tiles, B), vec(D_MODEL), full(D_MODEL, IN_COLS),
                  full(T_ROWS, D_MODEL), full(N_GATES, ROW_TILE), vec(LANES), vec(LANES), tab, tab, tab],
        out_specs=out_specs,
        compiler_params=pltpu.CompilerParams(vmem_limit_bytes=VMEM_LIMIT),
        name="in_proj",
    )(x_all, mod5, norm_mix, w_in_r, w_in_t, b_gates_b, qn, kn, cos_t, sa_t, sb_t)


def _attn_kernel(q_ref, k_ref, v_ref, o_ref, k0_ref, k1_ref, va_ref, *, ctx_len, first_tile):
    i = pl.program_id(1)
    total = k_ref.shape[0]

    @pl.when(i == 0)
    def _():
        k = k_ref[...]
        lo = lax.broadcasted_iota(jnp.int32, k.shape, 1) < HEAD_DIM
        zero = jnp.zeros_like(k)
        k0_ref[...] = jnp.where(lo, k, zero)
        k1_ref[...] = jnp.where(lo, zero, k)
        va_ref[:, 0:LANES] = v_ref[...]
        va_ref[:, LANES:2 * LANES] = jnp.ones((total, LANES), BF16)

    rows = q_ref.shape[0]
    lo_mask = lax.broadcasted_iota(jnp.int32, (rows, LANES), 1) < HEAD_DIM

    def block(nk):
        for c in range(ATT_WIDTH // LANES):
            qc = q_ref[:, c * LANES:(c + 1) * LANES]
            outs = []
            for kr in (k0_ref, k1_ref):
                s = lax.dot_general(qc, kr[0:nk, :], (((1,), (1,)), ((), ())), preferred_element_type=F32)
                m = jnp.max(s, axis=-1, keepdims=True)
                p = jnp.exp(s - m).astype(BF16)
                a = jnp.dot(p, va_ref[0:nk, :], preferred_element_type=F32)
                outs.append(a[:, 0:LANES] / a[:, LANES:2 * LANES])
            o_ref[:, c * LANES:(c + 1) * LANES] = jnp.where(lo_mask, outs[0], outs[1]).astype(BF16)

    if first_tile == 0:
        pl.when(i == 0)(lambda: block(ctx_len))
        pl.when(i > 0)(lambda: block(total))
    else:
        block(total)


def _attention(q, k, v, ctx_len, with_ctx):
    B, TA, _ = q.shape
    first = 0 if with_ctx else ctx_len // ROW_TILE
    nt = TA // ROW_TILE - first
    kern = functools.partial(_attn_kernel, ctx_len=ctx_len, first_tile=first)
    return pl.pallas_call(
        kern,
        out_shape=jax.ShapeDtypeStruct((B, nt * ROW_TILE, ATT_WIDTH), BF16),
        grid=(B, nt),
        in_specs=[pl.BlockSpec((None, ROW_TILE, ATT_WIDTH), lambda b, i: (b, i + first, 0)),
                  pl.BlockSpec((None, TA, KV_WIDTH), lambda b, i: (b, 0, 0)),
                  pl.BlockSpec((None, TA, KV_WIDTH), lambda b, i: (b, 0, 0))],
        out_specs=pl.BlockSpec((None, ROW_TILE, ATT_WIDTH), lambda b, i: (b, i, 0)),
        scratch_shapes=[pltpu.VMEM((TA, KV_WIDTH), BF16), pltpu.VMEM((TA, KV_WIDTH), BF16),
                        pltpu.VMEM((TA, 2 * LANES), BF16)],
        compiler_params=pltpu.CompilerParams(vmem_limit_bytes=VMEM_LIMIT),
        name="attention",
    )(q, k, v)


def _mlstm_kernel(qf_ref, kf_ref, vf_ref, gf_ref, qb_ref, kb_ref, vb_ref, gb_ref, tri_ref, hf_ref, hb_ref,
                  st_ref, m_ref, r_ref):
    i = pl.program_id(1)

    @pl.when(i == 0)
    def _():
        st_ref[...] = jnp.zeros_like(st_ref)
        m_ref[...] = jnp.zeros_like(m_ref)
        r_ref[...] = jnp.zeros_like(r_ref)

    L = CHUNK
    s_idx = lax.broadcasted_iota(jnp.int32, (L, L), 0)
    t_idx = lax.broadcasted_iota(jnp.int32, (L, L), 1)
    first_head_rows = lax.broadcasted_iota(jnp.int32, (LANES, L), 0) < HEAD_DIM
    ones_blk = jnp.where(lax.broadcasted_iota(jnp.int32, (HEAD_DIM, L), 0) == 0, 1.0, 0.0).astype(BF16)

    dirs = ((qf_ref, kf_ref, vf_ref, gf_ref, hf_ref), (qb_ref, kb_ref, vb_ref, gb_ref, hb_ref))
    for d, (q_r, k_r, v_r, g_r, h_r) in enumerate(dirs):
        g8 = g_r[8 * d:8 * d + 8, :]
        lf8 = jnp.minimum(g8, 0.0) - jnp.log1p(jnp.exp(-jnp.abs(g8)))
        parts = jnp.concatenate([p.astype(F32) for p in _split3(lf8)], axis=0)
        cum = jnp.dot(parts, tri_ref[d], preferred_element_type=F32)
        cum = cum[16:24] + cum[8:16] + cum[0:8]
        b8, btot8 = cum[:, 0:L], cum[:, L:2 * L]
        li8 = pltpu.roll(g8, ML_HEADS, 0)
        m_prev8 = m_ref[d]
        gg8 = btot8 - b8 + li8
        m_new8 = jnp.maximum(btot8 + m_prev8, jnp.max(gg8, axis=-1, keepdims=True))
        a8 = jnp.exp(btot8 + m_prev8 - m_new8)
        wk8 = jnp.exp(gg8 - m_new8) * (HEAD_DIM ** -0.5)
        inter8 = b8 + m_prev8
        m_ref[d] = m_new8
        r_ref[d, 0:8, :] = li8 - b8
        ct = r_ref[d].T
        causal = (s_idx <= t_idx) if d == 0 else (s_idx >= t_idx)
        hts = []
        for hd in range(ML_HEADS):
            u, r = d * ML_HEADS + hd, ML_HEADS + hd
            pair, second = divmod(hd, 2)
            k_pair = k_r[:, pair * LANES:(pair + 1) * LANES]
            q_pair = q_r[pair * LANES:(pair + 1) * LANES, :]
            keep = jnp.logical_not(first_head_rows) if second else first_head_rows
            qm = jnp.where(keep, q_pair, jnp.zeros_like(q_pair))
            dmat = jnp.where(causal, ct[:, r:r + 1] + b8[r:r + 1], -jnp.inf)
            inter = inter8[r:r + 1]
            m_t = jnp.maximum(inter, jnp.max(dmat, axis=0, keepdims=True))
            w_inter = jnp.exp(inter - m_t)
            s_t = jnp.dot(k_pair, qm, preferred_element_type=F32) * (HEAD_DIM ** -0.5) * jnp.exp(dmat - m_t)
            state = st_ref[u]
            v_aug = jnp.concatenate([v_r[hd * HEAD_DIM:(hd + 1) * HEAD_DIM, :], ones_blk], axis=0)
            a1 = jnp.dot(state.astype(BF16), qm, preferred_element_type=F32)
            a2 = jnp.dot(v_aug, s_t.astype(BF16), preferred_element_type=F32)
            nd = w_inter * a1 + a2
            den = nd[HEAD_DIM:HEAD_DIM + 1, :]
            hts.append(nd[0:HEAD_DIM, :] / jnp.maximum(jnp.abs(den), jnp.exp(-m_t)))
            vw = (v_aug.astype(F32) * wk8[r:r + 1]).astype(BF16)
            st_ref[u] = a8[r:r + 1] * state + jnp.dot(vw, k_pair, preferred_element_type=F32)
        h_r[...] = jnp.concatenate(hts, axis=0).T.astype(BF16)


def _mlstm(mqt, mk, mvt, gates, tri, ctx_len):
    B, TA, _ = mk.shape
    nc = TA // CHUNK
    ncc = ctx_len // CHUNK

    def fwd(i):
        return i

    def bwd(i):
        return jnp.where(i < ncc, ncc - 1 - i, nc - 1 - (i - ncc))

    tok = lambda f: pl.BlockSpec((None, CHUNK, ML_WIDTH), lambda b, i: (b, f(i), 0))
    feat = lambda f, r: pl.BlockSpec((None, r, CHUNK), lambda b, i: (b, 0, f(i)))
    n_units = 2 * ML_HEADS
    return pl.pallas_call(
        _mlstm_kernel,
        out_shape=[jax.ShapeDtypeStruct((B, TA, ML_WIDTH), BF16)] * 2,
        grid=(B, nc),
        in_specs=[feat(fwd, ML_WIDTH), tok(fwd), feat(fwd, ML_WIDTH), feat(fwd, N_GATES),
                  feat(bwd, ML_WIDTH), tok(bwd), feat(bwd, ML_WIDTH), feat(bwd, N_GATES),
                  pl.BlockSpec((2, CHUNK, 2 * CHUNK), lambda b, i: (0, 0, 0))],
        out_specs=[tok(fwd), tok(bwd)],
        scratch_shapes=[pltpu.VMEM((n_units, LANES, LANES), F32), pltpu.VMEM((2, 8, LANES), F32),
                        pltpu.VMEM((2, CHUNK, LANES), F32)],
        compiler_params=pltpu.CompilerParams(vmem_limit_bytes=VMEM_LIMIT),
        name="mlstm",
    )(mqt, mk, mvt, gates, mqt, mk, mvt, gates, tri)


def _pool_kernel(pp_ref, bd_ref, ps_ref, o_ref, p_ref, s2_ref, s4_ref, s8_ref, *, ctx_len):
    total = pp_ref.shape[0]
    seqs = ((0, ctx_len, POOL_PAD), (ctx_len, total - ctx_len, 2 * POOL_PAD + ctx_len))
    rows = p_ref.shape[0]
    for ref in (p_ref, s2_ref, s4_ref, s8_ref):
        ref[...] = jnp.zeros_like(ref)
    for src, n, dst in seqs:
        p_ref[dst:dst + n, :] = pp_ref[src:src + n, :].astype(F32)
    lo, m = 8, rows - 16
    s2_ref[lo:lo + m, :] = p_ref[lo - 1:lo - 1 + m, :] + p_ref[lo:lo + m, :]
    s4_ref[lo:lo + m, :] = s2_ref[lo - 1:lo - 1 + m, :] + s2_ref[lo + 1:lo + 1 + m, :]
    s8_ref[lo:lo + m, :] = s4_ref[lo - 2:lo - 2 + m, :] + s4_ref[lo + 2:lo + 2 + m, :]
    for src, n, dst in seqs:
        lane = lax.broadcasted_iota(jnp.int32, (n, POOL_WIDTH), 1)
        t = lax.broadcasted_iota(jnp.int32, (n, POOL_WIDTH), 0)
        g0, g1, g2 = lane < POOL_GROUP_DIM, lane < 2 * POOL_GROUP_DIM, lane < 3 * POOL_GROUP_DIM
        half = jnp.where(g0, 1, jnp.where(g1, 2, jnp.where(g2, 4, 8)))
        cnt = (jnp.minimum(t + half, n) - jnp.maximum(t - half, 0)).astype(F32)
        s16 = s8_ref[dst - 4:dst - 4 + n, :] + s8_ref[dst + 4:dst + 4 + n, :]
        wsum = jnp.where(g0, s2_ref[dst:dst + n, :],
                         jnp.where(g1, s4_ref[dst:dst + n, :], jnp.where(g2, s8_ref[dst:dst + n, :], s16)))
        p = wsum / cnt - p_ref[dst:dst + n, :]
        y = jnp.dot(p.astype(BF16), bd_ref[...], preferred_element_type=F32) * ps_ref[...]
        o_ref[src:src + n, :] = y.astype(BF16)


def _pool(pp, pool_bd, pool_scale, ctx_len):
    B, TA, _ = pp.shape
    rows = TA + 3 * POOL_PAD
    kern = functools.partial(_pool_kernel, ctx_len=ctx_len)
    full = pl.BlockSpec((None, TA, POOL_WIDTH), lambda b: (b, 0, 0))
    return pl.pallas_call(
        kern,
        out_shape=jax.ShapeDtypeStruct((B, TA, POOL_WIDTH), BF16),
        grid=(B,),
        in_specs=[full, pl.BlockSpec((POOL_WIDTH, POOL_WIDTH), lambda b: (0, 0)),
                  pl.BlockSpec((1, POOL_WIDTH), lambda b: (0, 0))],
        out_specs=full,
        scratch_shapes=[pltpu.VMEM((rows, POOL_WIDTH), F32)] * 4,
        compiler_params=pltpu.CompilerParams(vmem_limit_bytes=VMEM_LIMIT),
        name="pool_mix",
    )(pp, pool_bd, pool_scale)


def _mixffn_kernel(x_ref, att_ref, hf_ref, hb_ref, mo_ref, pool_ref, mod_ref, mln_ref, wo_ref, nf_ref,
                   wg_ref, wu_ref, wd_ref, fn_ref, o_ref, *, final):
    rows = x_ref.shape[0]
    lo_mask = lax.broadcasted_iota(jnp.int32, (rows, LANES), 1) < HEAD_DIM
    ml = []
    for c in range(ML_WIDTH // LANES):
        sl = slice(c * LANES, (c + 1) * LANES)
        hs = hf_ref[:, sl].astype(F32) + hb_ref[:, sl].astype(F32)
        hn = hs * lax.rsqrt(_segment_ms(hs, lo_mask) + EPS) * mln_ref[:, sl]
        ml.append((hn * _sigmoid(mo_ref[:, sl].astype(F32))).astype(BF16))
    cat = jnp.concatenate([att_ref[...]] + ml + [pool_ref[...]], axis=-1)
    mix = jnp.dot(cat, wo_ref[...], preferred_element_type=F32)
    x1 = x_ref[...] + mod_ref[2] * mix
    ms = jnp.mean(x1 * x1, axis=-1, keepdims=True)
    h2 = (x1 * lax.rsqrt(ms + EPS) * nf_ref[...] * (1.0 + mod_ref[4]) + mod_ref[3]).astype(BF16)
    gate = jnp.dot(h2, wg_ref[...], preferred_element_type=F32)
    up = jnp.dot(h2, wu_ref[...], preferred_element_type=F32)
    act = (gate * _sigmoid(gate) * up).astype(BF16)
    out = x1 + mod_ref[5] * jnp.dot(act, wd_ref[...], preferred_element_type=F32)
    if final:
        ms = jnp.mean(out * out, axis=-1, keepdims=True)
        out = out * lax.rsqrt(ms + EPS) * fn_ref[...]
    o_ref[...] = out


def _mix_ffn(x_all, att, hf, hb, mo, pool, mod5, layer, ml_norm, w_out_r, norm_ffn, wg, wu, wd, final_norm,
             ctx_tiles, final):
    B, TA, _ = x_all.shape
    first = ctx_tiles if final else 0
    nt = TA // ROW_TILE - first
    row = lambda w: pl.BlockSpec((None, ROW_TILE, w), lambda b, i: (b, i + first, 0))
    vec = lambda w: pl.BlockSpec((1, w), lambda b, i: (0, 0))
    wspec = lambda r, c: pl.BlockSpec((r, c), lambda b, i: (0, 0), pipeline_mode=pl.Buffered(1))

    kern = functools.partial(_mixffn_kernel, final=final)
    return pl.pallas_call(
        kern,
        out_shape=jax.ShapeDtypeStruct((B, nt * ROW_TILE, D_MODEL), F32),
        grid=(B, nt),
        in_specs=[row(D_MODEL), pl.BlockSpec((None, ROW_TILE, ATT_WIDTH), lambda b, i: (b, i, 0)),
                  row(ML_WIDTH), row(ML_WIDTH), row(ML_WIDTH), row(POOL_WIDTH),
                  _mod_spec(layer, ctx_tiles, B, first), vec(ML_WIDTH),
                  wspec(D_MODEL, D_MODEL), vec(D_MODEL), wspec(D_MODEL, D_FF), wspec(D_MODEL, D_FF),
                  wspec(D_FF, D_MODEL), vec(D_MODEL)],
        out_specs=pl.BlockSpec((None, ROW_TILE, D_MODEL), lambda b, i: (b, i, 0)),
        compiler_params=pltpu.CompilerParams(vmem_limit_bytes=VMEM_LIMIT),
        name="mix_ffn",
    )(x_all, att, hf, hb, mo, pool, mod5, ml_norm, w_out_r, norm_ffn, wg, wu, wd, final_norm)


def _rope_tables(t_latent, ctx_len):
    half = HEAD_DIM // 2
    inv_freq = 1.0 / (ROPE_THETA ** (jnp.arange(0, half, 2, dtype=F32) / half))
    rows = t_latent // GRID_W
    row = jnp.repeat(jnp.arange(rows, dtype=F32), GRID_W)
    col = jnp.tile(jnp.arange(GRID_W, dtype=F32), rows)
    a_row = row[:, None] * inv_freq
    a_col = col[:, None] * inv_freq
    ang = jnp.concatenate([a_row, a_row, a_col, a_col], -1)
    cos, sin = jnp.cos(ang), jnp.sin(ang)
    first_half = (jnp.arange(HEAD_DIM) % 32) < 16
    sa = jnp.where(first_half, -sin, 0.0)
    sb = jnp.where(first_half, 0.0, sin)
    pad = lambda a, fill: jnp.tile(jnp.concatenate([jnp.full((ctx_len, HEAD_DIM), fill, F32), a], 0), (1, 2))
    return pad(cos, 1.0), pad(sa, 0.0), pad(sb, 0.0)


def _rearrange_w_in(w_in):
    aq, ak, av, mq, mk, mv, mo, mg, pp = jnp.split(w_in, (512, 640, 768, 1024, 1280, 1536, 1792, 1808), axis=-1)
    w_rows = jnp.concatenate([aq[:, HEAD_PERM], ak, av, mk, mo, pp], axis=-1).astype(BF16)
    w_feat = jnp.concatenate([mq, mv, mg], axis=-1).T.astype(BF16)
    return w_rows, w_feat


def _cumsum_matrices():
    s = np.arange(CHUNK)[:, None]
    t = np.arange(CHUNK)[None, :]
    ones = np.ones((CHUNK, CHUNK), np.float32)
    prefix = np.concatenate([(s <= t).astype(np.float32), ones], axis=1)
    suffix = np.concatenate([(s >= t).astype(np.float32), ones], axis=1)
    return jnp.asarray(np.stack([prefix, suffix]))


def kernel(x, c, ctx, c_ctx, w_ada, b_ada, norm_mix, w_in, b_gates, q_norm, k_norm, ml_norm, pool_w, pool_scale,
           w_out, norm_ffn, w_ffn_gate, w_ffn_up, w_ffn_down, final_norm):
    B, T, _ = x.shape
    ctx_len = ctx.shape[1]
    depth = w_ada.shape[0]
    assert ctx_len % ROW_TILE == 0 and T % ROW_TILE == 0 and B + 1 <= MOD_ROWS
    ctx_tiles = ctx_len // ROW_TILE

    c_all = jnp.concatenate([c, c_ctx[None, :], jnp.zeros((MOD_ROWS - B - 1, D_MODEL), F32)], 0)
    mod = _modulation(c_all, w_ada, b_ada)
    mod5 = mod.reshape(depth, MOD_ROWS, 6, 1, D_MODEL)
    cos_t, sa_t, sb_t = _rope_tables(T, ctx_len)
    tri = _cumsum_matrices()
    vec = lambda a: a.reshape(1, -1)

    x_all = jnp.concatenate([ctx, x], axis=1)
    for l in range(depth):
        final = l == depth - 1
        w_in_r, w_in_t = _rearrange_w_in(w_in[l])
        b_gates_b = jnp.broadcast_to(b_gates[l][:, None], (N_GATES, ROW_TILE))
        qn = jnp.tile(q_norm[l], 2).reshape(1, LANES)
        kn = jnp.tile(k_norm[l], 2).reshape(1, LANES)
        pool_bd = jax.scipy.linalg.block_diag(*[pool_w[l, g] for g in range(len(POOL_WINDOWS))]).astype(BF16)
        w_out_r = jnp.concatenate([w_out[l, :ATT_WIDTH][HEAD_PERM], w_out[l, ATT_WIDTH:]], 0).astype(BF16)

        q, k, v, mk, mo, pp, mqt, mvt, gates = _in_proj(
            x_all, mod5, l, vec(norm_mix[l]), w_in_r, w_in_t, b_gates_b, qn, kn, cos_t, sa_t, sb_t, ctx_tiles)
        att = _attention(q, k, v, ctx_len, with_ctx=not final)
        hf, hb = _mlstm(mqt, mk, mvt, gates, tri, ctx_len)
        pool = _pool(pp, pool_bd, vec(pool_scale[l]), ctx_len)
        x_all = _mix_ffn(x_all, att, hf, hb, mo, pool, mod5, l, vec(ml_norm[l]), w_out_r, vec(norm_ffn[l]),
                         w_ffn_gate[l].astype(BF16), w_ffn_up[l].astype(BF16), w_ffn_down[l].astype(BF16),
                         vec(final_norm), ctx_tiles, final)
    return x_all
```

```python
import functools

import jax
import jax.numpy as jnp
import numpy as np
from jax import lax
from jax.experimental import pallas as pl
from jax.experimental.pallas import tpu as pltpu

F32 = jnp.float32
BF16 = jnp.bfloat16

D_MODEL = 1024
HEAD_DIM = 64
GRID_W = 64
ATT_WIDTH = 512
KV_WIDTH = 128
ML_WIDTH = 256
POOL_WIDTH = 256
ATT_HEADS = 8
ML_HEADS = 4
N_GATES = 16
POOL_WINDOWS = (2, 4, 8, 16)
POOL_GROUP_DIM = 64
D_FF = 2816
CHUNK = 128
ROPE_THETA = 10000.0
EPS = 1e-6

LANES = 128
ROW_TILE = 256
MOD_ROWS = 24
POOL_PAD = 16
VMEM_LIMIT = 52 * 1024 * 1024

OFF_Q, OFF_K, OFF_V, OFF_MK, OFF_MO, OFF_PP = 0, 512, 640, 768, 1024, 1280
IN_COLS = OFF_PP + POOL_WIDTH
ROW_MQ, ROW_MV, ROW_G = 0, 256, 512
T_ROWS = ROW_G + N_GATES
Q_SCALE = HEAD_DIM ** -0.5 * float(np.log2(np.e))
MLSTM_ROWS = 4

HEAD_PERM = np.concatenate(
    [np.concatenate([np.arange(c * 64, (c + 1) * 64), np.arange((4 + c) * 64, (5 + c) * 64)]) for c in range(4)])


def _sigmoid(x):
    return 1.0 / (1.0 + jnp.exp(-x))


def _split3(a):
    hi = a.astype(BF16)
    r1 = a - hi.astype(F32)
    mid = r1.astype(BF16)
    lo = (r1 - mid.astype(F32)).astype(BF16)
    return hi, mid, lo


def _mod_kernel(c_ref, w_ref, b_ref, o_ref):
    c = c_ref[...]
    s = c * _sigmoid(c)
    s_hi, s_mid, s_lo = _split3(s)
    w_hi, w_mid, w_lo = _split3(w_ref[...])
    dot = functools.partial(jnp.dot, preferred_element_type=F32)
    acc = dot(s_lo, w_hi) + dot(s_hi, w_lo) + dot(s_mid, w_mid)
    acc = acc + dot(s_mid, w_hi) + dot(s_hi, w_mid)
    acc = acc + dot(s_hi, w_hi)
    o_ref[...] = acc + b_ref[...]


def _modulation(c_all, w_ada, b_ada):
    depth = w_ada.shape[0]
    tn = 1024
    return pl.pallas_call(
        _mod_kernel,
        out_shape=jax.ShapeDtypeStruct((depth, MOD_ROWS, 6 * D_MODEL), F32),
        grid=(depth, 6 * D_MODEL // tn),
        in_specs=[
            pl.BlockSpec((MOD_ROWS, D_MODEL), lambda l, j: (0, 0)),
            pl.BlockSpec((None, D_MODEL, tn), lambda l, j: (l, 0, j)),
            pl.BlockSpec((None, 1, tn), lambda l, j: (l, 0, j)),
        ],
        out_specs=pl.BlockSpec((None, MOD_ROWS, tn), lambda l, j: (l, 0, j)),
        name="adaln_mod",
    )(c_all, w_ada, b_ada.reshape(depth, 1, 6 * D_MODEL))


def _segment_ms(x, lo_mask):
    x2 = x * x
    s_lo = jnp.sum(jnp.where(lo_mask, x2, 0.0), axis=-1, keepdims=True)
    s_hi = jnp.sum(jnp.where(lo_mask, 0.0, x2), axis=-1, keepdims=True)
    return jnp.where(lo_mask, s_lo, s_hi) * (1.0 / HEAD_DIM)


def _inproj_kernel(x_ref, mod_ref, nm_ref, w_ref, wt_ref, bg_ref, qn_ref, kn_ref, cos_ref, sa_ref, sb_ref,
                   q_ref, k_ref, v_ref, mk_ref, mo_ref, pp_ref, mqt_ref, mvt_ref, g_ref):
    x = x_ref[...]
    ms = jnp.mean(x * x, axis=-1, keepdims=True)
    y = x * lax.rsqrt(ms + EPS) * nm_ref[...]
    h = (y * (1.0 + mod_ref[1]) + mod_ref[0]).astype(BF16)
    acc = jnp.dot(h, w_ref[...], preferred_element_type=F32)
    acc_t = lax.dot_general(wt_ref[...], h, (((1,), (1,)), ((), ())), preferred_element_type=F32)

    rows = x.shape[0]
    lo_mask = lax.broadcasted_iota(jnp.int32, (rows, LANES), 1) < HEAD_DIM
    cos, sa, sb = cos_ref[...], sa_ref[...], sb_ref[...]

    def norm_rope(xc, gain):
        yc = xc * lax.rsqrt(_segment_ms(xc, lo_mask) + EPS) * gain
        return yc * cos + pltpu.roll(yc, LANES - 16, 1) * sa + pltpu.roll(yc, 16, 1) * sb

    for c in range(ATT_WIDTH // LANES):
        qc = norm_rope(acc[:, OFF_Q + c * LANES:OFF_Q + (c + 1) * LANES], qn_ref[...])
        q_ref[:, c * LANES:(c + 1) * LANES] = (qc * Q_SCALE).astype(BF16)
    k_ref[...] = norm_rope(acc[:, OFF_K:OFF_K + LANES], kn_ref[...]).astype(BF16)
    v_ref[...] = acc[:, OFF_V:OFF_V + KV_WIDTH].astype(BF16)
    mk_ref[...] = acc[:, OFF_MK:OFF_MK + ML_WIDTH].astype(BF16)
    mo_ref[...] = acc[:, OFF_MO:OFF_MO + ML_WIDTH].astype(BF16)
    pp_ref[...] = acc[:, OFF_PP:OFF_PP + POOL_WIDTH].astype(BF16)
    mqt_ref[...] = acc_t[ROW_MQ:ROW_MQ + ML_WIDTH, :].astype(BF16)
    mvt_ref[...] = acc_t[ROW_MV:ROW_MV + ML_WIDTH, :].astype(BF16)
    g_ref[...] = acc_t[ROW_G:ROW_G + N_GATES, :] + bg_ref[...]


def _mod_spec(layer, ctx_tiles, ctx_row, first=0):
    def idx(b, i):
        return (layer, jnp.where(i + first < ctx_tiles, ctx_row, b), 0, 0, 0)
    return pl.BlockSpec((None, None, 6, 1, D_MODEL), idx)


def _in_proj(x_all, mod5, layer, norm_mix, w_in_r, w_in_t, b_gates_b, qn, kn, cos_t, sa_t, sb_t, ctx_tiles):
    B, TA, _ = x_all.shape
    nt = TA // ROW_TILE
    row = lambda w: pl.BlockSpec((None, ROW_TILE, w), lambda b, i: (b, i, 0))
    col = lambda r: pl.BlockSpec((None, r, ROW_TILE), lambda b, i: (b, 0, i))
    vec = lambda w: pl.BlockSpec((1, w), lambda b, i: (0, 0))
    full = lambda r, c: pl.BlockSpec((r, c), lambda b, i: (0, 0))
    tab = pl.BlockSpec((ROW_TILE, LANES), lambda b, i: (i, 0))
    row_widths = (ATT_WIDTH, KV_WIDTH, KV_WIDTH, ML_WIDTH, ML_WIDTH, POOL_WIDTH)
    col_heights = (ML_WIDTH, ML_WIDTH)
    out_shapes = [jax.ShapeDtypeStruct((B, TA, w), BF16) for w in row_widths]
    out_shapes += [jax.ShapeDtypeStruct((B, r, TA), BF16) for r in col_heights]
    out_shapes.append(jax.ShapeDtypeStruct((B, N_GATES, TA), F32))
    out_specs = [row(w) for w in row_widths] + [col(r) for r in col_heights] + [col(N_GATES)]
    return pl.pallas_call(
        _inproj_kernel,
        out_shape=out_shapes,
        grid=(B, nt),
        in_specs=[row(D_MODEL), _mod_spec(layer, ctx_tiles, B), vec(D_MODEL), full(D_MODEL, IN_COLS),
                  full(T_ROWS, D_MODEL), full(N_GATES, ROW_TILE), vec(LANES), vec(LANES), tab, tab, tab],
        out_specs=out_specs,
        compiler_params=pltpu.CompilerParams(vmem_limit_bytes=VMEM_LIMIT),
        name="in_proj",
    )(x_all, mod5, norm_mix, w_in_r, w_in_t, b_gates_b, qn, kn, cos_t, sa_t, sb_t)


def _attn_kernel(q_ref, k_ref, v_ref, o_ref, k0_ref, k1_ref, va_ref, *, ctx_len, first_tile):
    i = pl.program_id(1)
    total = k_ref.shape[0]

    @pl.when(i == 0)
    def _():
        k = k_ref[...]
        lo = lax.broadcasted_iota(jnp.int32, k.shape, 1) < HEAD_DIM
        zero = jnp.zeros_like(k)
        k0_ref[...] = jnp.where(lo, k, zero)
        k1_ref[...] = jnp.where(lo, zero, k)
        va_ref[:, 0:LANES] = v_ref[...]
        va_ref[:, LANES:2 * LANES] = jnp.ones((total, LANES), BF16)

    rows = q_ref.shape[0]
    lo_mask = lax.broadcasted_iota(jnp.int32, (rows, LANES), 1) < HEAD_DIM

    def block(nk):
        for c in range(ATT_WIDTH // LANES):
            qc = q_ref[:, c * LANES:(c + 1) * LANES]
            outs = []
            for kr in (k0_ref, k1_ref):
                s = lax.dot_general(qc, kr[0:nk, :], (((1,), (1,)), ((), ())), preferred_element_type=F32)
                m = jnp.max(s, axis=-1, keepdims=True)
                p = jnp.exp2(s - m).astype(BF16)
                a = jnp.dot(p, va_ref[0:nk, :], preferred_element_type=F32)
                outs.append(a[:, 0:LANES] / a[:, LANES:2 * LANES])
            o_ref[:, c * LANES:(c + 1) * LANES] = jnp.where(lo_mask, outs[0], outs[1]).astype(BF16)

    if first_tile == 0:
        pl.when(i == 0)(lambda: block(ctx_len))
        pl.when(i > 0)(lambda: block(total))
    else:
        block(total)


def _attention(q, k, v, ctx_len, with_ctx):
    B, TA, _ = q.shape
    first = 0 if with_ctx else ctx_len // ROW_TILE
    nt = TA // ROW_TILE - first
    kern = functools.partial(_attn_kernel, ctx_len=ctx_len, first_tile=first)
    return pl.pallas_call(
        kern,
        out_shape=jax.ShapeDtypeStruct((B, nt * ROW_TILE, ATT_WIDTH), BF16),
        grid=(B, nt),
        in_specs=[pl.BlockSpec((None, ROW_TILE, ATT_WIDTH), lambda b, i: (b, i + first, 0)),
                  pl.BlockSpec((None, TA, KV_WIDTH), lambda b, i: (b, 0, 0)),
                  pl.BlockSpec((None, TA, KV_WIDTH), lambda b, i: (b, 0, 0))],
        out_specs=pl.BlockSpec((None, ROW_TILE, ATT_WIDTH), lambda b, i: (b, i, 0)),
        scratch_shapes=[pltpu.VMEM((TA, KV_WIDTH), BF16), pltpu.VMEM((TA, KV_WIDTH), BF16),
                        pltpu.VMEM((TA, 2 * LANES), BF16)],
        compiler_params=pltpu.CompilerParams(vmem_limit_bytes=VMEM_LIMIT),
        name="attention",
    )(q, k, v)


def _mlstm_kernel(qf_ref, kf_ref, vf_ref, gf_ref, qb_ref, kb_ref, vb_ref, gb_ref, tri_ref, hf_ref, hb_ref,
                  st_ref, m_ref, r_ref):
    i = pl.program_id(1)

    @pl.when(i == 0)
    def _():
        st_ref[...] = jnp.zeros_like(st_ref)
        m_ref[...] = jnp.zeros_like(m_ref)
        r_ref[...] = jnp.zeros_like(r_ref)

    L = CHUNK
    s_idx = lax.broadcasted_iota(jnp.int32, (L, L), 0)
    t_idx = lax.broadcasted_iota(jnp.int32, (L, L), 1)
    first_head_rows = lax.broadcasted_iota(jnp.int32, (LANES, L), 0) < HEAD_DIM
    ones_blk = jnp.where(lax.broadcasted_iota(jnp.int32, (HEAD_DIM, L), 0) == 0, 1.0, 0.0).astype(BF16)

    dirs = ((qf_ref, kf_ref, vf_ref, gf_ref, hf_ref), (qb_ref, kb_ref, vb_ref, gb_ref, hb_ref))
    chains = [divmod(bd, 2) for bd in range(2 * hf_ref.shape[0])]
    units = [(bd, hd) for bd in range(len(chains)) for hd in range(ML_HEADS)]
    refs = [tuple(ref.at[bb] for ref in dirs[d]) for bb, d in chains]

    qm, kp, va, st, s_raw, a1 = {}, {}, {}, {}, {}, {}
    for bd, hd in units:
        q_r, k_r, v_r = refs[bd][0:3]
        pair, second = divmod(hd, 2)
        kp[bd, hd] = k_r[:, pair * LANES:(pair + 1) * LANES]
        q_pair = q_r[pair * LANES:(pair + 1) * LANES, :]
        keep = jnp.logical_not(first_head_rows) if second else first_head_rows
        qm[bd, hd] = jnp.where(keep, q_pair, jnp.zeros_like(q_pair))
        va[bd, hd] = jnp.concatenate([v_r[hd * HEAD_DIM:(hd + 1) * HEAD_DIM, :], ones_blk], axis=0)
        st[bd, hd] = st_ref[bd * ML_HEADS + hd]
        s_raw[bd, hd] = jnp.dot(kp[bd, hd], qm[bd, hd], preferred_element_type=F32)
        a1[bd, hd] = jnp.dot(st[bd, hd].astype(BF16), qm[bd, hd], preferred_element_type=F32)

    g8 = [refs[bd][3][8 * d:8 * d + 8, :] for bd, (bb, d) in enumerate(chains)]
    lf8 = [jnp.minimum(g, 0.0) - jnp.log1p(jnp.exp(-jnp.abs(g))) for g in g8]
    parts = [jnp.concatenate([p.astype(F32) for p in _split3(lf)], axis=0) for lf in lf8]
    cum = [jnp.dot(p, tri_ref[d], preferred_element_type=F32) for p, (bb, d) in zip(parts, chains)]
    cum = [c[16:24] + c[8:16] + c[0:8] for c in cum]
    b8 = [c[:, 0:L] for c in cum]
    btot8 = [c[:, L:2 * L] for c in cum]
    li8 = [pltpu.roll(g, ML_HEADS, 0) for g in g8]
    for bd in range(len(chains)):
        r_ref[bd, 0:8, :] = li8[bd] - b8[bd]
    ct = [r_ref[bd].T for bd in range(len(chains))]
    m_prev8 = [m_ref[bd] for bd in range(len(chains))]
    gg8 = [bt - b + li for bt, b, li in zip(btot8, b8, li8)]
    m_new8 = [jnp.maximum(bt + mp, jnp.max(gg, axis=-1, keepdims=True)) for bt, mp, gg in zip(btot8, m_prev8, gg8)]
    a8 = [jnp.exp(bt + mp - mn) for bt, mp, mn in zip(btot8, m_prev8, m_new8)]
    wk8 = [jnp.exp(gg - mn) * (HEAD_DIM ** -0.5) for gg, mn in zip(gg8, m_new8)]
    inter8 = [b + mp for b, mp in zip(b8, m_prev8)]
    for bd in range(len(chains)):
        m_ref[bd] = m_new8[bd]

    dmat, m_t, s_t, a2, upd = {}, {}, {}, {}, {}
    for bd, hd in units:
        r = ML_HEADS + hd
        causal = (s_idx <= t_idx) if chains[bd][1] == 0 else (s_idx >= t_idx)
        dmat[bd, hd] = jnp.where(causal, ct[bd][:, r:r + 1] + b8[bd][r:r + 1], -jnp.inf)
        m_t[bd, hd] = jnp.maximum(inter8[bd][r:r + 1], jnp.max(dmat[bd, hd], axis=0, keepdims=True))
    for bd, hd in units:
        s_t[bd, hd] = (s_raw[bd, hd] * (HEAD_DIM ** -0.5) * jnp.exp(dmat[bd, hd] - m_t[bd, hd])).astype(BF16)
    for bd, hd in units:
        r = ML_HEADS + hd
        a2[bd, hd] = jnp.dot(va[bd, hd], s_t[bd, hd], preferred_element_type=F32)
        vw = (va[bd, hd].astype(F32) * wk8[bd][r:r + 1]).astype(BF16)
        upd[bd, hd] = jnp.dot(vw, kp[bd, hd], preferred_element_type=F32)
    hts = {}
    for bd, hd in units:
        r = ML_HEADS + hd
        inter = inter8[bd][r:r + 1]
        nd = jnp.exp(inter - m_t[bd, hd]) * a1[bd, hd] + a2[bd, hd]
        den = nd[HEAD_DIM:HEAD_DIM + 1, :]
        hts[bd, hd] = nd[0:HEAD_DIM, :] / jnp.maximum(jnp.abs(den), jnp.exp(-m_t[bd, hd]))
        st_ref[bd * ML_HEADS + hd] = a8[bd][r:r + 1] * st[bd, hd] + upd[bd, hd]
    for bd in range(len(chains)):
        refs[bd][4][...] = jnp.concatenate([hts[bd, hd] for hd in range(ML_HEADS)], axis=0).T.astype(BF16)


def _mlstm(mqt, mk, mvt, gates, tri, ctx_len):
    B, TA, _ = mk.shape
    nc = TA // CHUNK
    ncc = ctx_len // CHUNK

    def fwd(i):
        return i

    def bwd(i):
        return jnp.where(i < ncc, ncc - 1 - i, nc - 1 - (i - ncc))

    nb = MLSTM_ROWS if B % MLSTM_ROWS == 0 else 1
    tok = lambda f: pl.BlockSpec((nb, CHUNK, ML_WIDTH), lambda b, i: (b, f(i), 0))
    feat = lambda f, r: pl.BlockSpec((nb, r, CHUNK), lambda b, i: (b, 0, f(i)))
    n_units = 2 * nb * ML_HEADS
    return pl.pallas_call(
        _mlstm_kernel,
        out_shape=[jax.ShapeDtypeStruct((B, TA, ML_WIDTH), BF16)] * 2,
        grid=(B // nb, nc),
        in_specs=[feat(fwd, ML_WIDTH), tok(fwd), feat(fwd, ML_WIDTH), feat(fwd, N_GATES),
                  feat(bwd, ML_WIDTH), tok(bwd), feat(bwd, ML_WIDTH), feat(bwd, N_GATES),
                  pl.BlockSpec((2, CHUNK, 2 * CHUNK), lambda b, i: (0, 0, 0))],
        out_specs=[tok(fwd), tok(bwd)],
        scratch_shapes=[pltpu.VMEM((n_units, LANES, LANES), F32), pltpu.VMEM((2 * nb, 8, LANES), F32),
                        pltpu.VMEM((2 * nb, CHUNK, LANES), F32)],
        compiler_params=pltpu.CompilerParams(vmem_limit_bytes=VMEM_LIMIT),
        name="mlstm",
    )(mqt, mk, mvt, gates, mqt, mk, mvt, gates, tri)


def _pool_kernel(pp_ref, bd_ref, ps_ref, o_ref, p_ref, s2_ref, s4_ref, s8_ref, *, ctx_len):
    total = pp_ref.shape[0]
    seqs = ((0, ctx_len, POOL_PAD), (ctx_len, total - ctx_len, 2 * POOL_PAD + ctx_len))
    rows = p_ref.shape[0]
    for ref in (p_ref, s2_ref, s4_ref, s8_ref):
        ref[...] = jnp.zeros_like(ref)
    for src, n, dst in seqs:
        p_ref[dst:dst + n, :] = pp_ref[src:src + n, :].astype(F32)
    lo, m = 8, rows - 16
    s2_ref[lo:lo + m, :] = p_ref[lo - 1:lo - 1 + m, :] + p_ref[lo:lo + m, :]
    s4_ref[lo:lo + m, :] = s2_ref[lo - 1:lo - 1 + m, :] + s2_ref[lo + 1:lo + 1 + m, :]
    s8_ref[lo:lo + m, :] = s4_ref[lo - 2:lo - 2 + m, :] + s4_ref[lo + 2:lo + 2 + m, :]
    for src, n, dst in seqs:
        lane = lax.broadcasted_iota(jnp.int32, (n, POOL_WIDTH), 1)
        t = lax.broadcasted_iota(jnp.int32, (n, POOL_WIDTH), 0)
        g0, g1, g2 = lane < POOL_GROUP_DIM, lane < 2 * POOL_GROUP_DIM, lane < 3 * POOL_GROUP_DIM
        half = jnp.where(g0, 1, jnp.where(g1, 2, jnp.where(g2, 4, 8)))
        cnt = (jnp.minimum(t + half, n) - jnp.maximum(t - half, 0)).astype(F32)
        s16 = s8_ref[dst - 4:dst - 4 + n, :] + s8_ref[dst + 4:dst + 4 + n, :]
        wsum = jnp.where(g0, s2_ref[dst:dst + n, :],
                         jnp.where(g1, s4_ref[dst:dst + n, :], jnp.where(g2, s8_ref[dst:dst + n, :], s16)))
        p = wsum / cnt - p_ref[dst:dst + n, :]
        y = jnp.dot(p.astype(BF16), bd_ref[...], preferred_element_type=F32) * ps_ref[...]
        o_ref[src:src + n, :] = y.astype(BF16)


def _pool(pp, pool_bd, pool_scale, ctx_len):
    B, TA, _ = pp.shape
    rows = TA + 3 * POOL_PAD
    kern = functools.partial(_pool_kernel, ctx_len=ctx_len)
    full = pl.BlockSpec((None, TA, POOL_WIDTH), lambda b: (b, 0, 0))
    return pl.pallas_call(
        kern,
        out_shape=jax.ShapeDtypeStruct((B, TA, POOL_WIDTH), BF16),
        grid=(B,),
        in_specs=[full, pl.BlockSpec((POOL_WIDTH, POOL_WIDTH), lambda b: (0, 0)),
                  pl.BlockSpec((1, POOL_WIDTH), lambda b: (0, 0))],
        out_specs=full,
        scratch_shapes=[pltpu.VMEM((rows, POOL_WIDTH), F32)] * 4,
        compiler_params=pltpu.CompilerParams(vmem_limit_bytes=VMEM_LIMIT),
        name="pool_mix",
    )(pp, pool_bd, pool_scale)


def _mixffn_kernel(x_ref, att_ref, hf_ref, hb_ref, mo_ref, pool_ref, mod_ref, mln_ref, wo_ref, nf_ref,
                   wg_ref, wu_ref, wd_ref, fn_ref, o_ref, *, final):
    rows = x_ref.shape[0]
    lo_mask = lax.broadcasted_iota(jnp.int32, (rows, LANES), 1) < HEAD_DIM
    ml = []
    for c in range(ML_WIDTH // LANES):
        sl = slice(c * LANES, (c + 1) * LANES)
        hs = hf_ref[:, sl].astype(F32) + hb_ref[:, sl].astype(F32)
        hn = hs * lax.rsqrt(_segment_ms(hs, lo_mask) + EPS) * mln_ref[:, sl]
        ml.append((hn * _sigmoid(mo_ref[:, sl].astype(F32))).astype(BF16))
    cat = jnp.concatenate([att_ref[...]] + ml + [pool_ref[...]], axis=-1)
    mix = jnp.dot(cat, wo_ref[...], preferred_element_type=F32)
    x1 = x_ref[...] + mod_ref[2] * mix
    ms = jnp.mean(x1 * x1, axis=-1, keepdims=True)
    h2 = (x1 * lax.rsqrt(ms + EPS) * nf_ref[...] * (1.0 + mod_ref[4]) + mod_ref[3]).astype(BF16)
    gate = jnp.dot(h2, wg_ref[...], preferred_element_type=F32)
    up = jnp.dot(h2, wu_ref[...], preferred_element_type=F32)
    act = (gate * _sigmoid(gate) * up).astype(BF16)
    out = x1 + mod_ref[5] * jnp.dot(act, wd_ref[...], preferred_element_type=F32)
    if final:
        ms = jnp.mean(out * out, axis=-1, keepdims=True)
        out = out * lax.rsqrt(ms + EPS) * fn_ref[...]
    o_ref[...] = out


def _mix_ffn(x_all, att, hf, hb, mo, pool, mod5, layer, ml_norm, w_out_r, norm_ffn, wg, wu, wd, final_norm,
             ctx_tiles, final):
    B, TA, _ = x_all.shape
    first = ctx_tiles if final else 0
    nt = TA // ROW_TILE - first
    row = lambda w: pl.BlockSpec((None, ROW_TILE, w), lambda b, i: (b, i + first, 0))
    vec = lambda w: pl.BlockSpec((1, w), lambda b, i: (0, 0))
    wspec = lambda r, c: pl.BlockSpec((r, c), lambda b, i: (0, 0), pipeline_mode=pl.Buffered(1))

    kern = functools.partial(_mixffn_kernel, final=final)
    return pl.pallas_call(
        kern,
        out_shape=jax.ShapeDtypeStruct((B, nt * ROW_TILE, D_MODEL), F32),
        grid=(B, nt),
        in_specs=[row(D_MODEL), pl.BlockSpec((None, ROW_TILE, ATT_WIDTH), lambda b, i: (b, i, 0)),
                  row(ML_WIDTH), row(ML_WIDTH), row(ML_WIDTH), row(POOL_WIDTH),
                  _mod_spec(layer, ctx_tiles, B, first), vec(ML_WIDTH),
                  wspec(D_MODEL, D_MODEL), vec(D_MODEL), wspec(D_MODEL, D_FF), wspec(D_MODEL, D_FF),
                  wspec(D_FF, D_MODEL), vec(D_MODEL)],
        out_specs=pl.BlockSpec((None, ROW_TILE, D_MODEL), lambda b, i: (b, i, 0)),
        compiler_params=pltpu.CompilerParams(vmem_limit_bytes=VMEM_LIMIT),
        name="mix_ffn",
    )(x_all, att, hf, hb, mo, pool, mod5, ml_norm, w_out_r, norm_ffn, wg, wu, wd, final_norm)


def _rope_tables(t_latent, ctx_len):
    half = HEAD_DIM // 2
    inv_freq = 1.0 / (ROPE_THETA ** (jnp.arange(0, half, 2, dtype=F32) / half))
    rows = t_latent // GRID_W
    row = jnp.repeat(jnp.arange(rows, dtype=F32), GRID_W)
    col = jnp.tile(jnp.arange(GRID_W, dtype=F32), rows)
    a_row = row[:, None] * inv_freq
    a_col = col[:, None] * inv_freq
    ang = jnp.concatenate([a_row, a_row, a_col, a_col], -1)
    cos, sin = jnp.cos(ang), jnp.sin(ang)
    first_half = (jnp.arange(HEAD_DIM) % 32) < 16
    sa = jnp.where(first_half, -sin, 0.0)
    sb = jnp.where(first_half, 0.0, sin)
    pad = lambda a, fill: jnp.tile(jnp.concatenate([jnp.full((ctx_len, HEAD_DIM), fill, F32), a], 0), (1, 2))
    return pad(cos, 1.0), pad(sa, 0.0), pad(sb, 0.0)


def _rearrange_w_in(w_in):
    aq, ak, av, mq, mk, mv, mo, mg, pp = jnp.split(w_in, (512, 640, 768, 1024, 1280, 1536, 1792, 1808), axis=-1)
    w_rows = jnp.concatenate([aq[:, HEAD_PERM], ak, av, mk, mo, pp], axis=-1).astype(BF16)
    w_feat = jnp.concatenate([mq, mv, mg], axis=-1).T.astype(BF16)
    return w_rows, w_feat


def _cumsum_matrices():
    s = np.arange(CHUNK)[:, None]
    t = np.arange(CHUNK)[None, :]
    ones = np.ones((CHUNK, CHUNK), np.float32)
    prefix = np.concatenate([(s <= t).astype(np.float32), ones], axis=1)
    suffix = np.concatenate([(s >= t).astype(np.float32), ones], axis=1)
    return jnp.asarray(np.stack([prefix, suffix]))


def kernel(x, c, ctx, c_ctx, w_ada, b_ada, norm_mix, w_in, b_gates, q_norm, k_norm, ml_norm, pool_w, pool_scale,
           w_out, norm_ffn, w_ffn_gate, w_ffn_up, w_ffn_down, final_norm):
    B, T, _ = x.shape
    ctx_len = ctx.shape[1]
    depth = w_ada.shape[0]
    assert ctx_len % ROW_TILE == 0 and T % ROW_TILE == 0 and B + 1 <= MOD_ROWS
    ctx_tiles = ctx_len // ROW_TILE

    c_all = jnp.concatenate([c, c_ctx[None, :], jnp.zeros((MOD_ROWS - B - 1, D_MODEL), F32)], 0)
    mod = _modulation(c_all, w_ada, b_ada)
    mod5 = mod.reshape(depth, MOD_ROWS, 6, 1, D_MODEL)
    cos_t, sa_t, sb_t = _rope_tables(T, ctx_len)
    tri = _cumsum_matrices()
    vec = lambda a: a.reshape(1, -1)

    x_all = jnp.concatenate([ctx, x], axis=1)
    for l in range(depth):
        final = l == depth - 1
        w_in_r, w_in_t = _rearrange_w_in(w_in[l])
        b_gates_b = jnp.broadcast_to(b_gates[l][:, None], (N_GATES, ROW_TILE))
        qn = jnp.tile(q_norm[l], 2).reshape(1, LANES)
        kn = jnp.tile(k_norm[l], 2).reshape(1, LANES)
        pool_bd = jax.scipy.linalg.block_diag(*[pool_w[l, g] for g in range(len(POOL_WINDOWS))]).astype(BF16)
        w_out_r = jnp.concatenate([w_out[l, :ATT_WIDTH][HEAD_PERM], w_out[l, ATT_WIDTH:]], 0).astype(BF16)

        q, k, v, mk, mo, pp, mqt, mvt, gates = _in_proj(
            x_all, mod5, l, vec(norm_mix[l]), w_in_r, w_in_t, b_gates_b, qn, kn, cos_t, sa_t, sb_t, ctx_tiles)
        att = _attention(q, k, v, ctx_len, with_ctx=not final)
        hf, hb = _mlstm(mqt, mk, mvt, gates, tri, ctx_len)
        pool = _pool(pp, pool_bd, vec(pool_scale[l]), ctx_len)
        x_all = _mix_ffn(x_all, att, hf, hb, mo, pool, mod5, l, vec(ml_norm[l]), w_out_r, vec(norm_ffn[l]),
                         w_ffn_gate[l].astype(BF16), w_ffn_up[l].astype(BF16), w_ffn_down[l].astype(BF16),
                         vec(final_norm), ctx_tiles, final)
    return x_all
```

```python
import functools

import jax
import jax.numpy as jnp
import numpy as np
from jax import lax
from jax.experimental import pallas as pl
from jax.experimental.pallas import tpu as pltpu

F32 = jnp.float32
BF16 = jnp.bfloat16

D_MODEL = 1024
HEAD_DIM = 64
GRID_W = 64
ATT_WIDTH = 512
KV_WIDTH = 128
ML_WIDTH = 256
POOL_WIDTH = 256
ATT_HEADS = 8
ML_HEADS = 4
N_GATES = 16
POOL_WINDOWS = (2, 4, 8, 16)
POOL_GROUP_DIM = 64
D_FF = 2816
CHUNK = 128
ROPE_THETA = 10000.0
EPS = 1e-6

LANES = 128
ROW_TILE = 256
MOD_ROWS = 24
POOL_PAD = 16
VMEM_LIMIT = 52 * 1024 * 1024

OFF_Q, OFF_K, OFF_V, OFF_MK, OFF_MO, OFF_PP = 0, 512, 640, 768, 1024, 1280
IN_COLS = OFF_PP + POOL_WIDTH
ROW_MQ, ROW_MV, ROW_G = 0, 256, 512
T_ROWS = ROW_G + N_GATES
Q_SCALE = HEAD_DIM ** -0.5 * float(np.log2(np.e))
MLSTM_ROWS = 8
STEP_ROWS = 2
FF_SPLITS = ((0, 1536), (1536, D_FF))

HEAD_PERM = np.concatenate(
    [np.concatenate([np.arange(c * 64, (c + 1) * 64), np.arange((4 + c) * 64, (5 + c) * 64)]) for c in range(4)])


def _sigmoid(x):
    return 1.0 / (1.0 + jnp.exp(-x))


def _split3(a):
    hi = a.astype(BF16)
    r1 = a - hi.astype(F32)
    mid = r1.astype(BF16)
    lo = (r1 - mid.astype(F32)).astype(BF16)
    return hi, mid, lo


def _mod_kernel(c_ref, w_ref, b_ref, o_ref):
    c = c_ref[...]
    s = c * _sigmoid(c)
    s_hi, s_mid, s_lo = _split3(s)
    w_hi, w_mid, w_lo = _split3(w_ref[...])
    dot = functools.partial(jnp.dot, preferred_element_type=F32)
    acc = dot(s_lo, w_hi) + dot(s_hi, w_lo) + dot(s_mid, w_mid)
    acc = acc + dot(s_mid, w_hi) + dot(s_hi, w_mid)
    acc = acc + dot(s_hi, w_hi)
    o_ref[...] = acc + b_ref[...]


def _modulation(c_all, w_ada, b_ada):
    depth = w_ada.shape[0]
    tn = 1024
    return pl.pallas_call(
        _mod_kernel,
        out_shape=jax.ShapeDtypeStruct((depth, MOD_ROWS, 6 * D_MODEL), F32),
        grid=(depth, 6 * D_MODEL // tn),
        in_specs=[
            pl.BlockSpec((MOD_ROWS, D_MODEL), lambda l, j: (0, 0)),
            pl.BlockSpec((None, D_MODEL, tn), lambda l, j: (l, 0, j)),
            pl.BlockSpec((None, 1, tn), lambda l, j: (l, 0, j)),
        ],
        out_specs=pl.BlockSpec((None, MOD_ROWS, tn), lambda l, j: (l, 0, j)),
        name="adaln_mod",
    )(c_all, w_ada, b_ada.reshape(depth, 1, 6 * D_MODEL))


def _segment_ms(x, lo_mask):
    x2 = x * x
    s_lo = jnp.sum(jnp.where(lo_mask, x2, 0.0), axis=-1, keepdims=True)
    s_hi = jnp.sum(jnp.where(lo_mask, 0.0, x2), axis=-1, keepdims=True)
    return jnp.where(lo_mask, s_lo, s_hi) * (1.0 / HEAD_DIM)


def _inproj_kernel(*refs, ctx_tiles, split):
    if split:
        ctx_ref, x_ref = refs[0:2]
        refs = refs[2:]
        from_ctx = pl.program_id(1) < ctx_tiles
        load = lambda bb: jnp.where(from_ctx, ctx_ref[bb], x_ref[bb])
    else:
        x_ref = refs[0]
        refs = refs[1:]
        load = lambda bb: x_ref[bb]
    (mod_ref, nm_ref, w_ref, wt_ref, bg_ref, qn_ref, kn_ref, cos_ref, sa_ref, sb_ref,
     q_ref, k_ref, v_ref, mk_ref, mo_ref, pp_ref, mqt_ref, mvt_ref, g_ref) = refs
    nb, rows = q_ref.shape[0], q_ref.shape[1]

    hs = []
    for bb in range(nb):
        x = load(bb)
        ms = jnp.mean(x * x, axis=-1, keepdims=True)
        y = x * lax.rsqrt(ms + EPS) * nm_ref[...]
        hs.append((y * (1.0 + mod_ref[bb, 1]) + mod_ref[bb, 0]).astype(BF16))
    h = jnp.concatenate(hs, axis=0)
    acc = jnp.dot(h, w_ref[...], preferred_element_type=F32)
    acc_t = lax.dot_general(wt_ref[...], h, (((1,), (1,)), ((), ())), preferred_element_type=F32)

    lo_mask = lax.broadcasted_iota(jnp.int32, (rows, LANES), 1) < HEAD_DIM
    cos, sa, sb = cos_ref[...], sa_ref[...], sb_ref[...]

    def norm_rope(xc, gain):
        yc = xc * lax.rsqrt(_segment_ms(xc, lo_mask) + EPS) * gain
        return yc * cos + pltpu.roll(yc, LANES - 16, 1) * sa + pltpu.roll(yc, 16, 1) * sb

    for bb in range(nb):
        r0 = bb * rows
        blk = lambda off, width: acc[r0:r0 + rows, off:off + width]
        for c in range(ATT_WIDTH // LANES):
            qc = norm_rope(blk(OFF_Q + c * LANES, LANES), qn_ref[...])
            q_ref[bb, :, c * LANES:(c + 1) * LANES] = (qc * Q_SCALE).astype(BF16)
        k_ref[bb] = norm_rope(blk(OFF_K, LANES), kn_ref[...]).astype(BF16)
        v_ref[bb] = blk(OFF_V, KV_WIDTH).astype(BF16)
        mk_ref[bb] = blk(OFF_MK, ML_WIDTH).astype(BF16)
        mo_ref[bb] = blk(OFF_MO, ML_WIDTH).astype(BF16)
        pp_ref[bb] = blk(OFF_PP, POOL_WIDTH).astype(BF16)
        mqt_ref[bb] = acc_t[ROW_MQ:ROW_MQ + ML_WIDTH, r0:r0 + rows].astype(BF16)
        mvt_ref[bb] = acc_t[ROW_MV:ROW_MV + ML_WIDTH, r0:r0 + rows].astype(BF16)
        g_ref[bb] = acc_t[ROW_G:ROW_G + N_GATES, r0:r0 + rows] + bg_ref[...]


def _mod_spec(layer, nb, ctx_tiles, ctx_block, first=0):
    def idx(b, i):
        return (layer, jnp.where(i + first < ctx_tiles, ctx_block, b), 0, 0, 0)
    return pl.BlockSpec((None, nb, 6, 1, D_MODEL), idx)


def _token_specs(x_in, nb, ctx_tiles, first=0):
    if isinstance(x_in, tuple):
        ctx, x = x_in
        specs = [pl.BlockSpec((nb, ROW_TILE, D_MODEL), lambda b, i: (b, jnp.minimum(i, ctx_tiles - 1), 0)),
                 pl.BlockSpec((nb, ROW_TILE, D_MODEL), lambda b, i: (b, jnp.maximum(i - ctx_tiles, 0), 0))]
        return specs, [ctx, x], ctx.shape[0], ctx.shape[1] + x.shape[1]
    specs = [pl.BlockSpec((nb, ROW_TILE, D_MODEL), lambda b, i: (b, i + first, 0))]
    return specs, [x_in], x_in.shape[0], x_in.shape[1]


def _in_proj(x_in, mod5, layer, norm_mix, w_in_r, w_in_t, b_gates_b, qn, kn, cos_t, sa_t, sb_t, ctx_tiles):
    nb = STEP_ROWS
    x_specs, x_ops, B, TA = _token_specs(x_in, nb, ctx_tiles)
    nt = TA // ROW_TILE
    row = lambda w: pl.BlockSpec((nb, ROW_TILE, w), lambda b, i: (b, i, 0))
    col = lambda r: pl.BlockSpec((nb, r, ROW_TILE), lambda b, i: (b, 0, i))
    vec = lambda w: pl.BlockSpec((1, w), lambda b, i: (0, 0))
    full = lambda r, c: pl.BlockSpec((r, c), lambda b, i: (0, 0))
    tab = pl.BlockSpec((ROW_TILE, LANES), lambda b, i: (i, 0))
    row_widths = (ATT_WIDTH, KV_WIDTH, KV_WIDTH, ML_WIDTH, ML_WIDTH, POOL_WIDTH)
    col_heights = (ML_WIDTH, ML_WIDTH)
    out_shapes = [jax.ShapeDtypeStruct((B, TA, w), BF16) for w in row_widths]
    out_shapes += [jax.ShapeDtypeStruct((B, r, TA), BF16) for r in col_heights]
    out_shapes.append(jax.ShapeDtypeStruct((B, N_GATES, TA), F32))
    out_specs = [row(w) for w in row_widths] + [col(r) for r in col_heights] + [col(N_GATES)]
    kern = functools.partial(_inproj_kernel, ctx_tiles=ctx_tiles, split=len(x_ops) == 2)
    return pl.pallas_call(
        kern,
        out_shape=out_shapes,
        grid=(B // nb, nt),
        in_specs=x_specs + [_mod_spec(layer, nb, ctx_tiles, B // nb), vec(D_MODEL), full(D_MODEL, IN_COLS),
                            full(T_ROWS, D_MODEL), full(N_GATES, ROW_TILE), vec(LANES), vec(LANES), tab, tab, tab],
        out_specs=out_specs,
        compiler_params=pltpu.CompilerParams(vmem_limit_bytes=VMEM_LIMIT),
        name="in_proj",
    )(*x_ops, mod5, norm_mix, w_in_r, w_in_t, b_gates_b, qn, kn, cos_t, sa_t, sb_t)


def _attn_kernel(q_ref, k_ref, v_ref, o_ref, k0_ref, k1_ref, va_ref, *, ctx_len, first_tile):
    i = pl.program_id(1)
    total = k_ref.shape[0]

    @pl.when(i == 0)
    def _():
        k = k_ref[...]
        lo = lax.broadcasted_iota(jnp.int32, k.shape, 1) < HEAD_DIM
        zero = jnp.zeros_like(k)
        k0_ref[...] = jnp.where(lo, k, zero)
        k1_ref[...] = jnp.where(lo, zero, k)
        va_ref[:, 0:LANES] = v_ref[...]
        va_ref[:, LANES:2 * LANES] = jnp.ones((total, LANES), BF16)

    rows = q_ref.shape[0]
    lo_mask = lax.broadcasted_iota(jnp.int32, (rows, LANES), 1) < HEAD_DIM

    def block(nk):
        for c in range(ATT_WIDTH // LANES):
            qc = q_ref[:, c * LANES:(c + 1) * LANES]
            outs = []
            for kr in (k0_ref, k1_ref):
                s = lax.dot_general(qc, kr[0:nk, :], (((1,), (1,)), ((), ())), preferred_element_type=F32)
                m = jnp.max(s, axis=-1, keepdims=True)
                p = jnp.exp2(s - m).astype(BF16)
                a = jnp.dot(p, va_ref[0:nk, :], preferred_element_type=F32)
                outs.append(a[:, 0:LANES] / a[:, LANES:2 * LANES])
            o_ref[:, c * LANES:(c + 1) * LANES] = jnp.where(lo_mask, outs[0], outs[1]).astype(BF16)

    if first_tile == 0:
        pl.when(i == 0)(lambda: block(ctx_len))
        pl.when(i > 0)(lambda: block(total))
    else:
        block(total)


def _attention(q, k, v, ctx_len, with_ctx):
    B, TA, _ = q.shape
    first = 0 if with_ctx else ctx_len // ROW_TILE
    nt = TA // ROW_TILE - first
    kern = functools.partial(_attn_kernel, ctx_len=ctx_len, first_tile=first)
    return pl.pallas_call(
        kern,
        out_shape=jax.ShapeDtypeStruct((B, nt * ROW_TILE, ATT_WIDTH), BF16),
        grid=(B, nt),
        in_specs=[pl.BlockSpec((None, ROW_TILE, ATT_WIDTH), lambda b, i: (b, i + first, 0)),
                  pl.BlockSpec((None, TA, KV_WIDTH), lambda b, i: (b, 0, 0)),
                  pl.BlockSpec((None, TA, KV_WIDTH), lambda b, i: (b, 0, 0))],
        out_specs=pl.BlockSpec((None, ROW_TILE, ATT_WIDTH), lambda b, i: (b, i, 0)),
        scratch_shapes=[pltpu.VMEM((TA, KV_WIDTH), BF16), pltpu.VMEM((TA, KV_WIDTH), BF16),
                        pltpu.VMEM((TA, 2 * LANES), BF16)],
        compiler_params=pltpu.CompilerParams(vmem_limit_bytes=VMEM_LIMIT),
        name="attention",
    )(q, k, v)


def _mlstm_kernel(qf_ref, kf_ref, vf_ref, gf_ref, qb_ref, kb_ref, vb_ref, gb_ref, tri_ref, hf_ref, hb_ref,
                  st_ref, m_ref, r_ref):
    i = pl.program_id(1)

    @pl.when(i == 0)
    def _():
        st_ref[...] = jnp.zeros_like(st_ref)
        m_ref[...] = jnp.zeros_like(m_ref)
        r_ref[...] = jnp.zeros_like(r_ref)

    L = CHUNK
    s_idx = lax.broadcasted_iota(jnp.int32, (L, L), 0)
    t_idx = lax.broadcasted_iota(jnp.int32, (L, L), 1)
    first_head_rows = lax.broadcasted_iota(jnp.int32, (LANES, L), 0) < HEAD_DIM
    ones_blk = jnp.where(lax.broadcasted_iota(jnp.int32, (HEAD_DIM, L), 0) == 0, 1.0, 0.0).astype(BF16)

    dirs = ((qf_ref, kf_ref, vf_ref, gf_ref, hf_ref), (qb_ref, kb_ref, vb_ref, gb_ref, hb_ref))
    chains = [divmod(bd, 2) for bd in range(2 * hf_ref.shape[0])]
    units = [(bd, hd) for bd in range(len(chains)) for hd in range(ML_HEADS)]
    refs = [tuple(ref.at[bb] for ref in dirs[d]) for bb, d in chains]

    qm, kp, va, st, s_raw, a1 = {}, {}, {}, {}, {}, {}
    for bd, hd in units:
        q_r, k_r, v_r = refs[bd][0:3]
        pair, second = divmod(hd, 2)
        kp[bd, hd] = k_r[:, pair * LANES:(pair + 1) * LANES]
        q_pair = q_r[pair * LANES:(pair + 1) * LANES, :]
        keep = jnp.logical_not(first_head_rows) if second else first_head_rows
        qm[bd, hd] = jnp.where(keep, q_pair, jnp.zeros_like(q_pair))
        va[bd, hd] = jnp.concatenate([v_r[hd * HEAD_DIM:(hd + 1) * HEAD_DIM, :], ones_blk], axis=0)
        st[bd, hd] = st_ref[bd * ML_HEADS + hd]
        s_raw[bd, hd] = jnp.dot(kp[bd, hd], qm[bd, hd], preferred_element_type=F32)
        a1[bd, hd] = jnp.dot(st[bd, hd].astype(BF16), qm[bd, hd], preferred_element_type=F32)

    g8 = [refs[bd][3][8 * d:8 * d + 8, :] for bd, (bb, d) in enumerate(chains)]
    lf8 = [jnp.minimum(g, 0.0) - jnp.log1p(jnp.exp(-jnp.abs(g))) for g in g8]
    parts = [jnp.concatenate([p.astype(F32) for p in _split3(lf)], axis=0) for lf in lf8]
    cum = [jnp.dot(p, tri_ref[d], preferred_element_type=F32) for p, (bb, d) in zip(parts, chains)]
    cum = [c[16:24] + c[8:16] + c[0:8] for c in cum]
    b8 = [c[:, 0:L] for c in cum]
    btot8 = [c[:, L:2 * L] for c in cum]
    li8 = [pltpu.roll(g, ML_HEADS, 0) for g in g8]
    for bd in range(len(chains)):
        r_ref[bd, 0:8, :] = li8[bd] - b8[bd]
    ct = [r_ref[bd].T for bd in range(len(chains))]
    m_prev8 = [m_ref[bd] for bd in range(len(chains))]
    gg8 = [bt - b + li for bt, b, li in zip(btot8, b8, li8)]
    m_new8 = [jnp.maximum(bt + mp, jnp.max(gg, axis=-1, keepdims=True)) for bt, mp, gg in zip(btot8, m_prev8, gg8)]
    a8 = [jnp.exp(bt + mp - mn) for bt, mp, mn in zip(btot8, m_prev8, m_new8)]
    wk8 = [jnp.exp(gg - mn) * (HEAD_DIM ** -0.5) for gg, mn in zip(gg8, m_new8)]
    inter8 = [b + mp for b, mp in zip(b8, m_prev8)]
    for bd in range(len(chains)):
        m_ref[bd] = m_new8[bd]

    dmat, m_t, s_t, a2, upd = {}, {}, {}, {}, {}
    for bd, hd in units:
        r = ML_HEADS + hd
        causal = (s_idx <= t_idx) if chains[bd][1] == 0 else (s_idx >= t_idx)
        dmat[bd, hd] = jnp.where(causal, ct[bd][:, r:r + 1] + b8[bd][r:r + 1], -jnp.inf)
        m_t[bd, hd] = jnp.maximum(inter8[bd][r:r + 1], jnp.max(dmat[bd, hd], axis=0, keepdims=True))
    for bd, hd in units:
        s_t[bd, hd] = (s_raw[bd, hd] * (HEAD_DIM ** -0.5) * jnp.exp(dmat[bd, hd] - m_t[bd, hd])).astype(BF16)
    for bd, hd in units:
        r = ML_HEADS + hd
        a2[bd, hd] = jnp.dot(va[bd, hd], s_t[bd, hd], preferred_element_type=F32)
        vw = (va[bd, hd].astype(F32) * wk8[bd][r:r + 1]).astype(BF16)
        upd[bd, hd] = jnp.dot(vw, kp[bd, hd], preferred_element_type=F32)
    hts = {}
    for bd, hd in units:
        r = ML_HEADS + hd
        inter = inter8[bd][r:r + 1]
        nd = jnp.exp(inter - m_t[bd, hd]) * a1[bd, hd] + a2[bd, hd]
        den = nd[HEAD_DIM:HEAD_DIM + 1, :]
        hts[bd, hd] = nd[0:HEAD_DIM, :] / jnp.maximum(jnp.abs(den), jnp.exp(-m_t[bd, hd]))
        st_ref[bd * ML_HEADS + hd] = a8[bd][r:r + 1] * st[bd, hd] + upd[bd, hd]
    for bd in range(len(chains)):
        refs[bd][4][...] = jnp.concatenate([hts[bd, hd] for hd in range(ML_HEADS)], axis=0).T.astype(BF16)


def _mlstm(mqt, mk, mvt, gates, tri, ctx_len):
    B, TA, _ = mk.shape
    nc = TA // CHUNK
    ncc = ctx_len // CHUNK

    def fwd(i):
        return i

    def bwd(i):
        return jnp.where(i < ncc, ncc - 1 - i, nc - 1 - (i - ncc))

    nb = MLSTM_ROWS if B % MLSTM_ROWS == 0 else 1
    tok = lambda f: pl.BlockSpec((nb, CHUNK, ML_WIDTH), lambda b, i: (b, f(i), 0))
    feat = lambda f, r: pl.BlockSpec((nb, r, CHUNK), lambda b, i: (b, 0, f(i)))
    n_units = 2 * nb * ML_HEADS
    return pl.pallas_call(
        _mlstm_kernel,
        out_shape=[jax.ShapeDtypeStruct((B, TA, ML_WIDTH), BF16)] * 2,
        grid=(B // nb, nc),
        in_specs=[feat(fwd, ML_WIDTH), tok(fwd), feat(fwd, ML_WIDTH), feat(fwd, N_GATES),
                  feat(bwd, ML_WIDTH), tok(bwd), feat(bwd, ML_WIDTH), feat(bwd, N_GATES),
                  pl.BlockSpec((2, CHUNK, 2 * CHUNK), lambda b, i: (0, 0, 0))],
        out_specs=[tok(fwd), tok(bwd)],
        scratch_shapes=[pltpu.VMEM((n_units, LANES, LANES), F32), pltpu.VMEM((2 * nb, 8, LANES), F32),
                        pltpu.VMEM((2 * nb, CHUNK, LANES), F32)],
        compiler_params=pltpu.CompilerParams(vmem_limit_bytes=VMEM_LIMIT),
        name="mlstm",
    )(mqt, mk, mvt, gates, mqt, mk, mvt, gates, tri)


def _pool_kernel(pp_ref, bd_ref, ps_ref, o_ref, p_ref, s2_ref, s4_ref, s8_ref, *, ctx_len):
    total = pp_ref.shape[0]
    seqs = ((0, ctx_len, POOL_PAD), (ctx_len, total - ctx_len, 2 * POOL_PAD + ctx_len))
    rows = p_ref.shape[0]
    for ref in (p_ref, s2_ref, s4_ref, s8_ref):
        ref[...] = jnp.zeros_like(ref)
    for src, n, dst in seqs:
        p_ref[dst:dst + n, :] = pp_ref[src:src + n, :].astype(F32)
    lo, m = 8, rows - 16
    s2_ref[lo:lo + m, :] = p_ref[lo - 1:lo - 1 + m, :] + p_ref[lo:lo + m, :]
    s4_ref[lo:lo + m, :] = s2_ref[lo - 1:lo - 1 + m, :] + s2_ref[lo + 1:lo + 1 + m, :]
    s8_ref[lo:lo + m, :] = s4_ref[lo - 2:lo - 2 + m, :] + s4_ref[lo + 2:lo + 2 + m, :]
    for src, n, dst in seqs:
        lane = lax.broadcasted_iota(jnp.int32, (n, POOL_WIDTH), 1)
        t = lax.broadcasted_iota(jnp.int32, (n, POOL_WIDTH), 0)
        g0, g1, g2 = lane < POOL_GROUP_DIM, lane < 2 * POOL_GROUP_DIM, lane < 3 * POOL_GROUP_DIM
        half = jnp.where(g0, 1, jnp.where(g1, 2, jnp.where(g2, 4, 8)))
        cnt = (jnp.minimum(t + half, n) - jnp.maximum(t - half, 0)).astype(F32)
        s16 = s8_ref[dst - 4:dst - 4 + n, :] + s8_ref[dst + 4:dst + 4 + n, :]
        wsum = jnp.where(g0, s2_ref[dst:dst + n, :],
                         jnp.where(g1, s4_ref[dst:dst + n, :], jnp.where(g2, s8_ref[dst:dst + n, :], s16)))
        p = wsum / cnt - p_ref[dst:dst + n, :]
        y = jnp.dot(p.astype(BF16), bd_ref[...], preferred_element_type=F32) * ps_ref[...]
        o_ref[src:src + n, :] = y.astype(BF16)


def _pool(pp, pool_bd, pool_scale, ctx_len):
    B, TA, _ = pp.shape
    rows = TA + 3 * POOL_PAD
    kern = functools.partial(_pool_kernel, ctx_len=ctx_len)
    full = pl.BlockSpec((None, TA, POOL_WIDTH), lambda b: (b, 0, 0))
    return pl.pallas_call(
        kern,
        out_shape=jax.ShapeDtypeStruct((B, TA, POOL_WIDTH), BF16),
        grid=(B,),
        in_specs=[full, pl.BlockSpec((POOL_WIDTH, POOL_WIDTH), lambda b: (0, 0)),
                  pl.BlockSpec((1, POOL_WIDTH), lambda b: (0, 0))],
        out_specs=full,
        scratch_shapes=[pltpu.VMEM((rows, POOL_WIDTH), F32)] * 4,
        compiler_params=pltpu.CompilerParams(vmem_limit_bytes=VMEM_LIMIT),
        name="pool_mix",
    )(pp, pool_bd, pool_scale)


def _mixffn_kernel(*refs, ctx_tiles, split, final):
    if split:
        ctx_ref, x_ref = refs[0:2]
        refs = refs[2:]
        from_ctx = pl.program_id(1) < ctx_tiles
        load = lambda bb: jnp.where(from_ctx, ctx_ref[bb], x_ref[bb])
    else:
        x_ref = refs[0]
        refs = refs[1:]
        load = lambda bb: x_ref[bb]
    (att_ref, hf_ref, hb_ref, mo_ref, pool_ref, mod_ref, mln_ref, wo_ref, nf_ref,
     wg_ref, wu_ref, wd_ref, fn_ref, o_ref) = refs
    nb, rows = o_ref.shape[0], o_ref.shape[1]
    lo_mask = lax.broadcasted_iota(jnp.int32, (rows, LANES), 1) < HEAD_DIM

    cats = []
    for bb in range(nb):
        ml = []
        for c in range(ML_WIDTH // LANES):
            sl = slice(c * LANES, (c + 1) * LANES)
            hs = hf_ref[bb, :, sl].astype(F32) + hb_ref[bb, :, sl].astype(F32)
            hn = hs * lax.rsqrt(_segment_ms(hs, lo_mask) + EPS) * mln_ref[:, sl]
            ml.append((hn * _sigmoid(mo_ref[bb, :, sl].astype(F32))).astype(BF16))
        cats.append(jnp.concatenate([att_ref[bb]] + ml + [pool_ref[bb]], axis=-1))
    mix = jnp.dot(jnp.concatenate(cats, axis=0), wo_ref[...], preferred_element_type=F32)

    x1s, h2s = [], []
    for bb in range(nb):
        x1 = load(bb) + mod_ref[bb, 2] * mix[bb * rows:(bb + 1) * rows]
        ms = jnp.mean(x1 * x1, axis=-1, keepdims=True)
        h2s.append((x1 * lax.rsqrt(ms + EPS) * nf_ref[...] * (1.0 + mod_ref[bb, 4]) + mod_ref[bb, 3]).astype(BF16))
        x1s.append(x1)
    h2 = jnp.concatenate(h2s, axis=0)
    down = None
    for lo, hi in FF_SPLITS:
        gate = jnp.dot(h2, wg_ref[:, lo:hi], preferred_element_type=F32)
        up = jnp.dot(h2, wu_ref[:, lo:hi], preferred_element_type=F32)
        act = (gate * _sigmoid(gate) * up).astype(BF16)
        part = jnp.dot(act, wd_ref[lo:hi, :], preferred_element_type=F32)
        down = part if down is None else down + part
    for bb in range(nb):
        out = x1s[bb] + mod_ref[bb, 5] * down[bb * rows:(bb + 1) * rows]
        if final:
            ms = jnp.mean(out * out, axis=-1, keepdims=True)
            out = out * lax.rsqrt(ms + EPS) * fn_ref[...]
        o_ref[bb] = out


def _mix_ffn(x_in, att, hf, hb, mo, pool, mod5, layer, ml_norm, w_out_r, norm_ffn, wg, wu, wd, final_norm,
             ctx_tiles, final):
    nb = STEP_ROWS
    first = ctx_tiles if final else 0
    x_specs, x_ops, B, TA = _token_specs(x_in, nb, ctx_tiles, first)
    nt = TA // ROW_TILE - first
    row = lambda w: pl.BlockSpec((nb, ROW_TILE, w), lambda b, i: (b, i + first, 0))
    vec = lambda w: pl.BlockSpec((1, w), lambda b, i: (0, 0))
    wspec = lambda r, c: pl.BlockSpec((r, c), lambda b, i: (0, 0), pipeline_mode=pl.Buffered(1))

    kern = functools.partial(_mixffn_kernel, ctx_tiles=ctx_tiles, split=len(x_ops) == 2, final=final)
    return pl.pallas_call(
        kern,
        out_shape=jax.ShapeDtypeStruct((B, nt * ROW_TILE, D_MODEL), F32),
        grid=(B // nb, nt),
        in_specs=x_specs + [pl.BlockSpec((nb, ROW_TILE, ATT_WIDTH), lambda b, i: (b, i, 0)),
                            row(ML_WIDTH), row(ML_WIDTH), row(ML_WIDTH), row(POOL_WIDTH),
                            _mod_spec(layer, nb, ctx_tiles, B // nb, first), vec(ML_WIDTH),
                            wspec(D_MODEL, D_MODEL), vec(D_MODEL), wspec(D_MODEL, D_FF), wspec(D_MODEL, D_FF),
                            wspec(D_FF, D_MODEL), vec(D_MODEL)],
        out_specs=pl.BlockSpec((nb, ROW_TILE, D_MODEL), lambda b, i: (b, i, 0)),
        compiler_params=pltpu.CompilerParams(vmem_limit_bytes=VMEM_LIMIT),
        name="mix_ffn",
    )(*x_ops, att, hf, hb, mo, pool, mod5, ml_norm, w_out_r, norm_ffn, wg, wu, wd, final_norm)


def _rope_tables(t_latent, ctx_len):
    half = HEAD_DIM // 2
    inv_freq = 1.0 / (ROPE_THETA ** (jnp.arange(0, half, 2, dtype=F32) / half))
    rows = t_latent // GRID_W
    row = jnp.repeat(jnp.arange(rows, dtype=F32), GRID_W)
    col = jnp.tile(jnp.arange(GRID_W, dtype=F32), rows)
    a_row = row[:, None] * inv_freq
    a_col = col[:, None] * inv_freq
    ang = jnp.concatenate([a_row, a_row, a_col, a_col], -1)
    cos, sin = jnp.cos(ang), jnp.sin(ang)
    first_half = (jnp.arange(HEAD_DIM) % 32) < 16
    sa = jnp.where(first_half, -sin, 0.0)
    sb = jnp.where(first_half, 0.0, sin)
    pad = lambda a, fill: jnp.tile(jnp.concatenate([jnp.full((ctx_len, HEAD_DIM), fill, F32), a], 0), (1, 2))
    return pad(cos, 1.0), pad(sa, 0.0), pad(sb, 0.0)


def _rearrange_w_in(w_in):
    aq, ak, av, mq, mk, mv, mo, mg, pp = jnp.split(w_in, (512, 640, 768, 1024, 1280, 1536, 1792, 1808), axis=-1)
    w_rows = jnp.concatenate([aq[:, HEAD_PERM], ak, av, mk, mo, pp], axis=-1).astype(BF16)
    w_feat = jnp.concatenate([mq, mv, mg], axis=-1).T.astype(BF16)
    return w_rows, w_feat


def _cumsum_matrices():
    s = np.arange(CHUNK)[:, None]
    t = np.arange(CHUNK)[None, :]
    ones = np.ones((CHUNK, CHUNK), np.float32)
    prefix = np.concatenate([(s <= t).astype(np.float32), ones], axis=1)
    suffix = np.concatenate([(s >= t).astype(np.float32), ones], axis=1)
    return jnp.asarray(np.stack([prefix, suffix]))


def kernel(x, c, ctx, c_ctx, w_ada, b_ada, norm_mix, w_in, b_gates, q_norm, k_norm, ml_norm, pool_w, pool_scale,
           w_out, norm_ffn, w_ffn_gate, w_ffn_up, w_ffn_down, final_norm):
    B, T, _ = x.shape
    ctx_len = ctx.shape[1]
    depth = w_ada.shape[0]
    assert ctx_len % ROW_TILE == 0 and T % ROW_TILE == 0 and B % STEP_ROWS == 0 and B + STEP_ROWS <= MOD_ROWS
    ctx_tiles = ctx_len // ROW_TILE

    c_rep = jnp.broadcast_to(c_ctx[None, :], (STEP_ROWS, D_MODEL))
    c_all = jnp.concatenate([c, c_rep, jnp.zeros((MOD_ROWS - B - STEP_ROWS, D_MODEL), F32)], 0)
    mod = _modulation(c_all, w_ada, b_ada)
    mod5 = mod.reshape(depth, MOD_ROWS, 6, 1, D_MODEL)
    cos_t, sa_t, sb_t = _rope_tables(T, ctx_len)
    tri = _cumsum_matrices()
    vec = lambda a: a.reshape(1, -1)

    x_all = (ctx, x)
    for l in range(depth):
        final = l == depth - 1
        w_in_r, w_in_t = _rearrange_w_in(w_in[l])
        b_gates_b = jnp.broadcast_to(b_gates[l][:, None], (N_GATES, ROW_TILE))
        qn = jnp.tile(q_norm[l], 2).reshape(1, LANES)
        kn = jnp.tile(k_norm[l], 2).reshape(1, LANES)
        pool_bd = jax.scipy.linalg.block_diag(*[pool_w[l, g] for g in range(len(POOL_WINDOWS))]).astype(BF16)
        w_out_r = jnp.concatenate([w_out[l, :ATT_WIDTH][HEAD_PERM], w_out[l, ATT_WIDTH:]], 0).astype(BF16)

        q, k, v, mk, mo, pp, mqt, mvt, gates = _in_proj(
            x_all, mod5, l, vec(norm_mix[l]), w_in_r, w_in_t, b_gates_b, qn, kn, cos_t, sa_t, sb_t, ctx_tiles)
        att = _attention(q, k, v, ctx_len, with_ctx=not final)
        hf, hb = _mlstm(mqt, mk, mvt, gates, tri, ctx_len)
        pool = _pool(pp, pool_bd, vec(pool_scale[l]), ctx_len)
        x_all = _mix_ffn(x_all, att, hf, hb, mo, pool, mod5, l, vec(ml_norm[l]), w_out_r, vec(norm_ffn[l]),
                         w_ffn_gate[l].astype(BF16), w_ffn_up[l].astype(BF16), w_ffn_down[l].astype(BF16),
                         vec(final_norm), ctx_tiles, final)
    return x_all
```

```python
import functools

import jax
import jax.numpy as jnp
import numpy as np
from jax import lax
from jax.experimental import pallas as pl
from jax.experimental.pallas import tpu as pltpu

F32 = jnp.float32
BF16 = jnp.bfloat16

D_MODEL = 1024
HEAD_DIM = 64
GRID_W = 64
ATT_WIDTH = 512
KV_WIDTH = 128
ML_WIDTH = 256
POOL_WIDTH = 256
ATT_HEADS = 8
ML_HEADS = 4
N_GATES = 16
POOL_WINDOWS = (2, 4, 8, 16)
POOL_GROUP_DIM = 64
D_FF = 2816
CHUNK = 128
ROPE_THETA = 10000.0
EPS = 1e-6

LANES = 128
ROW_TILE = 256
MOD_ROWS = 24
POOL_PAD = 16
VMEM_LIMIT = 52 * 1024 * 1024

OFF_Q, OFF_K, OFF_V, OFF_MK, OFF_MO, OFF_PP = 0, 512, 640, 768, 1024, 1280
IN_COLS = OFF_PP + POOL_WIDTH
ROW_MQ, ROW_MV, ROW_G = 0, 256, 512
T_ROWS = ROW_G + N_GATES
Q_SCALE = HEAD_DIM ** -0.5 * float(np.log2(np.e))
ATT_GROUP = 2
MLSTM_ROWS = 8
STEP_ROWS = 2
FF_SPLITS = ((0, 1536), (1536, D_FF))

HEAD_PERM = np.concatenate(
    [np.concatenate([np.arange(c * 64, (c + 1) * 64), np.arange((4 + c) * 64, (5 + c) * 64)]) for c in range(4)])


def _sigmoid(x):
    return 1.0 / (1.0 + jnp.exp(-x))


def _split3(a):
    hi = a.astype(BF16)
    r1 = a - hi.astype(F32)
    mid = r1.astype(BF16)
    lo = (r1 - mid.astype(F32)).astype(BF16)
    return hi, mid, lo


def _mod_kernel(c_ref, w_ref, b_ref, o_ref):
    c = c_ref[...]
    s = c * _sigmoid(c)
    s_hi, s_mid, s_lo = _split3(s)
    w_hi, w_mid, w_lo = _split3(w_ref[...])
    dot = functools.partial(jnp.dot, preferred_element_type=F32)
    acc = dot(s_lo, w_hi) + dot(s_hi, w_lo) + dot(s_mid, w_mid)
    acc = acc + dot(s_mid, w_hi) + dot(s_hi, w_mid)
    acc = acc + dot(s_hi, w_hi)
    o_ref[...] = acc + b_ref[...]


def _modulation(c_all, w_ada, b_ada):
    depth = w_ada.shape[0]
    tn = 1024
    return pl.pallas_call(
        _mod_kernel,
        out_shape=jax.ShapeDtypeStruct((depth, MOD_ROWS, 6 * D_MODEL), F32),
        grid=(depth, 6 * D_MODEL // tn),
        in_specs=[
            pl.BlockSpec((MOD_ROWS, D_MODEL), lambda l, j: (0, 0)),
            pl.BlockSpec((None, D_MODEL, tn), lambda l, j: (l, 0, j)),
            pl.BlockSpec((None, 1, tn), lambda l, j: (l, 0, j)),
        ],
        out_specs=pl.BlockSpec((None, MOD_ROWS, tn), lambda l, j: (l, 0, j)),
        name="adaln_mod",
    )(c_all, w_ada, b_ada.reshape(depth, 1, 6 * D_MODEL))


def _segment_ms(x, lo_mask):
    x2 = x * x
    s_lo = jnp.sum(jnp.where(lo_mask, x2, 0.0), axis=-1, keepdims=True)
    s_hi = jnp.sum(jnp.where(lo_mask, 0.0, x2), axis=-1, keepdims=True)
    return jnp.where(lo_mask, s_lo, s_hi) * (1.0 / HEAD_DIM)


def _inproj_kernel(*refs, ctx_tiles, split):
    if split:
        ctx_ref, x_ref = refs[0:2]
        refs = refs[2:]
        from_ctx = pl.program_id(1) < ctx_tiles
        load = lambda bb: jnp.where(from_ctx, ctx_ref[bb], x_ref[bb])
    else:
        x_ref = refs[0]
        refs = refs[1:]
        load = lambda bb: x_ref[bb]
    (mod_ref, nm_ref, w_ref, wt_ref, bg_ref, qn_ref, kn_ref, cos_ref, sa_ref, sb_ref,
     q_ref, k_ref, v_ref, mk_ref, mo_ref, pp_ref, mqt_ref, mvt_ref, g_ref) = refs
    nb, rows = q_ref.shape[0], q_ref.shape[1]

    hs = []
    for bb in range(nb):
        x = load(bb)
        ms = jnp.mean(x * x, axis=-1, keepdims=True)
        y = x * lax.rsqrt(ms + EPS) * nm_ref[...]
        hs.append((y * (1.0 + mod_ref[bb, 1]) + mod_ref[bb, 0]).astype(BF16))
    h = jnp.concatenate(hs, axis=0)
    acc = jnp.dot(h, w_ref[...], preferred_element_type=F32)
    acc_t = lax.dot_general(wt_ref[...], h, (((1,), (1,)), ((), ())), preferred_element_type=F32)

    lo_mask = lax.broadcasted_iota(jnp.int32, (rows, LANES), 1) < HEAD_DIM
    cos, sa, sb = cos_ref[...], sa_ref[...], sb_ref[...]

    def norm_rope(xc, gain):
        yc = xc * lax.rsqrt(_segment_ms(xc, lo_mask) + EPS) * gain
        return yc * cos + pltpu.roll(yc, LANES - 16, 1) * sa + pltpu.roll(yc, 16, 1) * sb

    for bb in range(nb):
        r0 = bb * rows
        blk = lambda off, width: acc[r0:r0 + rows, off:off + width]
        for c in range(ATT_WIDTH // LANES):
            qc = norm_rope(blk(OFF_Q + c * LANES, LANES), qn_ref[...])
            q_ref[bb, :, c * LANES:(c + 1) * LANES] = (qc * Q_SCALE).astype(BF16)
        k_ref[bb] = norm_rope(blk(OFF_K, LANES), kn_ref[...]).astype(BF16)
        v_ref[bb] = blk(OFF_V, KV_WIDTH).astype(BF16)
        mk_ref[bb] = blk(OFF_MK, ML_WIDTH).astype(BF16)
        mo_ref[bb] = blk(OFF_MO, ML_WIDTH).astype(BF16)
        pp_ref[bb] = blk(OFF_PP, POOL_WIDTH).astype(BF16)
        mqt_ref[bb] = acc_t[ROW_MQ:ROW_MQ + ML_WIDTH, r0:r0 + rows].astype(BF16)
        mvt_ref[bb] = acc_t[ROW_MV:ROW_MV + ML_WIDTH, r0:r0 + rows].astype(BF16)
        g_ref[bb] = acc_t[ROW_G:ROW_G + N_GATES, r0:r0 + rows] + bg_ref[...]


def _mod_spec(layer, nb, ctx_tiles, ctx_block, first=0):
    def idx(b, i):
        return (layer, jnp.where(i + first < ctx_tiles, ctx_block, b), 0, 0, 0)
    return pl.BlockSpec((None, nb, 6, 1, D_MODEL), idx)


def _token_specs(x_in, nb, ctx_tiles, first=0):
    if isinstance(x_in, tuple):
        ctx, x = x_in
        specs = [pl.BlockSpec((nb, ROW_TILE, D_MODEL), lambda b, i: (b, jnp.minimum(i, ctx_tiles - 1), 0)),
                 pl.BlockSpec((nb, ROW_TILE, D_MODEL), lambda b, i: (b, jnp.maximum(i - ctx_tiles, 0), 0))]
        return specs, [ctx, x], ctx.shape[0], ctx.shape[1] + x.shape[1]
    specs = [pl.BlockSpec((nb, ROW_TILE, D_MODEL), lambda b, i: (b, i + first, 0))]
    return specs, [x_in], x_in.shape[0], x_in.shape[1]


def _in_proj(x_in, mod5, layer, norm_mix, w_in_r, w_in_t, b_gates_b, qn, kn, cos_t, sa_t, sb_t, ctx_tiles):
    nb = STEP_ROWS
    x_specs, x_ops, B, TA = _token_specs(x_in, nb, ctx_tiles)
    nt = TA // ROW_TILE
    row = lambda w: pl.BlockSpec((nb, ROW_TILE, w), lambda b, i: (b, i, 0))
    col = lambda r: pl.BlockSpec((nb, r, ROW_TILE), lambda b, i: (b, 0, i))
    vec = lambda w: pl.BlockSpec((1, w), lambda b, i: (0, 0))
    full = lambda r, c: pl.BlockSpec((r, c), lambda b, i: (0, 0))
    tab = pl.BlockSpec((ROW_TILE, LANES), lambda b, i: (i, 0))
    row_widths = (ATT_WIDTH, KV_WIDTH, KV_WIDTH, ML_WIDTH, ML_WIDTH, POOL_WIDTH)
    col_heights = (ML_WIDTH, ML_WIDTH)
    out_shapes = [jax.ShapeDtypeStruct((B, TA, w), BF16) for w in row_widths]
    out_shapes += [jax.ShapeDtypeStruct((B, r, TA), BF16) for r in col_heights]
    out_shapes.append(jax.ShapeDtypeStruct((B, N_GATES, TA), F32))
    out_specs = [row(w) for w in row_widths] + [col(r) for r in col_heights] + [col(N_GATES)]
    kern = functools.partial(_inproj_kernel, ctx_tiles=ctx_tiles, split=len(x_ops) == 2)
    return pl.pallas_call(
        kern,
        out_shape=out_shapes,
        grid=(B // nb, nt),
        in_specs=x_specs + [_mod_spec(layer, nb, ctx_tiles, B // nb), vec(D_MODEL), full(D_MODEL, IN_COLS),
                            full(T_ROWS, D_MODEL), full(N_GATES, ROW_TILE), vec(LANES), vec(LANES), tab, tab, tab],
        out_specs=out_specs,
        compiler_params=pltpu.CompilerParams(vmem_limit_bytes=VMEM_LIMIT),
        name="in_proj",
    )(*x_ops, mod5, norm_mix, w_in_r, w_in_t, b_gates_b, qn, kn, cos_t, sa_t, sb_t)


def _attn_kernel(q_ref, k_ref, v_ref, o_ref, k0_ref, k1_ref, va_ref, *, ctx_len, first_tile):
    i = pl.program_id(1)
    total = k_ref.shape[0]

    @pl.when(i == 0)
    def _():
        k = k_ref[...]
        lo = lax.broadcasted_iota(jnp.int32, k.shape, 1) < HEAD_DIM
        zero = jnp.zeros_like(k)
        k0_ref[...] = jnp.where(lo, k, zero)
        k1_ref[...] = jnp.where(lo, zero, k)
        va_ref[:, 0:LANES] = v_ref[...]
        va_ref[:, LANES:2 * LANES] = jnp.ones((total, LANES), BF16)

    rows = q_ref.shape[0]
    lo_mask = lax.broadcasted_iota(jnp.int32, (rows, LANES), 1) < HEAD_DIM

    def block(nk):
        chunks = list(range(ATT_WIDTH // LANES))
        for g in range(0, len(chunks), ATT_GROUP):
            cs = chunks[g:g + ATT_GROUP]
            ss = [lax.dot_general(q_ref[:, c * LANES:(c + 1) * LANES], kr[0:nk, :], (((1,), (1,)), ((), ())),
                                  preferred_element_type=F32) for c in cs for kr in (k0_ref, k1_ref)]
            ms = [jnp.max(s, axis=-1, keepdims=True) for s in ss]
            ps = [jnp.exp2(s - m).astype(BF16) for s, m in zip(ss, ms)]
            acs = [jnp.dot(p, va_ref[0:nk, :], preferred_element_type=F32) for p in ps]
            outs = [a[:, 0:LANES] / a[:, LANES:2 * LANES] for a in acs]
            for n, c in enumerate(cs):
                o_ref[:, c * LANES:(c + 1) * LANES] = jnp.where(lo_mask, outs[2 * n], outs[2 * n + 1]).astype(BF16)

    if first_tile == 0:
        pl.when(i == 0)(lambda: block(ctx_len))
        pl.when(i > 0)(lambda: block(total))
    else:
        block(total)


def _attention(q, k, v, ctx_len, with_ctx):
    B, TA, _ = q.shape
    first = 0 if with_ctx else ctx_len // ROW_TILE
    nt = TA // ROW_TILE - first
    kern = functools.partial(_attn_kernel, ctx_len=ctx_len, first_tile=first)
    return pl.pallas_call(
        kern,
        out_shape=jax.ShapeDtypeStruct((B, nt * ROW_TILE, ATT_WIDTH), BF16),
        grid=(B, nt),
        in_specs=[pl.BlockSpec((None, ROW_TILE, ATT_WIDTH), lambda b, i: (b, i + first, 0)),
                  pl.BlockSpec((None, TA, KV_WIDTH), lambda b, i: (b, 0, 0)),
                  pl.BlockSpec((None, TA, KV_WIDTH), lambda b, i: (b, 0, 0))],
        out_specs=pl.BlockSpec((None, ROW_TILE, ATT_WIDTH), lambda b, i: (b, i, 0)),
        scratch_shapes=[pltpu.VMEM((TA, KV_WIDTH), BF16), pltpu.VMEM((TA, KV_WIDTH), BF16),
                        pltpu.VMEM((TA, 2 * LANES), BF16)],
        compiler_params=pltpu.CompilerParams(vmem_limit_bytes=VMEM_LIMIT),
        name="attention",
    )(q, k, v)


def _mlstm_kernel(qf_ref, kf_ref, vf_ref, gf_ref, qb_ref, kb_ref, vb_ref, gb_ref, tri_ref, hf_ref, hb_ref,
                  st_ref, m_ref, r_ref):
    i = pl.program_id(1)

    @pl.when(i == 0)
    def _():
        st_ref[...] = jnp.zeros_like(st_ref)
        m_ref[...] = jnp.zeros_like(m_ref)
        r_ref[...] = jnp.zeros_like(r_ref)

    L = CHUNK
    s_idx = lax.broadcasted_iota(jnp.int32, (L, L), 0)
    t_idx = lax.broadcasted_iota(jnp.int32, (L, L), 1)
    first_head_rows = lax.broadcasted_iota(jnp.int32, (LANES, L), 0) < HEAD_DIM
    ones_blk = jnp.where(lax.broadcasted_iota(jnp.int32, (HEAD_DIM, L), 0) == 0, 1.0, 0.0).astype(BF16)

    dirs = ((qf_ref, kf_ref, vf_ref, gf_ref, hf_ref), (qb_ref, kb_ref, vb_ref, gb_ref, hb_ref))
    chains = [divmod(bd, 2) for bd in range(2 * hf_ref.shape[0])]
    units = [(bd, hd) for bd in range(len(chains)) for hd in range(ML_HEADS)]
    refs = [tuple(ref.at[bb] for ref in dirs[d]) for bb, d in chains]

    qm, kp, va, st, s_raw, a1 = {}, {}, {}, {}, {}, {}
    for bd, hd in units:
        q_r, k_r, v_r = refs[bd][0:3]
        pair, second = divmod(hd, 2)
        kp[bd, hd] = k_r[:, pair * LANES:(pair + 1) * LANES]
        q_pair = q_r[pair * LANES:(pair + 1) * LANES, :]
        keep = jnp.logical_not(first_head_rows) if second else first_head_rows
        qm[bd, hd] = jnp.where(keep, q_pair, jnp.zeros_like(q_pair))
        va[bd, hd] = jnp.concatenate([v_r[hd * HEAD_DIM:(hd + 1) * HEAD_DIM, :], ones_blk], axis=0)
        st[bd, hd] = st_ref[bd * ML_HEADS + hd]
        s_raw[bd, hd] = jnp.dot(kp[bd, hd], qm[bd, hd], preferred_element_type=F32)
        a1[bd, hd] = jnp.dot(st[bd, hd].astype(BF16), qm[bd, hd], preferred_element_type=F32)

    g8 = [refs[bd][3][8 * d:8 * d + 8, :] for bd, (bb, d) in enumerate(chains)]
    lf8 = [jnp.minimum(g, 0.0) - jnp.log1p(jnp.exp(-jnp.abs(g))) for g in g8]
    parts = [jnp.concatenate([p.astype(F32) for p in _split3(lf)], axis=0) for lf in lf8]
    cum = [jnp.dot(p, tri_ref[d], preferred_element_type=F32) for p, (bb, d) in zip(parts, chains)]
    cum = [c[16:24] + c[8:16] + c[0:8] for c in cum]
    b8 = [c[:, 0:L] for c in cum]
    btot8 = [c[:, L:2 * L] for c in cum]
    li8 = [pltpu.roll(g, ML_HEADS, 0) for g in g8]
    for bd in range(len(chains)):
        r_ref[bd, 0:8, :] = li8[bd] - b8[bd]
    ct = [r_ref[bd].T for bd in range(len(chains))]
    m_prev8 = [m_ref[bd] for bd in range(len(chains))]
    gg8 = [bt - b + li for bt, b, li in zip(btot8, b8, li8)]
    m_new8 = [jnp.maximum(bt + mp, jnp.max(gg, axis=-1, keepdims=True)) for bt, mp, gg in zip(btot8, m_prev8, gg8)]
    a8 = [jnp.exp(bt + mp - mn) for bt, mp, mn in zip(btot8, m_prev8, m_new8)]
    wk8 = [jnp.exp(gg - mn) * (HEAD_DIM ** -0.5) for gg, mn in zip(gg8, m_new8)]
    inter8 = [b + mp for b, mp in zip(b8, m_prev8)]
    for bd in range(len(chains)):
        m_ref[bd] = m_new8[bd]

    dmat, m_t, s_t, a2, upd = {}, {}, {}, {}, {}
    for bd, hd in units:
        r = ML_HEADS + hd
        causal = (s_idx <= t_idx) if chains[bd][1] == 0 else (s_idx >= t_idx)
        dmat[bd, hd] = jnp.where(causal, ct[bd][:, r:r + 1] + b8[bd][r:r + 1], -jnp.inf)
        m_t[bd, hd] = jnp.maximum(inter8[bd][r:r + 1], jnp.max(dmat[bd, hd], axis=0, keepdims=True))
    for bd, hd in units:
        s_t[bd, hd] = (s_raw[bd, hd] * (HEAD_DIM ** -0.5) * jnp.exp(dmat[bd, hd] - m_t[bd, hd])).astype(BF16)
    for bd, hd in units:
        r = ML_HEADS + hd
        a2[bd, hd] = jnp.dot(va[bd, hd], s_t[bd, hd], preferred_element_type=F32)
        vw = (va[bd, hd].astype(F32) * wk8[bd][r:r + 1]).astype(BF16)
        upd[bd, hd] = jnp.dot(vw, kp[bd, hd], preferred_element_type=F32)
    hts = {}
    for bd, hd in units:
        r = ML_HEADS + hd
        inter = inter8[bd][r:r + 1]
        nd = jnp.exp(inter - m_t[bd, hd]) * a1[bd, hd] + a2[bd, hd]
        den = nd[HEAD_DIM:HEAD_DIM + 1, :]
        hts[bd, hd] = nd[0:HEAD_DIM, :] / jnp.maximum(jnp.abs(den), jnp.exp(-m_t[bd, hd]))
        st_ref[bd * ML_HEADS + hd] = a8[bd][r:r + 1] * st[bd, hd] + upd[bd, hd]
    for bd in range(len(chains)):
        refs[bd][4][...] = jnp.concatenate([hts[bd, hd] for hd in range(ML_HEADS)], axis=0).T.astype(BF16)


def _mlstm(mqt, mk, mvt, gates, tri, ctx_len):
    B, TA, _ = mk.shape
    nc = TA // CHUNK
    ncc = ctx_len // CHUNK

    def fwd(i):
        return i

    def bwd(i):
        return jnp.where(i < ncc, ncc - 1 - i, nc - 1 - (i - ncc))

    nb = MLSTM_ROWS if B % MLSTM_ROWS == 0 else 1
    tok = lambda f: pl.BlockSpec((nb, CHUNK, ML_WIDTH), lambda b, i: (b, f(i), 0))
    feat = lambda f, r: pl.BlockSpec((nb, r, CHUNK), lambda b, i: (b, 0, f(i)))
    n_units = 2 * nb * ML_HEADS
    return pl.pallas_call(
        _mlstm_kernel,
        out_shape=[jax.ShapeDtypeStruct((B, TA, ML_WIDTH), BF16)] * 2,
        grid=(B // nb, nc),
        in_specs=[feat(fwd, ML_WIDTH), tok(fwd), feat(fwd, ML_WIDTH), feat(fwd, N_GATES),
                  feat(bwd, ML_WIDTH), tok(bwd), feat(bwd, ML_WIDTH), feat(bwd, N_GATES),
                  pl.BlockSpec((2, CHUNK, 2 * CHUNK), lambda b, i: (0, 0, 0))],
        out_specs=[tok(fwd), tok(bwd)],
        scratch_shapes=[pltpu.VMEM((n_units, LANES, LANES), F32), pltpu.VMEM((2 * nb, 8, LANES), F32),
                        pltpu.VMEM((2 * nb, CHUNK, LANES), F32)],
        compiler_params=pltpu.CompilerParams(vmem_limit_bytes=VMEM_LIMIT),
        name="mlstm",
    )(mqt, mk, mvt, gates, mqt, mk, mvt, gates, tri)


def _pool_kernel(pp_ref, bd_ref, ps_ref, o_ref, p_ref, s2_ref, s4_ref, s8_ref, *, ctx_len):
    total = pp_ref.shape[0]
    seqs = ((0, ctx_len, POOL_PAD), (ctx_len, total - ctx_len, 2 * POOL_PAD + ctx_len))
    rows = p_ref.shape[0]
    for ref in (p_ref, s2_ref, s4_ref, s8_ref):
        ref[...] = jnp.zeros_like(ref)
    for src, n, dst in seqs:
        p_ref[dst:dst + n, :] = pp_ref[src:src + n, :].astype(F32)
    lo, m = 8, rows - 16
    s2_ref[lo:lo + m, :] = p_ref[lo - 1:lo - 1 + m, :] + p_ref[lo:lo + m, :]
    s4_ref[lo:lo + m, :] = s2_ref[lo - 1:lo - 1 + m, :] + s2_ref[lo + 1:lo + 1 + m, :]
    s8_ref[lo:lo + m, :] = s4_ref[lo - 2:lo - 2 + m, :] + s4_ref[lo + 2:lo + 2 + m, :]
    for src, n, dst in seqs:
        lane = lax.broadcasted_iota(jnp.int32, (n, POOL_WIDTH), 1)
        t = lax.broadcasted_iota(jnp.int32, (n, POOL_WIDTH), 0)
        g0, g1, g2 = lane < POOL_GROUP_DIM, lane < 2 * POOL_GROUP_DIM, lane < 3 * POOL_GROUP_DIM
        half = jnp.where(g0, 1, jnp.where(g1, 2, jnp.where(g2, 4, 8)))
        cnt = (jnp.minimum(t + half, n) - jnp.maximum(t - half, 0)).astype(F32)
        s16 = s8_ref[dst - 4:dst - 4 + n, :] + s8_ref[dst + 4:dst + 4 + n, :]
        wsum = jnp.where(g0, s2_ref[dst:dst + n, :],
                         jnp.where(g1, s4_ref[dst:dst + n, :], jnp.where(g2, s8_ref[dst:dst + n, :], s16)))
        p = wsum / cnt - p_ref[dst:dst + n, :]
        y = jnp.dot(p.astype(BF16), bd_ref[...], preferred_element_type=F32) * ps_ref[...]
        o_ref[src:src + n, :] = y.astype(BF16)


def _pool(pp, pool_bd, pool_scale, ctx_len):
    B, TA, _ = pp.shape
    rows = TA + 3 * POOL_PAD
    kern = functools.partial(_pool_kernel, ctx_len=ctx_len)
    full = pl.BlockSpec((None, TA, POOL_WIDTH), lambda b: (b, 0, 0))
    return pl.pallas_call(
        kern,
        out_shape=jax.ShapeDtypeStruct((B, TA, POOL_WIDTH), BF16),
        grid=(B,),
        in_specs=[full, pl.BlockSpec((POOL_WIDTH, POOL_WIDTH), lambda b: (0, 0)),
                  pl.BlockSpec((1, POOL_WIDTH), lambda b: (0, 0))],
        out_specs=full,
        scratch_shapes=[pltpu.VMEM((rows, POOL_WIDTH), F32)] * 4,
        compiler_params=pltpu.CompilerParams(vmem_limit_bytes=VMEM_LIMIT),
        name="pool_mix",
    )(pp, pool_bd, pool_scale)


def _mixffn_kernel(*refs, ctx_tiles, split, final):
    if split:
        ctx_ref, x_ref = refs[0:2]
        refs = refs[2:]
        from_ctx = pl.program_id(1) < ctx_tiles
        load = lambda bb: jnp.where(from_ctx, ctx_ref[bb], x_ref[bb])
    else:
        x_ref = refs[0]
        refs = refs[1:]
        load = lambda bb: x_ref[bb]
    (att_ref, hf_ref, hb_ref, mo_ref, pool_ref, mod_ref, mln_ref, wo_ref, nf_ref,
     wg_ref, wu_ref, wd_ref, fn_ref, o_ref) = refs
    nb, rows = o_ref.shape[0], o_ref.shape[1]
    lo_mask = lax.broadcasted_iota(jnp.int32, (rows, LANES), 1) < HEAD_DIM

    cats = []
    for bb in range(nb):
        ml = []
        for c in range(ML_WIDTH // LANES):
            sl = slice(c * LANES, (c + 1) * LANES)
            hs = hf_ref[bb, :, sl].astype(F32) + hb_ref[bb, :, sl].astype(F32)
            hn = hs * lax.rsqrt(_segment_ms(hs, lo_mask) + EPS) * mln_ref[:, sl]
            ml.append((hn * _sigmoid(mo_ref[bb, :, sl].astype(F32))).astype(BF16))
        cats.append(jnp.concatenate([att_ref[bb]] + ml + [pool_ref[bb]], axis=-1))
    mixes = [jnp.dot(cat, wo_ref[...], preferred_element_type=F32) for cat in cats]

    x1s, h2s = [], []
    for bb in range(nb):
        x1 = load(bb) + mod_ref[bb, 2] * mixes[bb]
        ms = jnp.mean(x1 * x1, axis=-1, keepdims=True)
        h2s.append((x1 * lax.rsqrt(ms + EPS) * nf_ref[...] * (1.0 + mod_ref[bb, 4]) + mod_ref[bb, 3]).astype(BF16))
        x1s.append(x1)
    downs = []
    for h2 in h2s:
        down = None
        for lo, hi in FF_SPLITS:
            gate = jnp.dot(h2, wg_ref[:, lo:hi], preferred_element_type=F32)
            up = jnp.dot(h2, wu_ref[:, lo:hi], preferred_element_type=F32)
            act = (gate * _sigmoid(gate) * up).astype(BF16)
            part = jnp.dot(act, wd_ref[lo:hi, :], preferred_element_type=F32)
            down = part if down is None else down + part
        downs.append(down)
    for bb in range(nb):
        out = x1s[bb] + mod_ref[bb, 5] * downs[bb]
        if final:
            ms = jnp.mean(out * out, axis=-1, keepdims=True)
            out = out * lax.rsqrt(ms + EPS) * fn_ref[...]
        o_ref[bb] = out


def _mix_ffn(x_in, att, hf, hb, mo, pool, mod5, layer, ml_norm, w_out_r, norm_ffn, wg, wu, wd, final_norm,
             ctx_tiles, final):
    nb = STEP_ROWS
    first = ctx_tiles if final else 0
    x_specs, x_ops, B, TA = _token_specs(x_in, nb, ctx_tiles, first)
    nt = TA // ROW_TILE - first
    row = lambda w: pl.BlockSpec((nb, ROW_TILE, w), lambda b, i: (b, i + first, 0))
    vec = lambda w: pl.BlockSpec((1, w), lambda b, i: (0, 0))
    wspec = lambda r, c: pl.BlockSpec((r, c), lambda b, i: (0, 0), pipeline_mode=pl.Buffered(1))

    kern = functools.partial(_mixffn_kernel, ctx_tiles=ctx_tiles, split=len(x_ops) == 2, final=final)
    return pl.pallas_call(
        kern,
        out_shape=jax.ShapeDtypeStruct((B, nt * ROW_TILE, D_MODEL), F32),
        grid=(B // nb, nt),
        in_specs=x_specs + [pl.BlockSpec((nb, ROW_TILE, ATT_WIDTH), lambda b, i: (b, i, 0)),
                            row(ML_WIDTH), row(ML_WIDTH), row(ML_WIDTH), row(POOL_WIDTH),
                            _mod_spec(layer, nb, ctx_tiles, B // nb, first), vec(ML_WIDTH),
                            wspec(D_MODEL, D_MODEL), vec(D_MODEL), wspec(D_MODEL, D_FF), wspec(D_MODEL, D_FF),
                            wspec(D_FF, D_MODEL), vec(D_MODEL)],
        out_specs=pl.BlockSpec((nb, ROW_TILE, D_MODEL), lambda b, i: (b, i, 0)),
        compiler_params=pltpu.CompilerParams(vmem_limit_bytes=VMEM_LIMIT),
        name="mix_ffn",
    )(*x_ops, att, hf, hb, mo, pool, mod5, ml_norm, w_out_r, norm_ffn, wg, wu, wd, final_norm)


def _rope_tables(t_latent, ctx_len):
    half = HEAD_DIM // 2
    inv_freq = 1.0 / (ROPE_THETA ** (jnp.arange(0, half, 2, dtype=F32) / half))
    rows = t_latent // GRID_W
    row = jnp.repeat(jnp.arange(rows, dtype=F32), GRID_W)
    col = jnp.tile(jnp.arange(GRID_W, dtype=F32), rows)
    a_row = row[:, None] * inv_freq
    a_col = col[:, None] * inv_freq
    ang = jnp.concatenate([a_row, a_row, a_col, a_col], -1)
    cos, sin = jnp.cos(ang), jnp.sin(ang)
    first_half = (jnp.arange(HEAD_DIM) % 32) < 16
    sa = jnp.where(first_half, -sin, 0.0)
    sb = jnp.where(first_half, 0.0, sin)
    pad = lambda a, fill: jnp.tile(jnp.concatenate([jnp.full((ctx_len, HEAD_DIM), fill, F32), a], 0), (1, 2))
    return pad(cos, 1.0), pad(sa, 0.0), pad(sb, 0.0)


def _rearrange_w_in(w_in):
    aq, ak, av, mq, mk, mv, mo, mg, pp = jnp.split(w_in, (512, 640, 768, 1024, 1280, 1536, 1792, 1808), axis=-1)
    w_rows = jnp.concatenate([aq[:, HEAD_PERM], ak, av, mk, mo, pp], axis=-1).astype(BF16)
    w_feat = jnp.concatenate([mq, mv, mg], axis=-1).T.astype(BF16)
    return w_rows, w_feat


def _cumsum_matrices():
    s = np.arange(CHUNK)[:, None]
    t = np.arange(CHUNK)[None, :]
    ones = np.ones((CHUNK, CHUNK), np.float32)
    prefix = np.concatenate([(s <= t).astype(np.float32), ones], axis=1)
    suffix = np.concatenate([(s >= t).astype(np.float32), ones], axis=1)
    return jnp.asarray(np.stack([prefix, suffix]))


def kernel(x, c, ctx, c_ctx, w_ada, b_ada, norm_mix, w_in, b_gates, q_norm, k_norm, ml_norm, pool_w, pool_scale,
           w_out, norm_ffn, w_ffn_gate, w_ffn_up, w_ffn_down, final_norm):
    B, T, _ = x.shape
    ctx_len = ctx.shape[1]
    depth = w_ada.shape[0]
    assert ctx_len % ROW_TILE == 0 and T % ROW_TILE == 0 and B % STEP_ROWS == 0 and B + STEP_ROWS <= MOD_ROWS
    ctx_tiles = ctx_len // ROW_TILE

    c_rep = jnp.broadcast_to(c_ctx[None, :], (STEP_ROWS, D_MODEL))
    c_all = jnp.concatenate([c, c_rep, jnp.zeros((MOD_ROWS - B - STEP_ROWS, D_MODEL), F32)], 0)
    mod = _modulation(c_all, w_ada, b_ada)
    mod5 = mod.reshape(depth, MOD_ROWS, 6, 1, D_MODEL)
    cos_t, sa_t, sb_t = _rope_tables(T, ctx_len)
    tri = _cumsum_matrices()
    vec = lambda a: a.reshape(1, -1)

    x_all = (ctx, x)
    for l in range(depth):
        final = l == depth - 1
        w_in_r, w_in_t = _rearrange_w_in(w_in[l])
        b_gates_b = jnp.broadcast_to(b_gates[l][:, None], (N_GATES, ROW_TILE))
        qn = jnp.tile(q_norm[l], 2).reshape(1, LANES)
        kn = jnp.tile(k_norm[l], 2).reshape(1, LANES)
        pool_bd = jax.scipy.linalg.block_diag(*[pool_w[l, g] for g in range(len(POOL_WINDOWS))]).astype(BF16)
        w_out_r = jnp.concatenate([w_out[l, :ATT_WIDTH][HEAD_PERM], w_out[l, ATT_WIDTH:]], 0).astype(BF16)

        q, k, v, mk, mo, pp, mqt, mvt, gates = _in_proj(
            x_all, mod5, l, vec(norm_mix[l]), w_in_r, w_in_t, b_gates_b, qn, kn, cos_t, sa_t, sb_t, ctx_tiles)
        att = _attention(q, k, v, ctx_len, with_ctx=not final)
        hf, hb = _mlstm(mqt, mk, mvt, gates, tri, ctx_len)
        pool = _pool(pp, pool_bd, vec(pool_scale[l]), ctx_len)
        x_all = _mix_ffn(x_all, att, hf, hb, mo, pool, mod5, l, vec(ml_norm[l]), w_out_r, vec(norm_ffn[l]),
                         w_ffn_gate[l].astype(BF16), w_ffn_up[l].astype(BF16), w_ffn_down[l].astype(BF16),
                         vec(final_norm), ctx_tiles, final)
    return x_all
```

```python
import functools

import jax
import jax.numpy as jnp
import numpy as np
from jax import lax
from jax.experimental import pallas as pl
from jax.experimental.pallas import tpu as pltpu

F32 = jnp.float32
BF16 = jnp.bfloat16

D_MODEL = 1024
HEAD_DIM = 64
GRID_W = 64
ATT_WIDTH = 512
KV_WIDTH = 128
ML_WIDTH = 256
POOL_WIDTH = 256
ATT_HEADS = 8
ML_HEADS = 4
N_GATES = 16
POOL_WINDOWS = (2, 4, 8, 16)
POOL_GROUP_DIM = 64
D_FF = 2816
CHUNK = 128
ROPE_THETA = 10000.0
EPS = 1e-6

LANES = 128
ROW_TILE = 256
MOD_ROWS = 24
POOL_PAD = 16
VMEM_LIMIT = 52 * 1024 * 1024

OFF_Q, OFF_K, OFF_V, OFF_MK, OFF_MO, OFF_PP = 0, 512, 640, 768, 1024, 1280
IN_COLS = OFF_PP + POOL_WIDTH
ROW_MQ, ROW_MV, ROW_G = 0, 256, 512
T_ROWS = ROW_G + N_GATES
Q_SCALE = HEAD_DIM ** -0.5 * float(np.log2(np.e))
VA_ROWS = 80
ATT_GROUP = 2
MLSTM_ROWS = 8
STEP_ROWS = 2
INPROJ_ROWS = 4
FF_SPLITS = ((0, 1536), (1536, D_FF))

HEAD_PERM = np.concatenate(
    [np.concatenate([np.arange(c * 64, (c + 1) * 64), np.arange((4 + c) * 64, (5 + c) * 64)]) for c in range(4)])


def _sigmoid(x):
    return 1.0 / (1.0 + jnp.exp(-x))


def _split3(a):
    hi = a.astype(BF16)
    r1 = a - hi.astype(F32)
    mid = r1.astype(BF16)
    lo = (r1 - mid.astype(F32)).astype(BF16)
    return hi, mid, lo


def _mod_kernel(c_ref, w_ref, b_ref, o_ref):
    c = c_ref[...]
    s = c * _sigmoid(c)
    s_hi, s_mid, _ = _split3(s)
    w = w_ref[...]
    w_hi = w.astype(BF16)
    w_mid = (w - w_hi.astype(F32)).astype(BF16)
    dot = functools.partial(jnp.dot, preferred_element_type=F32)
    acc = dot(s_mid, w_hi) + dot(s_hi, w_mid)
    acc = acc + dot(s_hi, w_hi)
    o_ref[...] = acc + b_ref[...]


def _modulation(c_all, w_ada, b_ada):
    depth = w_ada.shape[0]
    tn = 1024
    return pl.pallas_call(
        _mod_kernel,
        out_shape=jax.ShapeDtypeStruct((depth, MOD_ROWS, 6 * D_MODEL), F32),
        grid=(depth, 6 * D_MODEL // tn),
        in_specs=[
            pl.BlockSpec((MOD_ROWS, D_MODEL), lambda l, j: (0, 0)),
            pl.BlockSpec((None, D_MODEL, tn), lambda l, j: (l, 0, j)),
            pl.BlockSpec((None, 1, tn), lambda l, j: (l, 0, j)),
        ],
        out_specs=pl.BlockSpec((None, MOD_ROWS, tn), lambda l, j: (l, 0, j)),
        name="adaln_mod",
    )(c_all, w_ada, b_ada.reshape(depth, 1, 6 * D_MODEL))


def _segment_ms(x, lo_mask):
    x2 = x * x
    s_lo = jnp.sum(jnp.where(lo_mask, x2, 0.0), axis=-1, keepdims=True)
    s_hi = jnp.sum(jnp.where(lo_mask, 0.0, x2), axis=-1, keepdims=True)
    return jnp.where(lo_mask, s_lo, s_hi) * (1.0 / HEAD_DIM)


def _inproj_kernel(*refs, ctx_tiles, split):
    if split:
        ctx_ref, x_ref = refs[0:2]
        refs = refs[2:]
        from_ctx = pl.program_id(1) < ctx_tiles
        load = lambda bb: jnp.where(from_ctx, ctx_ref[bb], x_ref[bb])
    else:
        x_ref = refs[0]
        refs = refs[1:]
        load = lambda bb: x_ref[bb]
    (mod_ref, nm_ref, w_ref, wt_ref, bg_ref, qn_ref, kn_ref, cos_ref, sa_ref, sb_ref,
     q_ref, k_ref, v_ref, mk_ref, mo_ref, pp_ref, mqt_ref, mvt_ref, g_ref) = refs
    nb, rows = q_ref.shape[0], q_ref.shape[1]

    hs = []
    for bb in range(nb):
        x = load(bb)
        ms = jnp.mean(x * x, axis=-1, keepdims=True)
        y = x * lax.rsqrt(ms + EPS) * nm_ref[...]
        hs.append((y * (1.0 + mod_ref[bb, 1]) + mod_ref[bb, 0]).astype(BF16))
    accs = [jnp.dot(h, w_ref[...], preferred_element_type=F32) for h in hs]
    accs_t = [lax.dot_general(wt_ref[...], h, (((1,), (1,)), ((), ())), preferred_element_type=F32)
              for h in hs]

    lo_mask = lax.broadcasted_iota(jnp.int32, (rows, LANES), 1) < HEAD_DIM
    cos, sa, sb = cos_ref[...], sa_ref[...], sb_ref[...]

    def norm_rope(xc, gain):
        yc = xc * lax.rsqrt(_segment_ms(xc, lo_mask) + EPS) * gain
        return yc * cos + pltpu.roll(yc, LANES - 16, 1) * sa + pltpu.roll(yc, 16, 1) * sb

    for bb in range(nb):
        acc, acc_t = accs[bb], accs_t[bb]
        blk = lambda off, width: acc[:, off:off + width]
        for c in range(ATT_WIDTH // LANES):
            qc = norm_rope(blk(OFF_Q + c * LANES, LANES), qn_ref[...])
            q_ref[bb, :, c * LANES:(c + 1) * LANES] = (qc * Q_SCALE).astype(BF16)
        k_ref[bb] = norm_rope(blk(OFF_K, LANES), kn_ref[...]).astype(BF16)
        v_ref[bb] = blk(OFF_V, KV_WIDTH).astype(BF16)
        mk_ref[bb] = blk(OFF_MK, ML_WIDTH).astype(BF16)
        mo_ref[bb] = blk(OFF_MO, ML_WIDTH).astype(BF16)
        pp_ref[bb] = blk(OFF_PP, POOL_WIDTH).astype(BF16)
        mqt_ref[bb] = acc_t[ROW_MQ:ROW_MQ + ML_WIDTH, :].astype(BF16)
        mvt_ref[bb] = acc_t[ROW_MV:ROW_MV + ML_WIDTH, :].astype(BF16)
        g_ref[bb] = acc_t[ROW_G:ROW_G + N_GATES, :] + bg_ref[...]


def _mod_spec(layer, nb, ctx_tiles, ctx_block, first=0):
    def idx(b, i):
        return (layer, jnp.where(i + first < ctx_tiles, ctx_block, b), 0, 0, 0)
    return pl.BlockSpec((None, nb, 6, 1, D_MODEL), idx)


def _token_specs(x_in, nb, ctx_tiles, first=0):
    if isinstance(x_in, tuple):
        ctx, x = x_in
        specs = [pl.BlockSpec((nb, ROW_TILE, D_MODEL), lambda b, i: (b, jnp.minimum(i, ctx_tiles - 1), 0)),
                 pl.BlockSpec((nb, ROW_TILE, D_MODEL), lambda b, i: (b, jnp.maximum(i - ctx_tiles, 0), 0))]
        return specs, [ctx, x], ctx.shape[0], ctx.shape[1] + x.shape[1]
    specs = [pl.BlockSpec((nb, ROW_TILE, D_MODEL), lambda b, i: (b, i + first, 0))]
    return specs, [x_in], x_in.shape[0], x_in.shape[1]


def _in_proj(x_in, mod5, layer, norm_mix, w_in_r, w_in_t, b_gates_b, qn, kn, cos_t, sa_t, sb_t, ctx_tiles):
    nb = INPROJ_ROWS
    x_specs, x_ops, B, TA = _token_specs(x_in, nb, ctx_tiles)
    nt = TA // ROW_TILE
    row = lambda w: pl.BlockSpec((nb, ROW_TILE, w), lambda b, i: (b, i, 0))
    col = lambda r: pl.BlockSpec((nb, r, ROW_TILE), lambda b, i: (b, 0, i))
    vec = lambda w: pl.BlockSpec((1, w), lambda b, i: (0, 0))
    full = lambda r, c: pl.BlockSpec((r, c), lambda b, i: (0, 0))
    tab = pl.BlockSpec((ROW_TILE, LANES), lambda b, i: (i, 0))
    row_widths = (ATT_WIDTH, KV_WIDTH, KV_WIDTH, ML_WIDTH, ML_WIDTH, POOL_WIDTH)
    col_heights = (ML_WIDTH, ML_WIDTH)
    out_shapes = [jax.ShapeDtypeStruct((B, TA, w), BF16) for w in row_widths]
    out_shapes += [jax.ShapeDtypeStruct((B, r, TA), BF16) for r in col_heights]
    out_shapes.append(jax.ShapeDtypeStruct((B, N_GATES, TA), F32))
    out_specs = [row(w) for w in row_widths] + [col(r) for r in col_heights] + [col(N_GATES)]
    kern = functools.partial(_inproj_kernel, ctx_tiles=ctx_tiles, split=len(x_ops) == 2)
    return pl.pallas_call(
        kern,
        out_shape=out_shapes,
        grid=(B // nb, nt),
        in_specs=x_specs + [_mod_spec(layer, nb, ctx_tiles, B // nb), vec(D_MODEL), full(D_MODEL, IN_COLS),
                            full(T_ROWS, D_MODEL), full(N_GATES, ROW_TILE), vec(LANES), vec(LANES), tab, tab, tab],
        out_specs=out_specs,
        compiler_params=pltpu.CompilerParams(vmem_limit_bytes=VMEM_LIMIT),
        name="in_proj",
    )(*x_ops, mod5, norm_mix, w_in_r, w_in_t, b_gates_b, qn, kn, cos_t, sa_t, sb_t)


def _attn_kernel(q_ref, k_ref, v_ref, o_ref, k0_ref, k1_ref, va_ref, *, ctx_len, first_tile):
    i = pl.program_id(1)
    total = k_ref.shape[0]

    @pl.when(i == 0)
    def _():
        k = k_ref[...]
        lo = lax.broadcasted_iota(jnp.int32, k.shape, 1) < HEAD_DIM
        zero = jnp.zeros_like(k)
        k0_ref[...] = jnp.where(lo, k, zero)
        k1_ref[...] = jnp.where(lo, zero, k)
        va_ref[:, 0:LANES] = v_ref[...]
        va_ref[:, LANES:2 * LANES] = jnp.ones((total, LANES), BF16)

    rows = q_ref.shape[0]
    lo_mask = lax.broadcasted_iota(jnp.int32, (rows, LANES), 1) < HEAD_DIM

    def block(nk):
        chunks = list(range(ATT_WIDTH // LANES))
        for g in range(0, len(chunks), ATT_GROUP):
            cs = chunks[g:g + ATT_GROUP]
            ss = [lax.dot_general(q_ref[:, c * LANES:(c + 1) * LANES], kr[0:nk, :], (((1,), (1,)), ((), ())),
                                  preferred_element_type=F32) for c in cs for kr in (k0_ref, k1_ref)]
            ms = [jnp.max(s, axis=-1, keepdims=True) for s in ss]
            ps = [jnp.exp2(s - m).astype(BF16) for s, m in zip(ss, ms)]
            acs = [jnp.dot(p, va_ref[0:nk, :], preferred_element_type=F32) for p in ps]
            outs = [a[:, 0:LANES] / a[:, LANES:2 * LANES] for a in acs]
            for n, c in enumerate(cs):
                o_ref[:, c * LANES:(c + 1) * LANES] = jnp.where(lo_mask, outs[2 * n], outs[2 * n + 1]).astype(BF16)

    if first_tile == 0:
        pl.when(i == 0)(lambda: block(ctx_len))
        pl.when(i > 0)(lambda: block(total))
    else:
        block(total)


def _attention(q, k, v, ctx_len, with_ctx):
    B, TA, _ = q.shape
    first = 0 if with_ctx else ctx_len // ROW_TILE
    nt = TA // ROW_TILE - first
    kern = functools.partial(_attn_kernel, ctx_len=ctx_len, first_tile=first)
    return pl.pallas_call(
        kern,
        out_shape=jax.ShapeDtypeStruct((B, nt * ROW_TILE, ATT_WIDTH), BF16),
        grid=(B, nt),
        in_specs=[pl.BlockSpec((None, ROW_TILE, ATT_WIDTH), lambda b, i: (b, i + first, 0)),
                  pl.BlockSpec((None, TA, KV_WIDTH), lambda b, i: (b, 0, 0)),
                  pl.BlockSpec((None, TA, KV_WIDTH), lambda b, i: (b, 0, 0))],
        out_specs=pl.BlockSpec((None, ROW_TILE, ATT_WIDTH), lambda b, i: (b, i, 0)),
        scratch_shapes=[pltpu.VMEM((TA, KV_WIDTH), BF16), pltpu.VMEM((TA, KV_WIDTH), BF16),
                        pltpu.VMEM((TA, 2 * LANES), BF16)],
        compiler_params=pltpu.CompilerParams(vmem_limit_bytes=VMEM_LIMIT),
        name="attention",
    )(q, k, v)


def _mlstm_kernel(qf_ref, kf_ref, vf_ref, gf_ref, qb_ref, kb_ref, vb_ref, gb_ref, tri_ref, hf_ref, hb_ref,
                  st_ref, m_ref, r_ref):
    i = pl.program_id(1)

    @pl.when(i == 0)
    def _():
        st_ref[...] = jnp.zeros_like(st_ref)
        m_ref[...] = jnp.zeros_like(m_ref)
        r_ref[...] = jnp.zeros_like(r_ref)

    L = CHUNK
    s_idx = lax.broadcasted_iota(jnp.int32, (L, L), 0)
    t_idx = lax.broadcasted_iota(jnp.int32, (L, L), 1)
    first_head_rows = lax.broadcasted_iota(jnp.int32, (LANES, L), 0) < HEAD_DIM
    ones_blk = jnp.where(lax.broadcasted_iota(jnp.int32, (VA_ROWS - HEAD_DIM, L), 0) == 0, 1.0, 0.0).astype(BF16)

    dirs = ((qf_ref, kf_ref, vf_ref, gf_ref, hf_ref), (qb_ref, kb_ref, vb_ref, gb_ref, hb_ref))
    chains = [divmod(bd, 2) for bd in range(2 * hf_ref.shape[0])]
    units = [(bd, hd) for bd in range(len(chains)) for hd in range(ML_HEADS)]
    refs = [tuple(ref.at[bb] for ref in dirs[d]) for bb, d in chains]

    qm, kp, va, st, s_raw, a1 = {}, {}, {}, {}, {}, {}
    for bd, hd in units:
        q_r, k_r, v_r = refs[bd][0:3]
        pair, second = divmod(hd, 2)
        kp[bd, hd] = k_r[:, pair * LANES:(pair + 1) * LANES]
        q_pair = q_r[pair * LANES:(pair + 1) * LANES, :]
        keep = jnp.logical_not(first_head_rows) if second else first_head_rows
        qm[bd, hd] = jnp.where(keep, q_pair, jnp.zeros_like(q_pair))
        va[bd, hd] = jnp.concatenate([v_r[hd * HEAD_DIM:(hd + 1) * HEAD_DIM, :], ones_blk], axis=0)
        st[bd, hd] = st_ref[bd * ML_HEADS + hd]
        s_raw[bd, hd] = jnp.dot(kp[bd, hd], qm[bd, hd], preferred_element_type=F32)
        a1[bd, hd] = jnp.dot(st[bd, hd].astype(BF16), qm[bd, hd], preferred_element_type=F32)

    g8 = [refs[bd][3][8 * d:8 * d + 8, :] for bd, (bb, d) in enumerate(chains)]
    lf8 = [jnp.minimum(g, 0.0) - jnp.log1p(jnp.exp(-jnp.abs(g))) for g in g8]
    parts = [jnp.concatenate([p.astype(F32) for p in _split3(lf)], axis=0) for lf in lf8]
    cum = [jnp.dot(p, tri_ref[d], preferred_element_type=F32) for p, (bb, d) in zip(parts, chains)]
    cum = [c[16:24] + c[8:16] + c[0:8] for c in cum]
    b8 = [c[:, 0:L] for c in cum]
    btot8 = [c[:, L:2 * L] for c in cum]
    li8 = [pltpu.roll(g, ML_HEADS, 0) for g in g8]
    for bd in range(len(chains)):
        r_ref[bd, 0:8, :] = li8[bd] - b8[bd]
    ct = [r_ref[bd].T for bd in range(len(chains))]
    m_prev8 = [m_ref[bd] for bd in range(len(chains))]
    gg8 = [bt - b + li for bt, b, li in zip(btot8, b8, li8)]
    m_new8 = [jnp.maximum(bt + mp, jnp.max(gg, axis=-1, keepdims=True)) for bt, mp, gg in zip(btot8, m_prev8, gg8)]
    a8 = [jnp.exp(bt + mp - mn) for bt, mp, mn in zip(btot8, m_prev8, m_new8)]
    wk8 = [jnp.exp(gg - mn) for gg, mn in zip(gg8, m_new8)]
    inter8 = [b + mp for b, mp in zip(b8, m_prev8)]
    for bd in range(len(chains)):
        m_ref[bd] = m_new8[bd]

    dmat, m_t, s_t, a2, upd = {}, {}, {}, {}, {}
    for bd, hd in units:
        r = ML_HEADS + hd
        causal = (s_idx <= t_idx) if chains[bd][1] == 0 else (s_idx >= t_idx)
        dmat[bd, hd] = jnp.where(causal, ct[bd][:, r:r + 1] + b8[bd][r:r + 1], -jnp.inf)
        m_t[bd, hd] = jnp.maximum(inter8[bd][r:r + 1], jnp.max(dmat[bd, hd], axis=0, keepdims=True))
    for bd, hd in units:
        s_t[bd, hd] = (s_raw[bd, hd] * jnp.exp(dmat[bd, hd] - m_t[bd, hd])).astype(BF16)
    for bd, hd in units:
        r = ML_HEADS + hd
        a2[bd, hd] = jnp.dot(va[bd, hd], s_t[bd, hd], preferred_element_type=F32)
        vw = (va[bd, hd].astype(F32) * wk8[bd][r:r + 1]).astype(BF16)
        upd[bd, hd] = jnp.dot(vw, kp[bd, hd], preferred_element_type=F32)
    hts = {}
    for bd, hd in units:
        r = ML_HEADS + hd
        inter = inter8[bd][r:r + 1]
        nd = jnp.exp(inter - m_t[bd, hd]) * a1[bd, hd] + a2[bd, hd]
        den = nd[HEAD_DIM:HEAD_DIM + 1, :]
        hts[bd, hd] = nd[0:HEAD_DIM, :] / jnp.maximum(jnp.abs(den), jnp.exp(-m_t[bd, hd]))
        st_ref[bd * ML_HEADS + hd] = a8[bd][r:r + 1] * st[bd, hd] + upd[bd, hd]
    for bd in range(len(chains)):
        refs[bd][4][...] = jnp.concatenate([hts[bd, hd] for hd in range(ML_HEADS)], axis=0).T.astype(BF16)


def _mlstm(mqt, mk, mvt, gates, tri, ctx_len):
    B, TA, _ = mk.shape
    nc = TA // CHUNK
    ncc = ctx_len // CHUNK

    def fwd(i):
        return i

    def bwd(i):
        return jnp.where(i < ncc, ncc - 1 - i, nc - 1 - (i - ncc))

    nb = MLSTM_ROWS if B % MLSTM_ROWS == 0 else 1
    tok = lambda f: pl.BlockSpec((nb, CHUNK, ML_WIDTH), lambda b, i: (b, f(i), 0))
    feat = lambda f, r: pl.BlockSpec((nb, r, CHUNK), lambda b, i: (b, 0, f(i)))
    n_units = 2 * nb * ML_HEADS
    return pl.pallas_call(
        _mlstm_kernel,
        out_shape=[jax.ShapeDtypeStruct((B, TA, ML_WIDTH), BF16)] * 2,
        grid=(B // nb, nc),
        in_specs=[feat(fwd, ML_WIDTH), tok(fwd), feat(fwd, ML_WIDTH), feat(fwd, N_GATES),
                  feat(bwd, ML_WIDTH), tok(bwd), feat(bwd, ML_WIDTH), feat(bwd, N_GATES),
                  pl.BlockSpec((2, CHUNK, 2 * CHUNK), lambda b, i: (0, 0, 0))],
        out_specs=[tok(fwd), tok(bwd)],
        scratch_shapes=[pltpu.VMEM((n_units, VA_ROWS, LANES), F32), pltpu.VMEM((2 * nb, 8, LANES), F32),
                        pltpu.VMEM((2 * nb, CHUNK, LANES), F32)],
        compiler_params=pltpu.CompilerParams(vmem_limit_bytes=VMEM_LIMIT),
        name="mlstm",
    )(mqt, mk, mvt, gates, mqt, mk, mvt, gates, tri)


def _pool_kernel(pp_ref, bd_ref, ps_ref, o_ref, p_ref, s2_ref, s4_ref, s8_ref, *, ctx_len):
    total = pp_ref.shape[0]
    seqs = ((0, ctx_len, POOL_PAD), (ctx_len, total - ctx_len, 2 * POOL_PAD + ctx_len))
    rows = p_ref.shape[0]
    for ref in (p_ref, s2_ref, s4_ref, s8_ref):
        ref[...] = jnp.zeros_like(ref)
    for src, n, dst in seqs:
        p_ref[dst:dst + n, :] = pp_ref[src:src + n, :].astype(F32)
    lo, m = 8, rows - 16
    s2_ref[lo:lo + m, :] = p_ref[lo - 1:lo - 1 + m, :] + p_ref[lo:lo + m, :]
    s4_ref[lo:lo + m, :] = s2_ref[lo - 1:lo - 1 + m, :] + s2_ref[lo + 1:lo + 1 + m, :]
    s8_ref[lo:lo + m, :] = s4_ref[lo - 2:lo - 2 + m, :] + s4_ref[lo + 2:lo + 2 + m, :]
    for src, n, dst in seqs:
        lane = lax.broadcasted_iota(jnp.int32, (n, POOL_WIDTH), 1)
        t = lax.broadcasted_iota(jnp.int32, (n, POOL_WIDTH), 0)
        g0, g1, g2 = lane < POOL_GROUP_DIM, lane < 2 * POOL_GROUP_DIM, lane < 3 * POOL_GROUP_DIM
        half = jnp.where(g0, 1, jnp.where(g1, 2, jnp.where(g2, 4, 8)))
        cnt = (jnp.minimum(t + half, n) - jnp.maximum(t - half, 0)).astype(F32)
        s16 = s8_ref[dst - 4:dst - 4 + n, :] + s8_ref[dst + 4:dst + 4 + n, :]
        wsum = jnp.where(g0, s2_ref[dst:dst + n, :],
                         jnp.where(g1, s4_ref[dst:dst + n, :], jnp.where(g2, s8_ref[dst:dst + n, :], s16)))
        p = wsum / cnt - p_ref[dst:dst + n, :]
        y = jnp.dot(p.astype(BF16), bd_ref[...], preferred_element_type=F32) * ps_ref[...]
        o_ref[src:src + n, :] = y.astype(BF16)


def _pool(pp, pool_bd, pool_scale, ctx_len):
    B, TA, _ = pp.shape
    rows = TA + 3 * POOL_PAD
    kern = functools.partial(_pool_kernel, ctx_len=ctx_len)
    full = pl.BlockSpec((None, TA, POOL_WIDTH), lambda b: (b, 0, 0))
    return pl.pallas_call(
        kern,
        out_shape=jax.ShapeDtypeStruct((B, TA, POOL_WIDTH), BF16),
        grid=(B,),
        in_specs=[full, pl.BlockSpec((POOL_WIDTH, POOL_WIDTH), lambda b: (0, 0)),
                  pl.BlockSpec((1, POOL_WIDTH), lambda b: (0, 0))],
        out_specs=full,
        scratch_shapes=[pltpu.VMEM((rows, POOL_WIDTH), F32)] * 4,
        compiler_params=pltpu.CompilerParams(vmem_limit_bytes=VMEM_LIMIT),
        name="pool_mix",
    )(pp, pool_bd, pool_scale)


def _mixffn_kernel(*refs, ctx_tiles, split, final):
    if split:
        ctx_ref, x_ref = refs[0:2]
        refs = refs[2:]
        from_ctx = pl.program_id(1) < ctx_tiles
        load = lambda bb: jnp.where(from_ctx, ctx_ref[bb], x_ref[bb])
    else:
        x_ref = refs[0]
        refs = refs[1:]
        load = lambda bb: x_ref[bb]
    (att_ref, hf_ref, hb_ref, mo_ref, pool_ref, mod_ref, mln_ref, wo_ref, nf_ref,
     wg_ref, wu_ref, wd_ref, fn_ref, o_ref) = refs
    nb, rows = o_ref.shape[0], o_ref.shape[1]
    lo_mask = lax.broadcasted_iota(jnp.int32, (rows, LANES), 1) < HEAD_DIM

    cats = []
    for bb in range(nb):
        ml = []
        for c in range(ML_WIDTH // LANES):
            sl = slice(c * LANES, (c + 1) * LANES)
            hs = hf_ref[bb, :, sl].astype(F32) + hb_ref[bb, :, sl].astype(F32)
            hn = hs * lax.rsqrt(_segment_ms(hs, lo_mask) + EPS) * mln_ref[:, sl]
            ml.append((hn * _sigmoid(mo_ref[bb, :, sl].astype(F32))).astype(BF16))
        cats.append(jnp.concatenate([att_ref[bb]] + ml + [pool_ref[bb]], axis=-1))
    mixes = [jnp.dot(cat, wo_ref[...], preferred_element_type=F32) for cat in cats]

    x1s, h2s = [], []
    for bb in range(nb):
        x1 = load(bb) + mod_ref[bb, 2] * mixes[bb]
        ms = jnp.mean(x1 * x1, axis=-1, keepdims=True)
        h2s.append((x1 * lax.rsqrt(ms + EPS) * nf_ref[...] * (1.0 + mod_ref[bb, 4]) + mod_ref[bb, 3]).astype(BF16))
        x1s.append(x1)
    downs = []
    for h2 in h2s:
        down = None
        for lo, hi in FF_SPLITS:
            gate = jnp.dot(h2, wg_ref[:, lo:hi], preferred_element_type=F32)
            up = jnp.dot(h2, wu_ref[:, lo:hi], preferred_element_type=F32)
            act = (gate * _sigmoid(gate) * up).astype(BF16)
            part = jnp.dot(act, wd_ref[lo:hi, :], preferred_element_type=F32)
            down = part if down is None else down + part
        downs.append(down)
    for bb in range(nb):
        out = x1s[bb] + mod_ref[bb, 5] * downs[bb]
        if final:
            ms = jnp.mean(out * out, axis=-1, keepdims=True)
            out = out * lax.rsqrt(ms + EPS) * fn_ref[...]
        o_ref[bb] = out


def _mix_ffn(x_in, att, hf, hb, mo, pool, mod5, layer, ml_norm, w_out_r, norm_ffn, wg, wu, wd, final_norm,
             ctx_tiles, final):
    nb = STEP_ROWS
    first = ctx_tiles if final else 0
    x_specs, x_ops, B, TA = _token_specs(x_in, nb, ctx_tiles, first)
    nt = TA // ROW_TILE - first
    row = lambda w: pl.BlockSpec((nb, ROW_TILE, w), lambda b, i: (b, i + first, 0))
    vec = lambda w: pl.BlockSpec((1, w), lambda b, i: (0, 0))
    wspec = lambda r, c: pl.BlockSpec((r, c), lambda b, i: (0, 0), pipeline_mode=pl.Buffered(1))

    kern = functools.partial(_mixffn_kernel, ctx_tiles=ctx_tiles, split=len(x_ops) == 2, final=final)
    return pl.pallas_call(
        kern,
        out_shape=jax.ShapeDtypeStruct((B, nt * ROW_TILE, D_MODEL), F32),
        grid=(B // nb, nt),
        in_specs=x_specs + [pl.BlockSpec((nb, ROW_TILE, ATT_WIDTH), lambda b, i: (b, i, 0)),
                            row(ML_WIDTH), row(ML_WIDTH), row(ML_WIDTH), row(POOL_WIDTH),
                            _mod_spec(layer, nb, ctx_tiles, B // nb, first), vec(ML_WIDTH),
                            wspec(D_MODEL, D_MODEL), vec(D_MODEL), wspec(D_MODEL, D_FF), wspec(D_MODEL, D_FF),
                            wspec(D_FF, D_MODEL), vec(D_MODEL)],
        out_specs=pl.BlockSpec((nb, ROW_TILE, D_MODEL), lambda b, i: (b, i, 0)),
        compiler_params=pltpu.CompilerParams(vmem_limit_bytes=VMEM_LIMIT),
        name="mix_ffn",
    )(*x_ops, att, hf, hb, mo, pool, mod5, ml_norm, w_out_r, norm_ffn, wg, wu, wd, final_norm)


def _rope_tables(t_latent, ctx_len):
    half = HEAD_DIM // 2
    inv_freq = 1.0 / (ROPE_THETA ** (jnp.arange(0, half, 2, dtype=F32) / half))
    rows = t_latent // GRID_W
    row = jnp.repeat(jnp.arange(rows, dtype=F32), GRID_W)
    col = jnp.tile(jnp.arange(GRID_W, dtype=F32), rows)
    a_row = row[:, None] * inv_freq
    a_col = col[:, None] * inv_freq
    ang = jnp.concatenate([a_row, a_row, a_col, a_col], -1)
    cos, sin = jnp.cos(ang), jnp.sin(ang)
    first_half = (jnp.arange(HEAD_DIM) % 32) < 16
    sa = jnp.where(first_half, -sin, 0.0)
    sb = jnp.where(first_half, 0.0, sin)
    pad = lambda a, fill: jnp.tile(jnp.concatenate([jnp.full((ctx_len, HEAD_DIM), fill, F32), a], 0), (1, 2))
    return pad(cos, 1.0), pad(sa, 0.0), pad(sb, 0.0)


def _rearrange_w_in(w_in):
    aq, ak, av, mq, mk, mv, mo, mg, pp = jnp.split(w_in, (512, 640, 768, 1024, 1280, 1536, 1792, 1808), axis=-1)
    mk = mk * (HEAD_DIM ** -0.5)
    w_rows = jnp.concatenate([aq[:, HEAD_PERM], ak, av, mk, mo, pp], axis=-1).astype(BF16)
    w_feat = jnp.concatenate([mq, mv, mg], axis=-1).T.astype(BF16)
    return w_rows, w_feat


def _cumsum_matrices():
    s = np.arange(CHUNK)[:, None]
    t = np.arange(CHUNK)[None, :]
    ones = np.ones((CHUNK, CHUNK), np.float32)
    prefix = np.concatenate([(s <= t).astype(np.float32), ones], axis=1)
    suffix = np.concatenate([(s >= t).astype(np.float32), ones], axis=1)
    return jnp.asarray(np.stack([prefix, suffix]))


def kernel(x, c, ctx, c_ctx, w_ada, b_ada, norm_mix, w_in, b_gates, q_norm, k_norm, ml_norm, pool_w, pool_scale,
           w_out, norm_ffn, w_ffn_gate, w_ffn_up, w_ffn_down, final_norm):
    B, T, _ = x.shape
    ctx_len = ctx.shape[1]
    depth = w_ada.shape[0]
    n_rep = max(STEP_ROWS, INPROJ_ROWS)
    assert ctx_len % ROW_TILE == 0 and T % ROW_TILE == 0 and B % n_rep == 0 and B + n_rep <= MOD_ROWS
    ctx_tiles = ctx_len // ROW_TILE

    c_rep = jnp.broadcast_to(c_ctx[None, :], (n_rep, D_MODEL))
    c_all = jnp.concatenate([c, c_rep, jnp.zeros((MOD_ROWS - B - n_rep, D_MODEL), F32)], 0)
    mod = _modulation(c_all, w_ada, b_ada)
    mod5 = mod.reshape(depth, MOD_ROWS, 6, 1, D_MODEL)
    cos_t, sa_t, sb_t = _rope_tables(T, ctx_len)
    tri = _cumsum_matrices()
    vec = lambda a: a.reshape(1, -1)

    x_all = (ctx, x)
    for l in range(depth):
        final = l == depth - 1
        w_in_r, w_in_t = _rearrange_w_in(w_in[l])
        b_gates_b = jnp.broadcast_to(b_gates[l][:, None], (N_GATES, ROW_TILE))
        qn = jnp.tile(q_norm[l], 2).reshape(1, LANES)
        kn = jnp.tile(k_norm[l], 2).reshape(1, LANES)
        pool_bd = jax.scipy.linalg.block_diag(*[pool_w[l, g] for g in range(len(POOL_WINDOWS))]).astype(BF16)
        w_out_r = jnp.concatenate([w_out[l, :ATT_WIDTH][HEAD_PERM], w_out[l, ATT_WIDTH:]], 0).astype(BF16)

        q, k, v, mk, mo, pp, mqt, mvt, gates = _in_proj(
            x_all, mod5, l, vec(norm_mix[l]), w_in_r, w_in_t, b_gates_b, qn, kn, cos_t, sa_t, sb_t, ctx_tiles)
        att = _attention(q, k, v, ctx_len, with_ctx=not final)
        hf, hb = _mlstm(mqt, mk, mvt, gates, tri, ctx_len)
        pool = _pool(pp, pool_bd, vec(pool_scale[l]), ctx_len)
        x_all = _mix_ffn(x_all, att, hf, hb, mo, pool, mod5, l, vec(ml_norm[l]), w_out_r, vec(norm_ffn[l]),
                         w_ffn_gate[l].astype(BF16), w_ffn_up[l].astype(BF16), w_ffn_down[l].astype(BF16),
                         vec(final_norm), ctx_tiles, final)
    return x_all
```

```python
import functools

import jax
import jax.numpy as jnp
import numpy as np
from jax import lax
from jax.experimental import pallas as pl
from jax.experimental.pallas import tpu as pltpu

F32 = jnp.float32
BF16 = jnp.bfloat16

D_MODEL = 1024
HEAD_DIM = 64
GRID_W = 64
ATT_WIDTH = 512
KV_WIDTH = 128
ML_WIDTH = 256
POOL_WIDTH = 256
ATT_HEADS = 8
ML_HEADS = 4
N_GATES = 16
POOL_WINDOWS = (2, 4, 8, 16)
POOL_GROUP_DIM = 64
D_FF = 2816
CHUNK = 128
ROPE_THETA = 10000.0
EPS = 1e-6

LANES = 128
ROW_TILE = 256
MOD_ROWS = 24
POOL_PAD = 16
VMEM_LIMIT = 52 * 1024 * 1024

OFF_Q, OFF_K, OFF_V, OFF_MK, OFF_MO, OFF_PP = 0, 512, 640, 768, 1024, 1280
IN_COLS = OFF_PP + POOL_WIDTH
ROW_MQ, ROW_MV, ROW_G = 0, 256, 512
T_ROWS = ROW_G + N_GATES
Q_SCALE = HEAD_DIM ** -0.5 * float(np.log2(np.e))
VA_ROWS = 80
ATT_GROUP = 2
MLSTM_ROWS = 8
STEP_ROWS = 2
INPROJ_ROWS = 4
FF_SPLITS = ((0, 1536), (1536, D_FF))

HEAD_PERM = np.concatenate(
    [np.concatenate([np.arange(c * 64, (c + 1) * 64), np.arange((4 + c) * 64, (5 + c) * 64)]) for c in range(4)])


def _sigmoid(x):
    return 1.0 / (1.0 + jnp.exp(-x))


def _split3(a):
    hi = a.astype(BF16)
    r1 = a - hi.astype(F32)
    mid = r1.astype(BF16)
    lo = (r1 - mid.astype(F32)).astype(BF16)
    return hi, mid, lo


def _mod_kernel(c_ref, w_ref, b_ref, o_ref):
    c = c_ref[...]
    s = c * _sigmoid(c)
    s_hi, s_mid, _ = _split3(s)
    w = w_ref[...]
    w_hi = w.astype(BF16)
    w_mid = (w - w_hi.astype(F32)).astype(BF16)
    dot = functools.partial(jnp.dot, preferred_element_type=F32)
    acc = dot(s_mid, w_hi) + dot(s_hi, w_mid)
    acc = acc + dot(s_hi, w_hi)
    o_ref[...] = acc + b_ref[...]


def _modulation(c_all, w_ada, b_ada):
    depth = w_ada.shape[0]
    tn = 1024
    return pl.pallas_call(
        _mod_kernel,
        out_shape=jax.ShapeDtypeStruct((depth, MOD_ROWS, 6 * D_MODEL), F32),
        grid=(depth, 6 * D_MODEL // tn),
        in_specs=[
            pl.BlockSpec((MOD_ROWS, D_MODEL), lambda l, j: (0, 0)),
            pl.BlockSpec((None, D_MODEL, tn), lambda l, j: (l, 0, j)),
            pl.BlockSpec((None, 1, tn), lambda l, j: (l, 0, j)),
        ],
        out_specs=pl.BlockSpec((None, MOD_ROWS, tn), lambda l, j: (l, 0, j)),
        name="adaln_mod",
    )(c_all, w_ada, b_ada.reshape(depth, 1, 6 * D_MODEL))


def _segment_ms(x, lo_mask):
    x2 = x * x
    s_lo = jnp.sum(jnp.where(lo_mask, x2, 0.0), axis=-1, keepdims=True)
    s_hi = jnp.sum(jnp.where(lo_mask, 0.0, x2), axis=-1, keepdims=True)
    return jnp.where(lo_mask, s_lo, s_hi) * (1.0 / HEAD_DIM)


def _inproj_kernel(*refs, ctx_tiles, split):
    if split:
        ctx_ref, x_ref = refs[0:2]
        refs = refs[2:]
        from_ctx = pl.program_id(1) < ctx_tiles
        load = lambda bb: jnp.where(from_ctx, ctx_ref[bb], x_ref[bb])
    else:
        x_ref = refs[0]
        refs = refs[1:]
        load = lambda bb: x_ref[bb]
    (mod_ref, nm_ref, w_ref, wt_ref, bg_ref, qn_ref, kn_ref, cos_ref, sa_ref, sb_ref,
     q_ref, k_ref, v_ref, mk_ref, mo_ref, pp_ref, mqt_ref, mvt_ref, g_ref) = refs
    nb, rows = q_ref.shape[0], q_ref.shape[1]

    hs = []
    for bb in range(nb):
        x = load(bb)
        ms = jnp.mean(x * x, axis=-1, keepdims=True)
        gain = nm_ref[...] * (1.0 + mod_ref[bb, 1])
        hs.append((x * lax.rsqrt(ms + EPS) * gain + mod_ref[bb, 0]).astype(BF16))
    accs = [jnp.dot(h, w_ref[...], preferred_element_type=F32) for h in hs]
    accs_t = [lax.dot_general(wt_ref[...], h, (((1,), (1,)), ((), ())), preferred_element_type=F32)
              for h in hs]

    lo_mask = lax.broadcasted_iota(jnp.int32, (rows, LANES), 1) < HEAD_DIM
    cos, sa, sb = cos_ref[...], sa_ref[...], sb_ref[...]

    def norm_rope(xc, gain):
        yc = xc * lax.rsqrt(_segment_ms(xc, lo_mask) + EPS) * gain
        return yc * cos + pltpu.roll(yc, LANES - 16, 1) * sa + pltpu.roll(yc, 16, 1) * sb

    for bb in range(nb):
        acc, acc_t = accs[bb], accs_t[bb]
        blk = lambda off, width: acc[:, off:off + width]
        for c in range(ATT_WIDTH // LANES):
            qc = norm_rope(blk(OFF_Q + c * LANES, LANES), qn_ref[...])
            q_ref[bb, :, c * LANES:(c + 1) * LANES] = (qc * Q_SCALE).astype(BF16)
        k_ref[bb] = norm_rope(blk(OFF_K, LANES), kn_ref[...]).astype(BF16)
        v_ref[bb] = blk(OFF_V, KV_WIDTH).astype(BF16)
        mk_ref[bb] = blk(OFF_MK, ML_WIDTH).astype(BF16)
        mo_ref[bb] = blk(OFF_MO, ML_WIDTH).astype(BF16)
        pp_ref[bb] = blk(OFF_PP, POOL_WIDTH).astype(BF16)
        mqt_ref[bb] = acc_t[ROW_MQ:ROW_MQ + ML_WIDTH, :].astype(BF16)
        mvt_ref[bb] = acc_t[ROW_MV:ROW_MV + ML_WIDTH, :].astype(BF16)
        g_ref[bb] = acc_t[ROW_G:ROW_G + N_GATES, :] + bg_ref[...]


def _mod_spec(layer, nb, ctx_tiles, ctx_block, first=0):
    def idx(b, i):
        return (layer, jnp.where(i + first < ctx_tiles, ctx_block, b), 0, 0, 0)
    return pl.BlockSpec((None, nb, 6, 1, D_MODEL), idx)


def _token_specs(x_in, nb, ctx_tiles, first=0):
    if isinstance(x_in, tuple):
        ctx, x = x_in
        specs = [pl.BlockSpec((nb, ROW_TILE, D_MODEL), lambda b, i: (b, jnp.minimum(i, ctx_tiles - 1), 0)),
                 pl.BlockSpec((nb, ROW_TILE, D_MODEL), lambda b, i: (b, jnp.maximum(i - ctx_tiles, 0), 0))]
        return specs, [ctx, x], ctx.shape[0], ctx.shape[1] + x.shape[1]
    specs = [pl.BlockSpec((nb, ROW_TILE, D_MODEL), lambda b, i: (b, i + first, 0))]
    return specs, [x_in], x_in.shape[0], x_in.shape[1]


def _in_proj(x_in, mod5, layer, norm_mix, w_in_r, w_in_t, b_gates_b, qn, kn, cos_t, sa_t, sb_t, ctx_tiles):
    nb = INPROJ_ROWS
    x_specs, x_ops, B, TA = _token_specs(x_in, nb, ctx_tiles)
    nt = TA // ROW_TILE
    row = lambda w: pl.BlockSpec((nb, ROW_TILE, w), lambda b, i: (b, i, 0))
    col = lambda r: pl.BlockSpec((nb, r, ROW_TILE), lambda b, i: (b, 0, i))
    vec = lambda w: pl.BlockSpec((None, 1, w), lambda b, i: (layer, 0, 0))
    full = lambda r, c: pl.BlockSpec((None, r, c), lambda b, i: (layer, 0, 0))
    tab = pl.BlockSpec((ROW_TILE, LANES), lambda b, i: (i, 0))
    row_widths = (ATT_WIDTH, KV_WIDTH, KV_WIDTH, ML_WIDTH, ML_WIDTH, POOL_WIDTH)
    col_heights = (ML_WIDTH, ML_WIDTH)
    out_shapes = [jax.ShapeDtypeStruct((B, TA, w), BF16) for w in row_widths]
    out_shapes += [jax.ShapeDtypeStruct((B, r, TA), BF16) for r in col_heights]
    out_shapes.append(jax.ShapeDtypeStruct((B, N_GATES, TA), F32))
    out_specs = [row(w) for w in row_widths] + [col(r) for r in col_heights] + [col(N_GATES)]
    kern = functools.partial(_inproj_kernel, ctx_tiles=ctx_tiles, split=len(x_ops) == 2)
    return pl.pallas_call(
        kern,
        out_shape=out_shapes,
        grid=(B // nb, nt),
        in_specs=x_specs + [_mod_spec(layer, nb, ctx_tiles, B // nb), vec(D_MODEL), full(D_MODEL, IN_COLS),
                            full(T_ROWS, D_MODEL), full(N_GATES, ROW_TILE), vec(LANES), vec(LANES), tab, tab, tab],
        out_specs=out_specs,
        compiler_params=pltpu.CompilerParams(vmem_limit_bytes=VMEM_LIMIT),
        name="in_proj",
    )(*x_ops, mod5, norm_mix, w_in_r, w_in_t, b_gates_b, qn, kn, cos_t, sa_t, sb_t)


def _attn_kernel(q_ref, k_ref, v_ref, o_ref, k0_ref, k1_ref, va_ref, *, ctx_len, first_tile):
    i = pl.program_id(1)
    total = k_ref.shape[0]

    @pl.when(i == 0)
    def _():
        k = k_ref[...]
        lo = lax.broadcasted_iota(jnp.int32, k.shape, 1) < HEAD_DIM
        zero = jnp.zeros_like(k)
        k0_ref[...] = jnp.where(lo, k, zero)
        k1_ref[...] = jnp.where(lo, zero, k)
        va_ref[:, 0:LANES] = v_ref[...]
        va_ref[:, LANES:2 * LANES] = jnp.ones((total, LANES), BF16)

    rows = q_ref.shape[0]
    lo_mask = lax.broadcasted_iota(jnp.int32, (rows, LANES), 1) < HEAD_DIM

    def block(nk):
        chunks = list(range(ATT_WIDTH // LANES))
        for g in range(0, len(chunks), ATT_GROUP):
            cs = chunks[g:g + ATT_GROUP]
            ss = [lax.dot_general(q_ref[:, c * LANES:(c + 1) * LANES], kr[0:nk, :], (((1,), (1,)), ((), ())),
                                  preferred_element_type=F32) for c in cs for kr in (k0_ref, k1_ref)]
            ms = [jnp.max(s, axis=-1, keepdims=True) for s in ss]
            ps = [jnp.exp2(s - m).astype(BF16) for s, m in zip(ss, ms)]
            acs = [jnp.dot(p, va_ref[0:nk, :], preferred_element_type=F32) for p in ps]
            outs = [a[:, 0:LANES] / a[:, LANES:2 * LANES] for a in acs]
            for n, c in enumerate(cs):
                o_ref[:, c * LANES:(c + 1) * LANES] = jnp.where(lo_mask, outs[2 * n], outs[2 * n + 1]).astype(BF16)

    if first_tile == 0:
        pl.when(i == 0)(lambda: block(ctx_len))
        pl.when(i > 0)(lambda: block(total))
    else:
        block(total)


def _attention(q, k, v, ctx_len, with_ctx):
    B, TA, _ = q.shape
    first = 0 if with_ctx else ctx_len // ROW_TILE
    nt = TA // ROW_TILE - first
    kern = functools.partial(_attn_kernel, ctx_len=ctx_len, first_tile=first)
    return pl.pallas_call(
        kern,
        out_shape=jax.ShapeDtypeStruct((B, nt * ROW_TILE, ATT_WIDTH), BF16),
        grid=(B, nt),
        in_specs=[pl.BlockSpec((None, ROW_TILE, ATT_WIDTH), lambda b, i: (b, i + first, 0)),
                  pl.BlockSpec((None, TA, KV_WIDTH), lambda b, i: (b, 0, 0)),
                  pl.BlockSpec((None, TA, KV_WIDTH), lambda b, i: (b, 0, 0))],
        out_specs=pl.BlockSpec((None, ROW_TILE, ATT_WIDTH), lambda b, i: (b, i, 0)),
        scratch_shapes=[pltpu.VMEM((TA, KV_WIDTH), BF16), pltpu.VMEM((TA, KV_WIDTH), BF16),
                        pltpu.VMEM((TA, 2 * LANES), BF16)],
        compiler_params=pltpu.CompilerParams(vmem_limit_bytes=VMEM_LIMIT),
        name="attention",
    )(q, k, v)


def _mlstm_kernel(qf_ref, kf_ref, vf_ref, gf_ref, qb_ref, kb_ref, vb_ref, gb_ref, tri_ref, hf_ref, hb_ref,
                  st_ref, m_ref, r_ref):
    i = pl.program_id(1)

    @pl.when(i == 0)
    def _():
        st_ref[...] = jnp.zeros_like(st_ref)
        m_ref[...] = jnp.zeros_like(m_ref)
        r_ref[...] = jnp.zeros_like(r_ref)

    L = CHUNK
    s_idx = lax.broadcasted_iota(jnp.int32, (L, L), 0)
    t_idx = lax.broadcasted_iota(jnp.int32, (L, L), 1)
    first_head_rows = lax.broadcasted_iota(jnp.int32, (LANES, L), 0) < HEAD_DIM
    ones_blk = jnp.where(lax.broadcasted_iota(jnp.int32, (VA_ROWS - HEAD_DIM, L), 0) == 0, 1.0, 0.0).astype(BF16)

    dirs = ((qf_ref, kf_ref, vf_ref, gf_ref, hf_ref), (qb_ref, kb_ref, vb_ref, gb_ref, hb_ref))
    chains = [divmod(bd, 2) for bd in range(2 * hf_ref.shape[0])]
    units = [(bd, hd) for bd in range(len(chains)) for hd in range(ML_HEADS)]
    refs = [tuple(ref.at[bb] for ref in dirs[d]) for bb, d in chains]

    qm, kp, va, st, s_raw, a1 = {}, {}, {}, {}, {}, {}
    for bd, hd in units:
        q_r, k_r, v_r = refs[bd][0:3]
        pair, second = divmod(hd, 2)
        kp[bd, hd] = k_r[:, pair * LANES:(pair + 1) * LANES]
        q_pair = q_r[pair * LANES:(pair + 1) * LANES, :]
        keep = jnp.logical_not(first_head_rows) if second else first_head_rows
        qm[bd, hd] = jnp.where(keep, q_pair, jnp.zeros_like(q_pair))
        va[bd, hd] = jnp.concatenate([v_r[hd * HEAD_DIM:(hd + 1) * HEAD_DIM, :], ones_blk], axis=0)
        st[bd, hd] = st_ref[bd * ML_HEADS + hd]
        s_raw[bd, hd] = jnp.dot(kp[bd, hd], qm[bd, hd], preferred_element_type=F32)
        a1[bd, hd] = jnp.dot(st[bd, hd].astype(BF16), qm[bd, hd], preferred_element_type=F32)

    g8 = [refs[bd][3][8 * d:8 * d + 8, :] for bd, (bb, d) in enumerate(chains)]
    lf8 = [jnp.minimum(g, 0.0) - jnp.log1p(jnp.exp(-jnp.abs(g))) for g in g8]
    parts = [jnp.concatenate([p.astype(F32) for p in _split3(lf)], axis=0) for lf in lf8]
    cum = [jnp.dot(p, tri_ref[d], preferred_element_type=F32) for p, (bb, d) in zip(parts, chains)]
    cum = [c[16:24] + c[8:16] + c[0:8] for c in cum]
    b8 = [c[:, 0:L] for c in cum]
    btot8 = [c[:, L:2 * L] for c in cum]
    li8 = [pltpu.roll(g, ML_HEADS, 0) for g in g8]
    for bd in range(len(chains)):
        r_ref[bd, 0:8, :] = li8[bd] - b8[bd]
    ct = [r_ref[bd].T for bd in range(len(chains))]
    m_prev8 = [m_ref[bd] for bd in range(len(chains))]
    gg8 = [bt - b + li for bt, b, li in zip(btot8, b8, li8)]
    m_new8 = [jnp.maximum(bt + mp, jnp.max(gg, axis=-1, keepdims=True)) for bt, mp, gg in zip(btot8, m_prev8, gg8)]
    a8 = [jnp.exp(bt + mp - mn) for bt, mp, mn in zip(btot8, m_prev8, m_new8)]
    wk8 = [jnp.exp(gg - mn) for gg, mn in zip(gg8, m_new8)]
    inter8 = [b + mp for b, mp in zip(b8, m_prev8)]
    for bd in range(len(chains)):
        m_ref[bd] = m_new8[bd]

    dmat, m_t, s_t, a2, upd = {}, {}, {}, {}, {}
    for bd, hd in units:
        r = ML_HEADS + hd
        causal = (s_idx <= t_idx) if chains[bd][1] == 0 else (s_idx >= t_idx)
        dmat[bd, hd] = jnp.where(causal, ct[bd][:, r:r + 1] + b8[bd][r:r + 1], -jnp.inf)
        m_t[bd, hd] = jnp.maximum(inter8[bd][r:r + 1], jnp.max(dmat[bd, hd], axis=0, keepdims=True))
    for bd, hd in units:
        s_t[bd, hd] = (s_raw[bd, hd] * jnp.exp(dmat[bd, hd] - m_t[bd, hd])).astype(BF16)
    for bd, hd in units:
        r = ML_HEADS + hd
        a2[bd, hd] = jnp.dot(va[bd, hd], s_t[bd, hd], preferred_element_type=F32)
        vw = (va[bd, hd].astype(F32) * wk8[bd][r:r + 1]).astype(BF16)
        upd[bd, hd] = jnp.dot(vw, kp[bd, hd], preferred_element_type=F32)
    hts = {}
    for bd, hd in units:
        r = ML_HEADS + hd
        inter = inter8[bd][r:r + 1]
        nd = jnp.exp(inter - m_t[bd, hd]) * a1[bd, hd] + a2[bd, hd]
        den = nd[HEAD_DIM:HEAD_DIM + 1, :]
        hts[bd, hd] = nd[0:HEAD_DIM, :] / jnp.maximum(jnp.abs(den), jnp.exp(-m_t[bd, hd]))
        st_ref[bd * ML_HEADS + hd] = a8[bd][r:r + 1] * st[bd, hd] + upd[bd, hd]
    for bd in range(len(chains)):
        refs[bd][4][...] = jnp.concatenate([hts[bd, hd] for hd in range(ML_HEADS)], axis=0).T.astype(BF16)


def _mlstm(mqt, mk, mvt, gates, tri, ctx_len):
    B, TA, _ = mk.shape
    nc = TA // CHUNK
    ncc = ctx_len // CHUNK

    def fwd(i):
        return i

    def bwd(i):
        return jnp.where(i < ncc, ncc - 1 - i, nc - 1 - (i - ncc))

    nb = MLSTM_ROWS if B % MLSTM_ROWS == 0 else 1
    tok = lambda f: pl.BlockSpec((nb, CHUNK, ML_WIDTH), lambda b, i: (b, f(i), 0))
    feat = lambda f, r: pl.BlockSpec((nb, r, CHUNK), lambda b, i: (b, 0, f(i)))
    n_units = 2 * nb * ML_HEADS
    return pl.pallas_call(
        _mlstm_kernel,
        out_shape=[jax.ShapeDtypeStruct((B, TA, ML_WIDTH), BF16)] * 2,
        grid=(B // nb, nc),
        in_specs=[feat(fwd, ML_WIDTH), tok(fwd), feat(fwd, ML_WIDTH), feat(fwd, N_GATES),
                  feat(bwd, ML_WIDTH), tok(bwd), feat(bwd, ML_WIDTH), feat(bwd, N_GATES),
                  pl.BlockSpec((2, CHUNK, 2 * CHUNK), lambda b, i: (0, 0, 0))],
        out_specs=[tok(fwd), tok(bwd)],
        scratch_shapes=[pltpu.VMEM((n_units, VA_ROWS, LANES), F32), pltpu.VMEM((2 * nb, 8, LANES), F32),
                        pltpu.VMEM((2 * nb, CHUNK, LANES), F32)],
        compiler_params=pltpu.CompilerParams(vmem_limit_bytes=VMEM_LIMIT),
        name="mlstm",
    )(mqt, mk, mvt, gates, mqt, mk, mvt, gates, tri)


def _pool_kernel(pp_ref, bd_ref, ps_ref, o_ref, p_ref, s2_ref, s4_ref, s8_ref, *, ctx_len):
    total = pp_ref.shape[0]
    seqs = ((0, ctx_len, POOL_PAD), (ctx_len, total - ctx_len, 2 * POOL_PAD + ctx_len))
    rows = p_ref.shape[0]
    for ref in (p_ref, s2_ref, s4_ref, s8_ref):
        ref[...] = jnp.zeros_like(ref)
    for src, n, dst in seqs:
        p_ref[dst:dst + n, :] = pp_ref[src:src + n, :].astype(F32)
    lo, m = 8, rows - 16
    s2_ref[lo:lo + m, :] = p_ref[lo - 1:lo - 1 + m, :] + p_ref[lo:lo + m, :]
    s4_ref[lo:lo + m, :] = s2_ref[lo - 1:lo - 1 + m, :] + s2_ref[lo + 1:lo + 1 + m, :]
    s8_ref[lo:lo + m, :] = s4_ref[lo - 2:lo - 2 + m, :] + s4_ref[lo + 2:lo + 2 + m, :]
    for src, n, dst in seqs:
        lane = lax.broadcasted_iota(jnp.int32, (n, POOL_WIDTH), 1)
        t = lax.broadcasted_iota(jnp.int32, (n, POOL_WIDTH), 0)
        g0, g1, g2 = lane < POOL_GROUP_DIM, lane < 2 * POOL_GROUP_DIM, lane < 3 * POOL_GROUP_DIM
        half = jnp.where(g0, 1, jnp.where(g1, 2, jnp.where(g2, 4, 8)))
        cnt = (jnp.minimum(t + half, n) - jnp.maximum(t - half, 0)).astype(F32)
        s16 = s8_ref[dst - 4:dst - 4 + n, :] + s8_ref[dst + 4:dst + 4 + n, :]
        wsum = jnp.where(g0, s2_ref[dst:dst + n, :],
                         jnp.where(g1, s4_ref[dst:dst + n, :], jnp.where(g2, s8_ref[dst:dst + n, :], s16)))
        p = wsum / cnt - p_ref[dst:dst + n, :]
        y = jnp.dot(p.astype(BF16), bd_ref[...], preferred_element_type=F32) * ps_ref[...]
        o_ref[src:src + n, :] = y.astype(BF16)


def _pool(pp, pool_bd, pool_scale, layer, ctx_len):
    B, TA, _ = pp.shape
    rows = TA + 3 * POOL_PAD
    kern = functools.partial(_pool_kernel, ctx_len=ctx_len)
    full = pl.BlockSpec((None, TA, POOL_WIDTH), lambda b: (b, 0, 0))
    return pl.pallas_call(
        kern,
        out_shape=jax.ShapeDtypeStruct((B, TA, POOL_WIDTH), BF16),
        grid=(B,),
        in_specs=[full, pl.BlockSpec((None, POOL_WIDTH, POOL_WIDTH), lambda b: (layer, 0, 0)),
                  pl.BlockSpec((None, 1, POOL_WIDTH), lambda b: (layer, 0, 0))],
        out_specs=full,
        scratch_shapes=[pltpu.VMEM((rows, POOL_WIDTH), F32)] * 4,
        compiler_params=pltpu.CompilerParams(vmem_limit_bytes=VMEM_LIMIT),
        name="pool_mix",
    )(pp, pool_bd, pool_scale)


def _mixffn_kernel(*refs, ctx_tiles, split, final):
    if split:
        ctx_ref, x_ref = refs[0:2]
        refs = refs[2:]
        from_ctx = pl.program_id(1) < ctx_tiles
        load = lambda bb: jnp.where(from_ctx, ctx_ref[bb], x_ref[bb])
    else:
        x_ref = refs[0]
        refs = refs[1:]
        load = lambda bb: x_ref[bb]
    (att_ref, hf_ref, hb_ref, mo_ref, pool_ref, mod_ref, mln_ref, wo_ref, nf_ref,
     wg_ref, wu_ref, wd_ref, fn_ref, o_ref) = refs
    nb, rows = o_ref.shape[0], o_ref.shape[1]
    lo_mask = lax.broadcasted_iota(jnp.int32, (rows, LANES), 1) < HEAD_DIM

    cats = []
    for bb in range(nb):
        ml = []
        for c in range(ML_WIDTH // LANES):
            sl = slice(c * LANES, (c + 1) * LANES)
            hs = hf_ref[bb, :, sl].astype(F32) + hb_ref[bb, :, sl].astype(F32)
            hn = hs * lax.rsqrt(_segment_ms(hs, lo_mask) + EPS) * mln_ref[:, sl]
            ml.append((hn * _sigmoid(mo_ref[bb, :, sl].astype(F32))).astype(BF16))
        cats.append(jnp.concatenate([att_ref[bb]] + ml + [pool_ref[bb]], axis=-1))
    mixes = [jnp.dot(cat, wo_ref[...], preferred_element_type=F32) for cat in cats]

    x1s, h2s = [], []
    for bb in range(nb):
        x1 = load(bb) + mod_ref[bb, 2] * mixes[bb]
        ms = jnp.mean(x1 * x1, axis=-1, keepdims=True)
        gain = nf_ref[...] * (1.0 + mod_ref[bb, 4])
        h2s.append((x1 * lax.rsqrt(ms + EPS) * gain + mod_ref[bb, 3]).astype(BF16))
        x1s.append(x1)
    downs = []
    for h2 in h2s:
        down = None
        for lo, hi in FF_SPLITS:
            gate = jnp.dot(h2, wg_ref[:, lo:hi], preferred_element_type=F32)
            up = jnp.dot(h2, wu_ref[:, lo:hi], preferred_element_type=F32)
            act = (gate * _sigmoid(gate) * up).astype(BF16)
            part = jnp.dot(act, wd_ref[lo:hi, :], preferred_element_type=F32)
            down = part if down is None else down + part
        downs.append(down)
    for bb in range(nb):
        out = x1s[bb] + mod_ref[bb, 5] * downs[bb]
        if final:
            ms = jnp.mean(out * out, axis=-1, keepdims=True)
            out = out * lax.rsqrt(ms + EPS) * fn_ref[...]
        o_ref[bb] = out


def _mix_ffn(x_in, att, hf, hb, mo, pool, mod5, layer, ml_norm, w_out_r, norm_ffn, wg, wu, wd, final_norm,
             ctx_tiles, final):
    nb = STEP_ROWS
    first = ctx_tiles if final else 0
    x_specs, x_ops, B, TA = _token_specs(x_in, nb, ctx_tiles, first)
    nt = TA // ROW_TILE - first
    row = lambda w: pl.BlockSpec((nb, ROW_TILE, w), lambda b, i: (b, i + first, 0))
    vec = lambda w: pl.BlockSpec((None, 1, w), lambda b, i: (layer, 0, 0))
    wspec = lambda r, c: pl.BlockSpec((None, r, c), lambda b, i: (layer, 0, 0), pipeline_mode=pl.Buffered(1))

    kern = functools.partial(_mixffn_kernel, ctx_tiles=ctx_tiles, split=len(x_ops) == 2, final=final)
    return pl.pallas_call(
        kern,
        out_shape=jax.ShapeDtypeStruct((B, nt * ROW_TILE, D_MODEL), F32),
        grid=(B // nb, nt),
        in_specs=x_specs + [pl.BlockSpec((nb, ROW_TILE, ATT_WIDTH), lambda b, i: (b, i, 0)),
                            row(ML_WIDTH), row(ML_WIDTH), row(ML_WIDTH), row(POOL_WIDTH),
                            _mod_spec(layer, nb, ctx_tiles, B // nb, first), vec(ML_WIDTH),
                            wspec(D_MODEL, D_MODEL), vec(D_MODEL), wspec(D_MODEL, D_FF), wspec(D_MODEL, D_FF),
                            wspec(D_FF, D_MODEL), pl.BlockSpec((1, D_MODEL), lambda b, i: (0, 0))],
        out_specs=pl.BlockSpec((nb, ROW_TILE, D_MODEL), lambda b, i: (b, i, 0)),
        compiler_params=pltpu.CompilerParams(vmem_limit_bytes=VMEM_LIMIT),
        name="mix_ffn",
    )(*x_ops, att, hf, hb, mo, pool, mod5, ml_norm, w_out_r, norm_ffn, wg, wu, wd, final_norm)


def _rope_tables(t_latent, ctx_len):
    half = HEAD_DIM // 2
    inv_freq = 1.0 / (ROPE_THETA ** (jnp.arange(0, half, 2, dtype=F32) / half))
    rows = t_latent // GRID_W
    row = jnp.repeat(jnp.arange(rows, dtype=F32), GRID_W)
    col = jnp.tile(jnp.arange(GRID_W, dtype=F32), rows)
    a_row = row[:, None] * inv_freq
    a_col = col[:, None] * inv_freq
    ang = jnp.concatenate([a_row, a_row, a_col, a_col], -1)
    cos, sin = jnp.cos(ang), jnp.sin(ang)
    first_half = (jnp.arange(HEAD_DIM) % 32) < 16
    sa = jnp.where(first_half, -sin, 0.0)
    sb = jnp.where(first_half, 0.0, sin)
    pad = lambda a, fill: jnp.tile(jnp.concatenate([jnp.full((ctx_len, HEAD_DIM), fill, F32), a], 0), (1, 2))
    return pad(cos, 1.0), pad(sa, 0.0), pad(sb, 0.0)


_IN = np.arange(2064)
COLS_TOKEN = np.concatenate([HEAD_PERM, _IN[512:768], _IN[1024:1280], _IN[1536:1792], _IN[1808:2064]])
COLS_FEATURE = np.concatenate([_IN[768:1024], _IN[1280:1536], _IN[1792:1808]])
COL_SCALE = np.where((np.arange(IN_COLS) >= OFF_MK) & (np.arange(IN_COLS) < OFF_MO), HEAD_DIM ** -0.5, 1.0).astype(np.float32)
ROWS_OUT = np.concatenate([HEAD_PERM, np.arange(ATT_WIDTH, D_MODEL)])


def _take_runs(a, index, axis):
    cuts = [0] + [n for n in range(1, len(index)) if index[n] != index[n - 1] + 1] + [len(index)]
    runs = [lax.slice_in_dim(a, int(index[s]), int(index[e - 1]) + 1, axis=axis) for s, e in zip(cuts[:-1], cuts[1:])]
    return jnp.concatenate(runs, axis=axis)


def _prepare_params(w_in, b_gates, q_norm, k_norm, pool_w, w_out, w_ffn_gate, w_ffn_up, w_ffn_down):
    depth = w_in.shape[0]
    groups = len(POOL_WINDOWS)
    eye = jnp.eye(groups, dtype=F32)
    return dict(
        w_rows=(_take_runs(w_in, COLS_TOKEN, 2) * COL_SCALE).astype(BF16),
        w_feat=jnp.swapaxes(_take_runs(w_in, COLS_FEATURE, 2), 1, 2).astype(BF16),
        b_gates=jnp.broadcast_to(b_gates[:, :, None], (depth, N_GATES, ROW_TILE)),
        qn=jnp.tile(q_norm, (1, 2))[:, None, :], kn=jnp.tile(k_norm, (1, 2))[:, None, :],
        pool_bd=(pool_w[:, :, :, None, :] * eye[None, :, None, :, None]).reshape(depth, POOL_WIDTH, POOL_WIDTH).astype(BF16),
        w_out=_take_runs(w_out, ROWS_OUT, 1).astype(BF16),
        wg=w_ffn_gate.astype(BF16), wu=w_ffn_up.astype(BF16), wd=w_ffn_down.astype(BF16))


def _cumsum_matrices():
    s = np.arange(CHUNK)[:, None]
    t = np.arange(CHUNK)[None, :]
    ones = np.ones((CHUNK, CHUNK), np.float32)
    prefix = np.concatenate([(s <= t).astype(np.float32), ones], axis=1)
    suffix = np.concatenate([(s >= t).astype(np.float32), ones], axis=1)
    return jnp.asarray(np.stack([prefix, suffix]))


def kernel(x, c, ctx, c_ctx, w_ada, b_ada, norm_mix, w_in, b_gates, q_norm, k_norm, ml_norm, pool_w, pool_scale,
           w_out, norm_ffn, w_ffn_gate, w_ffn_up, w_ffn_down, final_norm):
    B, T, _ = x.shape
    ctx_len = ctx.shape[1]
    depth = w_ada.shape[0]
    n_rep = max(STEP_ROWS, INPROJ_ROWS)
    assert ctx_len % ROW_TILE == 0 and T % ROW_TILE == 0 and B % n_rep == 0 and B + n_rep <= MOD_ROWS
    ctx_tiles = ctx_len // ROW_TILE

    c_rep = jnp.broadcast_to(c_ctx[None, :], (n_rep, D_MODEL))
    c_all = jnp.concatenate([c, c_rep, jnp.zeros((MOD_ROWS - B - n_rep, D_MODEL), F32)], 0)
    mod = _modulation(c_all, w_ada, b_ada)
    mod5 = mod.reshape(depth, MOD_ROWS, 6, 1, D_MODEL)
    cos_t, sa_t, sb_t = _rope_tables(T, ctx_len)
    tri = _cumsum_matrices()
    p = _prepare_params(w_in, b_gates, q_norm, k_norm, pool_w, w_out, w_ffn_gate, w_ffn_up, w_ffn_down)
    stack = lambda a: a[:, None, :]

    x_all = (ctx, x)
    for l in range(depth):
        final = l == depth - 1
        q, k, v, mk, mo, pp, mqt, mvt, gates = _in_proj(
            x_all, mod5, l, stack(norm_mix), p["w_rows"], p["w_feat"], p["b_gates"], p["qn"], p["kn"],
            cos_t, sa_t, sb_t, ctx_tiles)
        att = _attention(q, k, v, ctx_len, with_ctx=not final)
        hf, hb = _mlstm(mqt, mk, mvt, gates, tri, ctx_len)
        pool = _pool(pp, p["pool_bd"], stack(pool_scale), l, ctx_len)
        x_all = _mix_ffn(x_all, att, hf, hb, mo, pool, mod5, l, stack(ml_norm), p["w_out"], stack(norm_ffn),
                         p["wg"], p["wu"], p["wd"], final_norm.reshape(1, -1), ctx_tiles, final)
    return x_all
```

```python
import functools

import jax
import jax.numpy as jnp
import numpy as np
from jax import lax
from jax.experimental import pallas as pl
from jax.experimental.pallas import tpu as pltpu

F32 = jnp.float32
BF16 = jnp.bfloat16

D_MODEL = 1024
HEAD_DIM = 64
GRID_W = 64
ATT_WIDTH = 512
KV_WIDTH = 128
ML_WIDTH = 256
POOL_WIDTH = 256
ATT_HEADS = 8
ML_HEADS = 4
N_GATES = 16
POOL_WINDOWS = (2, 4, 8, 16)
POOL_GROUP_DIM = 64
D_FF = 2816
CHUNK = 128
ROPE_THETA = 10000.0
EPS = 1e-6

LANES = 128
ROW_TILE = 256
MOD_ROWS = 24
POOL_PAD = 16
VMEM_LIMIT = 52 * 1024 * 1024

OFF_Q, OFF_K, OFF_V, OFF_MK, OFF_MO, OFF_PP = 0, 512, 640, 768, 1024, 1280
IN_COLS = OFF_PP + POOL_WIDTH
ROW_MQ, ROW_MV, ROW_G = 0, 256, 512
T_ROWS = ROW_G + N_GATES
Q_SCALE = HEAD_DIM ** -0.5 * float(np.log2(np.e))
VA_ROWS = 80
ATT_GROUP = 2
MLSTM_ROWS = 8
STEP_ROWS = 2
INPROJ_ROWS = 4
FF_SPLITS = ((0, 1536), (1536, D_FF))

HEAD_PERM = np.concatenate(
    [np.concatenate([np.arange(c * 64, (c + 1) * 64), np.arange((4 + c) * 64, (5 + c) * 64)]) for c in range(4)])


def _sigmoid(x):
    return 1.0 / (1.0 + jnp.exp(-x))


def _split3(a):
    hi = a.astype(BF16)
    r1 = a - hi.astype(F32)
    mid = r1.astype(BF16)
    lo = (r1 - mid.astype(F32)).astype(BF16)
    return hi, mid, lo


def _mod_kernel(c_ref, w_ref, b_ref, o_ref):
    c = c_ref[...]
    s = c * _sigmoid(c)
    s_hi, s_mid, _ = _split3(s)
    w = w_ref[...]
    w_hi = w.astype(BF16)
    w_mid = (w - w_hi.astype(F32)).astype(BF16)
    dot = functools.partial(jnp.dot, preferred_element_type=F32)
    acc = dot(s_mid, w_hi) + dot(s_hi, w_mid)
    acc = acc + dot(s_hi, w_hi)
    o_ref[...] = acc + b_ref[...]


def _modulation(c_all, w_ada, b_ada):
    depth = w_ada.shape[0]
    tn = 1024
    return pl.pallas_call(
        _mod_kernel,
        out_shape=jax.ShapeDtypeStruct((depth, MOD_ROWS, 6 * D_MODEL), F32),
        grid=(depth, 6 * D_MODEL // tn),
        in_specs=[
            pl.BlockSpec((MOD_ROWS, D_MODEL), lambda l, j: (0, 0)),
            pl.BlockSpec((None, D_MODEL, tn), lambda l, j: (l, 0, j)),
            pl.BlockSpec((None, 1, tn), lambda l, j: (l, 0, j)),
        ],
        out_specs=pl.BlockSpec((None, MOD_ROWS, tn), lambda l, j: (l, 0, j)),
        name="adaln_mod",
    )(c_all, w_ada, b_ada.reshape(depth, 1, 6 * D_MODEL))


def _segment_ms(x, lo_mask):
    x2 = x * x
    s_lo = jnp.sum(jnp.where(lo_mask, x2, 0.0), axis=-1, keepdims=True)
    s_hi = jnp.sum(jnp.where(lo_mask, 0.0, x2), axis=-1, keepdims=True)
    return jnp.where(lo_mask, s_lo, s_hi) * (1.0 / HEAD_DIM)


def _inproj_kernel(*refs, ctx_tiles, split):
    if split:
        ctx_ref, x_ref = refs[0:2]
        refs = refs[2:]
        from_ctx = pl.program_id(1) < ctx_tiles
        load = lambda bb: jnp.where(from_ctx, ctx_ref[bb], x_ref[bb])
    else:
        x_ref = refs[0]
        refs = refs[1:]
        load = lambda bb: x_ref[bb]
    (mod_ref, nm_ref, w_ref, wt_ref, bg_ref, qn_ref, kn_ref, cos_ref, sa_ref, sb_ref,
     q_ref, k_ref, v_ref, mk_ref, mo_ref, pp_ref, mqt_ref, mvt_ref, g_ref) = refs
    nb, rows = q_ref.shape[0], q_ref.shape[1]

    hs = []
    for bb in range(nb):
        x = load(bb)
        ms = jnp.mean(x * x, axis=-1, keepdims=True)
        gain = nm_ref[...] * (1.0 + mod_ref[bb, 1])
        hs.append((x * lax.rsqrt(ms + EPS) * gain + mod_ref[bb, 0]).astype(BF16))
    accs = [jnp.dot(h, w_ref[...], preferred_element_type=F32) for h in hs]
    accs_t = [lax.dot_general(wt_ref[...], h, (((1,), (1,)), ((), ())), preferred_element_type=F32)
              for h in hs]

    lo_mask = lax.broadcasted_iota(jnp.int32, (rows, LANES), 1) < HEAD_DIM
    cos, sa, sb = cos_ref[...], sa_ref[...], sb_ref[...]

    def norm_rope(xc, gain):
        yc = xc * lax.rsqrt(_segment_ms(xc, lo_mask) + EPS) * gain
        return yc * cos + pltpu.roll(yc, LANES - 16, 1) * sa + pltpu.roll(yc, 16, 1) * sb

    for bb in range(nb):
        acc, acc_t = accs[bb], accs_t[bb]
        blk = lambda off, width: acc[:, off:off + width]
        for c in range(ATT_WIDTH // LANES):
            qc = norm_rope(blk(OFF_Q + c * LANES, LANES), qn_ref[...])
            q_ref[bb, :, c * LANES:(c + 1) * LANES] = (qc * Q_SCALE).astype(BF16)
        k_ref[bb] = norm_rope(blk(OFF_K, LANES), kn_ref[...]).astype(BF16)
        v_ref[bb] = blk(OFF_V, KV_WIDTH).astype(BF16)
        mk_ref[bb] = blk(OFF_MK, ML_WIDTH).astype(BF16)
        mo_ref[bb] = blk(OFF_MO, ML_WIDTH).astype(BF16)
        pp_ref[bb] = blk(OFF_PP, POOL_WIDTH).astype(BF16)
        mqt_ref[bb] = acc_t[ROW_MQ:ROW_MQ + ML_WIDTH, :].astype(BF16)
        mvt_ref[bb] = acc_t[ROW_MV:ROW_MV + ML_WIDTH, :].astype(BF16)
        g_ref[bb] = acc_t[ROW_G:ROW_G + N_GATES, :] + bg_ref[...]


def _mod_spec(layer, nb, ctx_tiles, ctx_block, first=0):
    def idx(b, i):
        return (layer, jnp.where(i + first < ctx_tiles, ctx_block, b), 0, 0, 0)
    return pl.BlockSpec((None, nb, 6, 1, D_MODEL), idx)


def _token_specs(x_in, nb, ctx_tiles, first=0):
    if isinstance(x_in, tuple):
        ctx, x = x_in
        specs = [pl.BlockSpec((nb, ROW_TILE, D_MODEL), lambda b, i: (b, jnp.minimum(i, ctx_tiles - 1), 0)),
                 pl.BlockSpec((nb, ROW_TILE, D_MODEL), lambda b, i: (b, jnp.maximum(i - ctx_tiles, 0), 0))]
        return specs, [ctx, x], ctx.shape[0], ctx.shape[1] + x.shape[1]
    specs = [pl.BlockSpec((nb, ROW_TILE, D_MODEL), lambda b, i: (b, i + first, 0))]
    return specs, [x_in], x_in.shape[0], x_in.shape[1]


def _in_proj(x_in, mod5, layer, norm_mix, w_in_r, w_in_t, b_gates_b, qn, kn, cos_t, sa_t, sb_t, ctx_tiles):
    nb = INPROJ_ROWS
    x_specs, x_ops, B, TA = _token_specs(x_in, nb, ctx_tiles)
    nt = TA // ROW_TILE
    row = lambda w: pl.BlockSpec((nb, ROW_TILE, w), lambda b, i: (b, i, 0))
    col = lambda r: pl.BlockSpec((nb, r, ROW_TILE), lambda b, i: (b, 0, i))
    vec = lambda w: pl.BlockSpec((None, 1, w), lambda b, i: (layer, 0, 0))
    full = lambda r, c: pl.BlockSpec((None, r, c), lambda b, i: (layer, 0, 0))
    tab = pl.BlockSpec((ROW_TILE, LANES), lambda b, i: (i, 0))
    row_widths = (ATT_WIDTH, KV_WIDTH, KV_WIDTH, ML_WIDTH, ML_WIDTH, POOL_WIDTH)
    col_heights = (ML_WIDTH, ML_WIDTH)
    out_shapes = [jax.ShapeDtypeStruct((B, TA, w), BF16) for w in row_widths]
    out_shapes += [jax.ShapeDtypeStruct((B, r, TA), BF16) for r in col_heights]
    out_shapes.append(jax.ShapeDtypeStruct((B, N_GATES, TA), F32))
    out_specs = [row(w) for w in row_widths] + [col(r) for r in col_heights] + [col(N_GATES)]
    kern = functools.partial(_inproj_kernel, ctx_tiles=ctx_tiles, split=len(x_ops) == 2)
    return pl.pallas_call(
        kern,
        out_shape=out_shapes,
        grid=(B // nb, nt),
        in_specs=x_specs + [_mod_spec(layer, nb, ctx_tiles, B // nb), vec(D_MODEL), full(D_MODEL, IN_COLS),
                            full(T_ROWS, D_MODEL), full(N_GATES, ROW_TILE), vec(LANES), vec(LANES), tab, tab, tab],
        out_specs=out_specs,
        compiler_params=pltpu.CompilerParams(vmem_limit_bytes=VMEM_LIMIT),
        name="in_proj",
    )(*x_ops, mod5, norm_mix, w_in_r, w_in_t, b_gates_b, qn, kn, cos_t, sa_t, sb_t)


def _attn_kernel(q_ref, k_ref, v_ref, o_ref, k0_ref, k1_ref, va_ref, *, ctx_len, first_tile):
    i = pl.program_id(1)
    total = k_ref.shape[0]

    @pl.when(i == 0)
    def _():
        k = k_ref[...]
        lo = lax.broadcasted_iota(jnp.int32, k.shape, 1) < HEAD_DIM
        zero = jnp.zeros_like(k)
        k0_ref[...] = jnp.where(lo, k, zero)
        k1_ref[...] = jnp.where(lo, zero, k)
        va_ref[:, 0:LANES] = v_ref[...]
        va_ref[:, LANES:2 * LANES] = jnp.ones((total, LANES), BF16)

    rows = q_ref.shape[0]
    lo_mask = lax.broadcasted_iota(jnp.int32, (rows, LANES), 1) < HEAD_DIM

    def block(nk):
        chunks = list(range(ATT_WIDTH // LANES))
        for g in range(0, len(chunks), ATT_GROUP):
            cs = chunks[g:g + ATT_GROUP]
            ss = [lax.dot_general(q_ref[:, c * LANES:(c + 1) * LANES], kr[0:nk, :], (((1,), (1,)), ((), ())),
                                  preferred_element_type=F32) for c in cs for kr in (k0_ref, k1_ref)]
            ms = [jnp.max(s, axis=-1, keepdims=True) for s in ss]
            ps = [jnp.exp2(s - m).astype(BF16) for s, m in zip(ss, ms)]
            acs = [jnp.dot(p, va_ref[0:nk, :], preferred_element_type=F32) for p in ps]
            outs = [a[:, 0:LANES] / a[:, LANES:2 * LANES] for a in acs]
            for n, c in enumerate(cs):
                o_ref[:, c * LANES:(c + 1) * LANES] = jnp.where(lo_mask, outs[2 * n], outs[2 * n + 1]).astype(BF16)

    if first_tile == 0:
        pl.when(i == 0)(lambda: block(ctx_len))
        pl.when(i > 0)(lambda: block(total))
    else:
        block(total)


def _attention(q, k, v, ctx_len, with_ctx):
    B, TA, _ = q.shape
    first = 0 if with_ctx else ctx_len // ROW_TILE
    nt = TA // ROW_TILE - first
    kern = functools.partial(_attn_kernel, ctx_len=ctx_len, first_tile=first)
    return pl.pallas_call(
        kern,
        out_shape=jax.ShapeDtypeStruct((B, nt * ROW_TILE, ATT_WIDTH), BF16),
        grid=(B, nt),
        in_specs=[pl.BlockSpec((None, ROW_TILE, ATT_WIDTH), lambda b, i: (b, i + first, 0)),
                  pl.BlockSpec((None, TA, KV_WIDTH), lambda b, i: (b, 0, 0)),
                  pl.BlockSpec((None, TA, KV_WIDTH), lambda b, i: (b, 0, 0))],
        out_specs=pl.BlockSpec((None, ROW_TILE, ATT_WIDTH), lambda b, i: (b, i, 0)),
        scratch_shapes=[pltpu.VMEM((TA, KV_WIDTH), BF16), pltpu.VMEM((TA, KV_WIDTH), BF16),
                        pltpu.VMEM((TA, 2 * LANES), BF16)],
        compiler_params=pltpu.CompilerParams(vmem_limit_bytes=VMEM_LIMIT),
        name="attention",
    )(q, k, v)


def _mlstm_kernel(qf_ref, kf_ref, vf_ref, gf_ref, qb_ref, kb_ref, vb_ref, gb_ref, tri_ref, hf_ref, hb_ref,
                  st_ref, m_ref, r_ref):
    i = pl.program_id(1)

    @pl.when(i == 0)
    def _():
        st_ref[...] = jnp.zeros_like(st_ref)
        m_ref[...] = jnp.zeros_like(m_ref)
        r_ref[...] = jnp.zeros_like(r_ref)

    L = CHUNK
    s_idx = lax.broadcasted_iota(jnp.int32, (L, L), 0)
    t_idx = lax.broadcasted_iota(jnp.int32, (L, L), 1)
    first_head_rows = lax.broadcasted_iota(jnp.int32, (LANES, L), 0) < HEAD_DIM
    ones_blk = jnp.where(lax.broadcasted_iota(jnp.int32, (VA_ROWS - HEAD_DIM, L), 0) == 0, 1.0, 0.0).astype(BF16)

    dirs = ((qf_ref, kf_ref, vf_ref, gf_ref, hf_ref), (qb_ref, kb_ref, vb_ref, gb_ref, hb_ref))
    chains = [divmod(bd, 2) for bd in range(2 * hf_ref.shape[0])]
    units = [(bd, hd) for bd in range(len(chains)) for hd in range(ML_HEADS)]
    refs = [tuple(ref.at[bb] for ref in dirs[d]) for bb, d in chains]

    qm, kp, va, st, s_raw, a1 = {}, {}, {}, {}, {}, {}
    for bd, hd in units:
        q_r, k_r, v_r = refs[bd][0:3]
        pair, second = divmod(hd, 2)
        kp[bd, hd] = k_r[:, pair * LANES:(pair + 1) * LANES]
        q_pair = q_r[pair * LANES:(pair + 1) * LANES, :]
        keep = jnp.logical_not(first_head_rows) if second else first_head_rows
        qm[bd, hd] = jnp.where(keep, q_pair, jnp.zeros_like(q_pair))
        va[bd, hd] = jnp.concatenate([v_r[hd * HEAD_DIM:(hd + 1) * HEAD_DIM, :], ones_blk], axis=0)
        st[bd, hd] = st_ref[bd * ML_HEADS + hd]
        s_raw[bd, hd] = jnp.dot(kp[bd, hd], qm[bd, hd], preferred_element_type=F32)
        a1[bd, hd] = jnp.dot(st[bd, hd].astype(BF16), qm[bd, hd], preferred_element_type=F32)

    g8 = [refs[bd][3][8 * d:8 * d + 8, :] for bd, (bb, d) in enumerate(chains)]
    lf8 = [jnp.minimum(g, 0.0) - jnp.log1p(jnp.exp(-jnp.abs(g))) for g in g8]
    parts = [jnp.concatenate([p.astype(F32) for p in _split3(lf)], axis=0).astype(BF16) for lf in lf8]
    cum = [jnp.dot(p, tri_ref[d], preferred_element_type=F32) for p, (bb, d) in zip(parts, chains)]
    cum = [c[16:24] + c[8:16] + c[0:8] for c in cum]
    b8 = [c[:, 0:L] for c in cum]
    btot8 = [c[:, L:2 * L] for c in cum]
    li8 = [pltpu.roll(g, ML_HEADS, 0) for g in g8]
    for bd in range(len(chains)):
        r_ref[bd, 0:8, :] = li8[bd] - b8[bd]
    ct = [r_ref[bd].T for bd in range(len(chains))]
    m_prev8 = [m_ref[bd] for bd in range(len(chains))]
    gg8 = [bt - b + li for bt, b, li in zip(btot8, b8, li8)]
    m_new8 = [jnp.maximum(bt + mp, jnp.max(gg, axis=-1, keepdims=True)) for bt, mp, gg in zip(btot8, m_prev8, gg8)]
    a8 = [jnp.exp(bt + mp - mn) for bt, mp, mn in zip(btot8, m_prev8, m_new8)]
    wk8 = [jnp.exp(gg - mn) for gg, mn in zip(gg8, m_new8)]
    inter8 = [b + mp for b, mp in zip(b8, m_prev8)]
    for bd in range(len(chains)):
        m_ref[bd] = m_new8[bd]

    dmat, m_t, s_t, a2, upd = {}, {}, {}, {}, {}
    for bd, hd in units:
        r = ML_HEADS + hd
        causal = (s_idx <= t_idx) if chains[bd][1] == 0 else (s_idx >= t_idx)
        dmat[bd, hd] = jnp.where(causal, ct[bd][:, r:r + 1] + b8[bd][r:r + 1], -jnp.inf)
        m_t[bd, hd] = jnp.maximum(inter8[bd][r:r + 1], jnp.max(dmat[bd, hd], axis=0, keepdims=True))
    for bd, hd in units:
        s_t[bd, hd] = (s_raw[bd, hd] * jnp.exp(dmat[bd, hd] - m_t[bd, hd])).astype(BF16)
    for bd, hd in units:
        r = ML_HEADS + hd
        a2[bd, hd] = jnp.dot(va[bd, hd], s_t[bd, hd], preferred_element_type=F32)
        vw = (va[bd, hd].astype(F32) * wk8[bd][r:r + 1]).astype(BF16)
        upd[bd, hd] = jnp.dot(vw, kp[bd, hd], preferred_element_type=F32)
    hts = {}
    for bd, hd in units:
        r = ML_HEADS + hd
        inter = inter8[bd][r:r + 1]
        nd = jnp.exp(inter - m_t[bd, hd]) * a1[bd, hd] + a2[bd, hd]
        den = nd[HEAD_DIM:HEAD_DIM + 1, :]
        hts[bd, hd] = nd[0:HEAD_DIM, :] / jnp.maximum(jnp.abs(den), jnp.exp(-m_t[bd, hd]))
        st_ref[bd * ML_HEADS + hd] = a8[bd][r:r + 1] * st[bd, hd] + upd[bd, hd]
    for bd in range(len(chains)):
        refs[bd][4][...] = jnp.concatenate([hts[bd, hd] for hd in range(ML_HEADS)], axis=0).T.astype(BF16)


def _mlstm(mqt, mk, mvt, gates, tri, ctx_len):
    B, TA, _ = mk.shape
    nc = TA // CHUNK
    ncc = ctx_len // CHUNK

    def fwd(i):
        return i

    def bwd(i):
        return jnp.where(i < ncc, ncc - 1 - i, nc - 1 - (i - ncc))

    nb = MLSTM_ROWS if B % MLSTM_ROWS == 0 else 1
    tok = lambda f: pl.BlockSpec((nb, CHUNK, ML_WIDTH), lambda b, i: (b, f(i), 0))
    feat = lambda f, r: pl.BlockSpec((nb, r, CHUNK), lambda b, i: (b, 0, f(i)))
    n_units = 2 * nb * ML_HEADS
    return pl.pallas_call(
        _mlstm_kernel,
        out_shape=[jax.ShapeDtypeStruct((B, TA, ML_WIDTH), BF16)] * 2,
        grid=(B // nb, nc),
        in_specs=[feat(fwd, ML_WIDTH), tok(fwd), feat(fwd, ML_WIDTH), feat(fwd, N_GATES),
                  feat(bwd, ML_WIDTH), tok(bwd), feat(bwd, ML_WIDTH), feat(bwd, N_GATES),
                  pl.BlockSpec((2, CHUNK, 2 * CHUNK), lambda b, i: (0, 0, 0))],
        out_specs=[tok(fwd), tok(bwd)],
        scratch_shapes=[pltpu.VMEM((n_units, VA_ROWS, LANES), F32), pltpu.VMEM((2 * nb, 8, LANES), F32),
                        pltpu.VMEM((2 * nb, CHUNK, LANES), F32)],
        compiler_params=pltpu.CompilerParams(vmem_limit_bytes=VMEM_LIMIT),
        name="mlstm",
    )(mqt, mk, mvt, gates, mqt, mk, mvt, gates, tri)


def _pool_kernel(pp_ref, ic_ref, bd_ref, ps_ref, o_ref, p_ref, s2_ref, s4_ref, s8_ref, *, ctx_len):
    total = pp_ref.shape[0]
    seqs = ((0, ctx_len, POOL_PAD), (ctx_len, total - ctx_len, 2 * POOL_PAD + ctx_len))
    rows = p_ref.shape[0]
    upper = slice(LANES, 2 * LANES)
    for ref in (p_ref, s2_ref, s4_ref, s8_ref):
        ref[...] = jnp.zeros_like(ref)
    for src, n, dst in seqs:
        p_ref[dst:dst + n, :] = pp_ref[src:src + n, :].astype(F32)
    lo, m = 8, rows - 16
    s2_ref[lo:lo + m, :] = p_ref[lo - 1:lo - 1 + m, :] + p_ref[lo:lo + m, :]
    s4_ref[lo:lo + m, :] = s2_ref[lo - 1:lo - 1 + m, :] + s2_ref[lo + 1:lo + 1 + m, :]
    s8_ref[lo:lo + m, :] = s4_ref[lo - 2:lo - 2 + m, upper] + s4_ref[lo + 2:lo + 2 + m, upper]
    for src, n, dst in seqs:
        first_group = lax.broadcasted_iota(jnp.int32, (n, LANES), 1) < POOL_GROUP_DIM
        s16 = s8_ref[dst - 4:dst - 4 + n, :] + s8_ref[dst + 4:dst + 4 + n, :]
        wsum = jnp.concatenate([jnp.where(first_group, s2_ref[dst:dst + n, 0:LANES], s4_ref[dst:dst + n, 0:LANES]),
                                jnp.where(first_group, s8_ref[dst:dst + n, :], s16)], axis=-1)
        p = wsum * ic_ref[src:src + n, :] - p_ref[dst:dst + n, :]
        y = jnp.dot(p.astype(BF16), bd_ref[...], preferred_element_type=F32) * ps_ref[...]
        o_ref[src:src + n, :] = y.astype(BF16)


def _inverse_window_counts(ctx_len, t_latent):
    half = np.repeat(np.array(POOL_WINDOWS) // 2, POOL_GROUP_DIM)[None, :]
    tables = []
    for n in (ctx_len, t_latent):
        t = np.arange(n)[:, None]
        tables.append(1.0 / (np.minimum(t + half, n) - np.maximum(t - half, 0)))
    return jnp.asarray(np.concatenate(tables, axis=0), dtype=F32)


def _pool(pp, inv_cnt, pool_bd, pool_scale, layer, ctx_len):
    B, TA, _ = pp.shape
    rows = TA + 3 * POOL_PAD
    kern = functools.partial(_pool_kernel, ctx_len=ctx_len)
    full = pl.BlockSpec((None, TA, POOL_WIDTH), lambda b: (b, 0, 0))
    return pl.pallas_call(
        kern,
        out_shape=jax.ShapeDtypeStruct((B, TA, POOL_WIDTH), BF16),
        grid=(B,),
        in_specs=[full, pl.BlockSpec((TA, POOL_WIDTH), lambda b: (0, 0)),
                  pl.BlockSpec((None, POOL_WIDTH, POOL_WIDTH), lambda b: (layer, 0, 0)),
                  pl.BlockSpec((None, 1, POOL_WIDTH), lambda b: (layer, 0, 0))],
        out_specs=full,
        scratch_shapes=[pltpu.VMEM((rows, POOL_WIDTH), F32)] * 3 + [pltpu.VMEM((rows, LANES), F32)],
        compiler_params=pltpu.CompilerParams(vmem_limit_bytes=VMEM_LIMIT),
        name="pool_mix",
    )(pp, inv_cnt, pool_bd, pool_scale)


def _mixffn_kernel(*refs, ctx_tiles, split, final):
    if split:
        ctx_ref, x_ref = refs[0:2]
        refs = refs[2:]
        from_ctx = pl.program_id(1) < ctx_tiles
        load = lambda bb: jnp.where(from_ctx, ctx_ref[bb], x_ref[bb])
    else:
        x_ref = refs[0]
        refs = refs[1:]
        load = lambda bb: x_ref[bb]
    (att_ref, hf_ref, hb_ref, mo_ref, pool_ref, mod_ref, mln_ref, wo_ref, nf_ref,
     wg_ref, wu_ref, wd_ref, fn_ref, o_ref) = refs
    nb, rows = o_ref.shape[0], o_ref.shape[1]
    lo_mask = lax.broadcasted_iota(jnp.int32, (rows, LANES), 1) < HEAD_DIM

    cats = []
    for bb in range(nb):
        ml = []
        for c in range(ML_WIDTH // LANES):
            sl = slice(c * LANES, (c + 1) * LANES)
            hs = hf_ref[bb, :, sl].astype(F32) + hb_ref[bb, :, sl].astype(F32)
            hn = hs * lax.rsqrt(_segment_ms(hs, lo_mask) + EPS) * mln_ref[:, sl]
            ml.append((hn * _sigmoid(mo_ref[bb, :, sl].astype(F32))).astype(BF16))
        cats.append(jnp.concatenate([att_ref[bb]] + ml + [pool_ref[bb]], axis=-1))
    mixes = [jnp.dot(cat, wo_ref[...], preferred_element_type=F32) for cat in cats]

    x1s, h2s = [], []
    for bb in range(nb):
        x1 = load(bb) + mod_ref[bb, 2] * mixes[bb]
        ms = jnp.mean(x1 * x1, axis=-1, keepdims=True)
        gain = nf_ref[...] * (1.0 + mod_ref[bb, 4])
        h2s.append((x1 * lax.rsqrt(ms + EPS) * gain + mod_ref[bb, 3]).astype(BF16))
        x1s.append(x1)
    downs = []
    for h2 in h2s:
        down = None
        for lo, hi in FF_SPLITS:
            gate = jnp.dot(h2, wg_ref[:, lo:hi], preferred_element_type=F32)
            up = jnp.dot(h2, wu_ref[:, lo:hi], preferred_element_type=F32)
            act = (gate * _sigmoid(gate) * up).astype(BF16)
            part = jnp.dot(act, wd_ref[lo:hi, :], preferred_element_type=F32)
            down = part if down is None else down + part
        downs.append(down)
    for bb in range(nb):
        out = x1s[bb] + mod_ref[bb, 5] * downs[bb]
        if final:
            ms = jnp.mean(out * out, axis=-1, keepdims=True)
            out = out * lax.rsqrt(ms + EPS) * fn_ref[...]
        o_ref[bb] = out


def _mix_ffn(x_in, att, hf, hb, mo, pool, mod5, layer, ml_norm, w_out_r, norm_ffn, wg, wu, wd, final_norm,
             ctx_tiles, final):
    nb = STEP_ROWS
    first = ctx_tiles if final else 0
    x_specs, x_ops, B, TA = _token_specs(x_in, nb, ctx_tiles, first)
    nt = TA // ROW_TILE - first
    row = lambda w: pl.BlockSpec((nb, ROW_TILE, w), lambda b, i: (b, i + first, 0))
    vec = lambda w: pl.BlockSpec((None, 1, w), lambda b, i: (layer, 0, 0))
    wspec = lambda r, c: pl.BlockSpec((None, r, c), lambda b, i: (layer, 0, 0), pipeline_mode=pl.Buffered(1))

    kern = functools.partial(_mixffn_kernel, ctx_tiles=ctx_tiles, split=len(x_ops) == 2, final=final)
    return pl.pallas_call(
        kern,
        out_shape=jax.ShapeDtypeStruct((B, nt * ROW_TILE, D_MODEL), F32),
        grid=(B // nb, nt),
        in_specs=x_specs + [pl.BlockSpec((nb, ROW_TILE, ATT_WIDTH), lambda b, i: (b, i, 0)),
                            row(ML_WIDTH), row(ML_WIDTH), row(ML_WIDTH), row(POOL_WIDTH),
                            _mod_spec(layer, nb, ctx_tiles, B // nb, first), vec(ML_WIDTH),
                            wspec(D_MODEL, D_MODEL), vec(D_MODEL), wspec(D_MODEL, D_FF), wspec(D_MODEL, D_FF),
                            wspec(D_FF, D_MODEL), pl.BlockSpec((1, D_MODEL), lambda b, i: (0, 0))],
        out_specs=pl.BlockSpec((nb, ROW_TILE, D_MODEL), lambda b, i: (b, i, 0)),
        compiler_params=pltpu.CompilerParams(vmem_limit_bytes=VMEM_LIMIT),
        name="mix_ffn",
    )(*x_ops, att, hf, hb, mo, pool, mod5, ml_norm, w_out_r, norm_ffn, wg, wu, wd, final_norm)


def _rope_tables(t_latent, ctx_len):
    f32 = np.float32
    half = HEAD_DIM // 2
    inv_freq = (f32(1.0) / np.power(f32(ROPE_THETA), np.arange(0, half, 2, dtype=f32) / f32(half))).astype(f32)
    rows = t_latent // GRID_W
    row = np.repeat(np.arange(rows, dtype=f32), GRID_W)
    col = np.tile(np.arange(GRID_W, dtype=f32), rows)
    a_row = row[:, None] * inv_freq
    a_col = col[:, None] * inv_freq
    ang = np.concatenate([a_row, a_row, a_col, a_col], -1).astype(f32)
    cos, sin = np.cos(ang), np.sin(ang)
    first_half = (np.arange(HEAD_DIM) % 32) < 16
    sa = np.where(first_half, -sin, 0.0)
    sb = np.where(first_half, 0.0, sin)
    pad = lambda a, fill: jnp.asarray(np.tile(np.concatenate([np.full((ctx_len, HEAD_DIM), fill, f32), a], 0), (1, 2)),
                                      dtype=F32)
    return pad(cos, 1.0), pad(sa, 0.0), pad(sb, 0.0)


_IN = np.arange(2064)
COLS_TOKEN = np.concatenate([HEAD_PERM, _IN[512:768], _IN[1024:1280], _IN[1536:1792], _IN[1808:2064]])
COLS_FEATURE = np.concatenate([_IN[768:1024], _IN[1280:1536], _IN[1792:1808]])
COL_SCALE = np.where((np.arange(IN_COLS) >= OFF_MK) & (np.arange(IN_COLS) < OFF_MO), HEAD_DIM ** -0.5, 1.0).astype(np.float32)
ROWS_OUT = np.concatenate([HEAD_PERM, np.arange(ATT_WIDTH, D_MODEL)])


def _take_runs(a, index, axis):
    cuts = [0] + [n for n in range(1, len(index)) if index[n] != index[n - 1] + 1] + [len(index)]
    runs = [lax.slice_in_dim(a, int(index[s]), int(index[e - 1]) + 1, axis=axis) for s, e in zip(cuts[:-1], cuts[1:])]
    return jnp.concatenate(runs, axis=axis)


def _prepare_params(w_in, b_gates, q_norm, k_norm, pool_w, w_out, w_ffn_gate, w_ffn_up, w_ffn_down):
    depth = w_in.shape[0]
    groups = len(POOL_WINDOWS)
    eye = jnp.eye(groups, dtype=F32)
    return dict(
        w_rows=(_take_runs(w_in, COLS_TOKEN, 2) * COL_SCALE).astype(BF16),
        w_feat=jnp.swapaxes(_take_runs(w_in, COLS_FEATURE, 2), 1, 2).astype(BF16),
        b_gates=jnp.broadcast_to(b_gates[:, :, None], (depth, N_GATES, ROW_TILE)),
        qn=jnp.tile(q_norm, (1, 2))[:, None, :], kn=jnp.tile(k_norm, (1, 2))[:, None, :],
        pool_bd=(pool_w[:, :, :, None, :] * eye[None, :, None, :, None]).reshape(depth, POOL_WIDTH, POOL_WIDTH).astype(BF16),
        w_out=_take_runs(w_out, ROWS_OUT, 1).astype(BF16),
        wg=w_ffn_gate.astype(BF16), wu=w_ffn_up.astype(BF16), wd=w_ffn_down.astype(BF16))


def _cumsum_matrices():
    s = np.arange(CHUNK)[:, None]
    t = np.arange(CHUNK)[None, :]
    ones = np.ones((CHUNK, CHUNK), np.float32)
    prefix = np.concatenate([(s <= t).astype(np.float32), ones], axis=1)
    suffix = np.concatenate([(s >= t).astype(np.float32), ones], axis=1)
    return jnp.asarray(np.stack([prefix, suffix]), dtype=BF16)


def kernel(x, c, ctx, c_ctx, w_ada, b_ada, norm_mix, w_in, b_gates, q_norm, k_norm, ml_norm, pool_w, pool_scale,
           w_out, norm_ffn, w_ffn_gate, w_ffn_up, w_ffn_down, final_norm):
    B, T, _ = x.shape
    ctx_len = ctx.shape[1]
    depth = w_ada.shape[0]
    n_rep = max(STEP_ROWS, INPROJ_ROWS)
    assert ctx_len % ROW_TILE == 0 and T % ROW_TILE == 0 and B % n_rep == 0 and B + n_rep <= MOD_ROWS
    ctx_tiles = ctx_len // ROW_TILE

    c_rep = jnp.broadcast_to(c_ctx[None, :], (n_rep, D_MODEL))
    c_all = jnp.concatenate([c, c_rep, jnp.zeros((MOD_ROWS - B - n_rep, D_MODEL), F32)], 0)
    mod = _modulation(c_all, w_ada, b_ada)
    mod5 = mod.reshape(depth, MOD_ROWS, 6, 1, D_MODEL)
    cos_t, sa_t, sb_t = _rope_tables(T, ctx_len)
    tri = _cumsum_matrices()
    inv_cnt = _inverse_window_counts(ctx_len, T)
    p = _prepare_params(w_in, b_gates, q_norm, k_norm, pool_w, w_out, w_ffn_gate, w_ffn_up, w_ffn_down)
    stack = lambda a: a[:, None, :]

    x_all = (ctx, x)
    for l in range(depth):
        final = l == depth - 1
        q, k, v, mk, mo, pp, mqt, mvt, gates = _in_proj(
            x_all, mod5, l, stack(norm_mix), p["w_rows"], p["w_feat"], p["b_gates"], p["qn"], p["kn"],
            cos_t, sa_t, sb_t, ctx_tiles)
        att = _attention(q, k, v, ctx_len, with_ctx=not final)
        hf, hb = _mlstm(mqt, mk, mvt, gates, tri, ctx_len)
        pool = _pool(pp, inv_cnt, p["pool_bd"], stack(pool_scale), l, ctx_len)
        x_all = _mix_ffn(x_all, att, hf, hb, mo, pool, mod5, l, stack(ml_norm), p["w_out"], stack(norm_ffn),
                         p["wg"], p["wu"], p["wd"], final_norm.reshape(1, -1), ctx_tiles, final)
    return x_all
```

```python
import functools

import jax
import jax.numpy as jnp
import numpy as np
from jax import lax
from jax.experimental import pallas as pl
from jax.experimental.pallas import tpu as pltpu

F32 = jnp.float32
BF16 = jnp.bfloat16

D_MODEL = 1024
HEAD_DIM = 64
GRID_W = 64
ATT_WIDTH = 512
KV_WIDTH = 128
ML_WIDTH = 256
POOL_WIDTH = 256
ATT_HEADS = 8
ML_HEADS = 4
N_GATES = 16
POOL_WINDOWS = (2, 4, 8, 16)
POOL_GROUP_DIM = 64
D_FF = 2816
CHUNK = 128
ROPE_THETA = 10000.0
EPS = 1e-6

LANES = 128
ROW_TILE = 256
MOD_ROWS = 24
POOL_PAD = 16
VMEM_LIMIT = 52 * 1024 * 1024

OFF_Q, OFF_K, OFF_V, OFF_MK, OFF_MO, OFF_PP = 0, 512, 640, 768, 1024, 1280
IN_COLS = OFF_PP + POOL_WIDTH
ROW_MQ, ROW_MV, ROW_G = 0, 256, 512
T_ROWS = ROW_G + N_GATES
Q_SCALE = HEAD_DIM ** -0.5 * float(np.log2(np.e))
VA_ROWS = 80
ATT_GROUP = 2
MLSTM_ROWS = 16
STEP_ROWS = 2
INPROJ_ROWS = 4
FF_SPLITS = ((0, 1536), (1536, D_FF))

HEAD_PERM = np.concatenate(
    [np.concatenate([np.arange(c * 64, (c + 1) * 64), np.arange((4 + c) * 64, (5 + c) * 64)]) for c in range(4)])


def _sigmoid(x):
    return 1.0 / (1.0 + jnp.exp(-x))


def _split3(a):
    hi = a.astype(BF16)
    r1 = a - hi.astype(F32)
    mid = r1.astype(BF16)
    lo = (r1 - mid.astype(F32)).astype(BF16)
    return hi, mid, lo


def _mod_kernel(c_ref, w_ref, b_ref, o_ref):
    c = c_ref[...]
    s = c * _sigmoid(c)
    s_hi, s_mid, _ = _split3(s)
    w = w_ref[...]
    w_hi = w.astype(BF16)
    w_mid = (w - w_hi.astype(F32)).astype(BF16)
    dot = functools.partial(jnp.dot, preferred_element_type=F32)
    acc = dot(s_mid, w_hi) + dot(s_hi, w_mid)
    acc = acc + dot(s_hi, w_hi)
    o_ref[...] = acc + b_ref[...]


def _modulation(c_all, w_ada, b_ada):
    depth = w_ada.shape[0]
    tn = 1024
    return pl.pallas_call(
        _mod_kernel,
        out_shape=jax.ShapeDtypeStruct((depth, MOD_ROWS, 6 * D_MODEL), F32),
        grid=(depth, 6 * D_MODEL // tn),
        in_specs=[
            pl.BlockSpec((MOD_ROWS, D_MODEL), lambda l, j: (0, 0)),
            pl.BlockSpec((None, D_MODEL, tn), lambda l, j: (l, 0, j)),
            pl.BlockSpec((None, 1, tn), lambda l, j: (l, 0, j)),
        ],
        out_specs=pl.BlockSpec((None, MOD_ROWS, tn), lambda l, j: (l, 0, j)),
        name="adaln_mod",
    )(c_all, w_ada, b_ada.reshape(depth, 1, 6 * D_MODEL))


def _segment_ms(x, lo_mask):
    x2 = x * x
    s_lo = jnp.sum(jnp.where(lo_mask, x2, 0.0), axis=-1, keepdims=True)
    s_hi = jnp.sum(jnp.where(lo_mask, 0.0, x2), axis=-1, keepdims=True)
    return jnp.where(lo_mask, s_lo, s_hi) * (1.0 / HEAD_DIM)


def _inproj_kernel(*refs, ctx_tiles, split):
    if split:
        ctx_ref, x_ref = refs[0:2]
        refs = refs[2:]
        from_ctx = pl.program_id(1) < ctx_tiles
        load = lambda bb: jnp.where(from_ctx, ctx_ref[bb], x_ref[bb])
    else:
        x_ref = refs[0]
        refs = refs[1:]
        load = lambda bb: x_ref[bb]
    (mod_ref, nm_ref, w_ref, wt_ref, bg_ref, qn_ref, kn_ref, cos_ref, sa_ref, sb_ref,
     q_ref, k_ref, v_ref, mk_ref, mo_ref, pp_ref, mqt_ref, mvt_ref, g_ref) = refs
    nb, rows = q_ref.shape[0], q_ref.shape[1]

    hs = []
    for bb in range(nb):
        x = load(bb)
        ms = jnp.mean(x * x, axis=-1, keepdims=True)
        gain = nm_ref[...] * (1.0 + mod_ref[bb, 1])
        hs.append((x * lax.rsqrt(ms + EPS) * gain + mod_ref[bb, 0]).astype(BF16))
    accs = [jnp.dot(h, w_ref[...], preferred_element_type=F32) for h in hs]
    accs_t = [lax.dot_general(wt_ref[...], h, (((1,), (1,)), ((), ())), preferred_element_type=F32)
              for h in hs]

    lo_mask = lax.broadcasted_iota(jnp.int32, (rows, LANES), 1) < HEAD_DIM
    cos, sa, sb = cos_ref[...], sa_ref[...], sb_ref[...]

    def norm_rope(xc, gain):
        yc = xc * lax.rsqrt(_segment_ms(xc, lo_mask) + EPS) * gain
        return yc * cos + pltpu.roll(yc, LANES - 16, 1) * sa + pltpu.roll(yc, 16, 1) * sb

    for bb in range(nb):
        acc, acc_t = accs[bb], accs_t[bb]
        blk = lambda off, width: acc[:, off:off + width]
        for c in range(ATT_WIDTH // LANES):
            qc = norm_rope(blk(OFF_Q + c * LANES, LANES), qn_ref[...])
            q_ref[bb, :, c * LANES:(c + 1) * LANES] = (qc * Q_SCALE).astype(BF16)
        k_ref[bb] = norm_rope(blk(OFF_K, LANES), kn_ref[...]).astype(BF16)
        v_ref[bb] = blk(OFF_V, KV_WIDTH).astype(BF16)
        mk_ref[bb] = blk(OFF_MK, ML_WIDTH).astype(BF16)
        mo_ref[bb] = blk(OFF_MO, ML_WIDTH).astype(BF16)
        pp_ref[bb] = blk(OFF_PP, POOL_WIDTH).astype(BF16)
        mqt_ref[bb] = acc_t[ROW_MQ:ROW_MQ + ML_WIDTH, :].astype(BF16)
        mvt_ref[bb] = acc_t[ROW_MV:ROW_MV + ML_WIDTH, :].astype(BF16)
        g_ref[bb] = acc_t[ROW_G:ROW_G + N_GATES, :] + bg_ref[...]


def _mod_spec(layer, nb, ctx_tiles, ctx_block, first=0):
    def idx(b, i):
        return (layer, jnp.where(i + first < ctx_tiles, ctx_block, b), 0, 0, 0)
    return pl.BlockSpec((None, nb, 6, 1, D_MODEL), idx)


def _token_specs(x_in, nb, ctx_tiles, first=0):
    if isinstance(x_in, tuple):
        ctx, x = x_in
        specs = [pl.BlockSpec((nb, ROW_TILE, D_MODEL), lambda b, i: (b, jnp.minimum(i, ctx_tiles - 1), 0)),
                 pl.BlockSpec((nb, ROW_TILE, D_MODEL), lambda b, i: (b, jnp.maximum(i - ctx_tiles, 0), 0))]
        return specs, [ctx, x], ctx.shape[0], ctx.shape[1] + x.shape[1]
    specs = [pl.BlockSpec((nb, ROW_TILE, D_MODEL), lambda b, i: (b, i + first, 0))]
    return specs, [x_in], x_in.shape[0], x_in.shape[1]


def _in_proj(x_in, mod5, layer, norm_mix, w_in_r, w_in_t, b_gates_b, qn, kn, cos_t, sa_t, sb_t, ctx_tiles):
    nb = INPROJ_ROWS
    x_specs, x_ops, B, TA = _token_specs(x_in, nb, ctx_tiles)
    nt = TA // ROW_TILE
    row = lambda w: pl.BlockSpec((nb, ROW_TILE, w), lambda b, i: (b, i, 0))
    col = lambda r: pl.BlockSpec((nb, r, ROW_TILE), lambda b, i: (b, 0, i))
    vec = lambda w: pl.BlockSpec((None, 1, w), lambda b, i: (layer, 0, 0))
    full = lambda r, c: pl.BlockSpec((None, r, c), lambda b, i: (layer, 0, 0))
    tab = pl.BlockSpec((ROW_TILE, LANES), lambda b, i: (i, 0))
    row_widths = (ATT_WIDTH, KV_WIDTH, KV_WIDTH, ML_WIDTH, ML_WIDTH, POOL_WIDTH)
    col_heights = (ML_WIDTH, ML_WIDTH)
    out_shapes = [jax.ShapeDtypeStruct((B, TA, w), BF16) for w in row_widths]
    out_shapes += [jax.ShapeDtypeStruct((B, r, TA), BF16) for r in col_heights]
    out_shapes.append(jax.ShapeDtypeStruct((B, N_GATES, TA), F32))
    out_specs = [row(w) for w in row_widths] + [col(r) for r in col_heights] + [col(N_GATES)]
    kern = functools.partial(_inproj_kernel, ctx_tiles=ctx_tiles, split=len(x_ops) == 2)
    return pl.pallas_call(
        kern,
        out_shape=out_shapes,
        grid=(B // nb, nt),
        in_specs=x_specs + [_mod_spec(layer, nb, ctx_tiles, B // nb), vec(D_MODEL), full(D_MODEL, IN_COLS),
                            full(T_ROWS, D_MODEL), full(N_GATES, ROW_TILE), vec(LANES), vec(LANES), tab, tab, tab],
        out_specs=out_specs,
        compiler_params=pltpu.CompilerParams(vmem_limit_bytes=VMEM_LIMIT),
        name="in_proj",
    )(*x_ops, mod5, norm_mix, w_in_r, w_in_t, b_gates_b, qn, kn, cos_t, sa_t, sb_t)


def _attn_kernel(q_ref, k_ref, v_ref, o_ref, k0_ref, k1_ref, va_ref, *, ctx_len, first_tile):
    i = pl.program_id(1)
    total = k_ref.shape[0]

    @pl.when(i == 0)
    def _():
        k = k_ref[...]
        lo = lax.broadcasted_iota(jnp.int32, k.shape, 1) < HEAD_DIM
        zero = jnp.zeros_like(k)
        k0_ref[...] = jnp.where(lo, k, zero)
        k1_ref[...] = jnp.where(lo, zero, k)
        va_ref[:, 0:LANES] = v_ref[...]
        va_ref[:, LANES:2 * LANES] = jnp.ones((total, LANES), BF16)

    rows = q_ref.shape[0]
    lo_mask = lax.broadcasted_iota(jnp.int32, (rows, LANES), 1) < HEAD_DIM

    def block(nk):
        chunks = list(range(ATT_WIDTH // LANES))
        for g in range(0, len(chunks), ATT_GROUP):
            cs = chunks[g:g + ATT_GROUP]
            ss = [lax.dot_general(q_ref[:, c * LANES:(c + 1) * LANES], kr[0:nk, :], (((1,), (1,)), ((), ())),
                                  preferred_element_type=F32) for c in cs for kr in (k0_ref, k1_ref)]
            ms = [jnp.max(s, axis=-1, keepdims=True) for s in ss]
            ps = [jnp.exp2(s - m).astype(BF16) for s, m in zip(ss, ms)]
            acs = [jnp.dot(p, va_ref[0:nk, :], preferred_element_type=F32) for p in ps]
            outs = [a[:, 0:LANES] / a[:, LANES:2 * LANES] for a in acs]
            for n, c in enumerate(cs):
                o_ref[:, c * LANES:(c + 1) * LANES] = jnp.where(lo_mask, outs[2 * n], outs[2 * n + 1]).astype(BF16)

    if first_tile == 0:
        pl.when(i == 0)(lambda: block(ctx_len))
        pl.when(i > 0)(lambda: block(total))
    else:
        block(total)


def _attention(q, k, v, ctx_len, with_ctx):
    B, TA, _ = q.shape
    first = 0 if with_ctx else ctx_len // ROW_TILE
    nt = TA // ROW_TILE - first
    kern = functools.partial(_attn_kernel, ctx_len=ctx_len, first_tile=first)
    return pl.pallas_call(
        kern,
        out_shape=jax.ShapeDtypeStruct((B, nt * ROW_TILE, ATT_WIDTH), BF16),
        grid=(B, nt),
        in_specs=[pl.BlockSpec((None, ROW_TILE, ATT_WIDTH), lambda b, i: (b, i + first, 0)),
                  pl.BlockSpec((None, TA, KV_WIDTH), lambda b, i: (b, 0, 0)),
                  pl.BlockSpec((None, TA, KV_WIDTH), lambda b, i: (b, 0, 0))],
        out_specs=pl.BlockSpec((None, ROW_TILE, ATT_WIDTH), lambda b, i: (b, i, 0)),
        scratch_shapes=[pltpu.VMEM((TA, KV_WIDTH), BF16), pltpu.VMEM((TA, KV_WIDTH), BF16),
                        pltpu.VMEM((TA, 2 * LANES), BF16)],
        compiler_params=pltpu.CompilerParams(vmem_limit_bytes=VMEM_LIMIT),
        name="attention",
    )(q, k, v)


def _mlstm_kernel(qf_ref, kf_ref, vf_ref, gf_ref, qb_ref, kb_ref, vb_ref, gb_ref, tri_ref, hf_ref, hb_ref,
                  st_ref, m_ref, r_ref):
    i = pl.program_id(1)

    @pl.when(i == 0)
    def _():
        st_ref[...] = jnp.zeros_like(st_ref)
        m_ref[...] = jnp.zeros_like(m_ref)
        r_ref[...] = jnp.zeros_like(r_ref)

    L = CHUNK
    s_idx = lax.broadcasted_iota(jnp.int32, (L, L), 0)
    t_idx = lax.broadcasted_iota(jnp.int32, (L, L), 1)
    first_head_rows = lax.broadcasted_iota(jnp.int32, (LANES, L), 0) < HEAD_DIM
    ones_blk = jnp.where(lax.broadcasted_iota(jnp.int32, (VA_ROWS - HEAD_DIM, L), 0) == 0, 1.0, 0.0).astype(BF16)

    dirs = ((qf_ref, kf_ref, vf_ref, gf_ref, hf_ref), (qb_ref, kb_ref, vb_ref, gb_ref, hb_ref))
    chains = [divmod(bd, 2) for bd in range(2 * hf_ref.shape[0])]
    units = [(bd, hd) for bd in range(len(chains)) for hd in range(ML_HEADS)]
    refs = [tuple(ref.at[bb] for ref in dirs[d]) for bb, d in chains]

    qm, kp, va, st, s_raw, a1 = {}, {}, {}, {}, {}, {}
    for bd, hd in units:
        q_r, k_r, v_r = refs[bd][0:3]
        pair, second = divmod(hd, 2)
        kp[bd, hd] = k_r[:, pair * LANES:(pair + 1) * LANES]
        q_pair = q_r[pair * LANES:(pair + 1) * LANES, :]
        keep = jnp.logical_not(first_head_rows) if second else first_head_rows
        qm[bd, hd] = jnp.where(keep, q_pair, jnp.zeros_like(q_pair))
        va[bd, hd] = jnp.concatenate([v_r[hd * HEAD_DIM:(hd + 1) * HEAD_DIM, :], ones_blk], axis=0)
        st[bd, hd] = st_ref[bd * ML_HEADS + hd]
        s_raw[bd, hd] = jnp.dot(kp[bd, hd], qm[bd, hd], preferred_element_type=F32)
        a1[bd, hd] = jnp.dot(st[bd, hd].astype(BF16), qm[bd, hd], preferred_element_type=F32)

    g8 = [refs[bd][3][8 * d:8 * d + 8, :] for bd, (bb, d) in enumerate(chains)]
    lf8 = [jnp.minimum(g, 0.0) - jnp.log1p(jnp.exp(-jnp.abs(g))) for g in g8]
    parts = [jnp.concatenate([p.astype(F32) for p in _split3(lf)], axis=0).astype(BF16) for lf in lf8]
    cum = [jnp.dot(p, tri_ref[d], preferred_element_type=F32) for p, (bb, d) in zip(parts, chains)]
    cum = [c[16:24] + c[8:16] + c[0:8] for c in cum]
    b8 = [c[:, 0:L] for c in cum]
    btot8 = [c[:, L:2 * L] for c in cum]
    li8 = [pltpu.roll(g, ML_HEADS, 0) for g in g8]
    for bd in range(len(chains)):
        r_ref[bd, 0:8, :] = li8[bd] - b8[bd]
    ct = [r_ref[bd].T for bd in range(len(chains))]
    m_prev8 = [m_ref[bd] for bd in range(len(chains))]
    gg8 = [bt - b + li for bt, b, li in zip(btot8, b8, li8)]
    m_new8 = [jnp.maximum(bt + mp, jnp.max(gg, axis=-1, keepdims=True)) for bt, mp, gg in zip(btot8, m_prev8, gg8)]
    a8 = [jnp.exp(bt + mp - mn) for bt, mp, mn in zip(btot8, m_prev8, m_new8)]
    wk8 = [jnp.exp(gg - mn) for gg, mn in zip(gg8, m_new8)]
    inter8 = [b + mp for b, mp in zip(b8, m_prev8)]
    for bd in range(len(chains)):
        m_ref[bd] = m_new8[bd]

    dmat, m_t, s_t, a2, upd = {}, {}, {}, {}, {}
    for bd, hd in units:
        r = ML_HEADS + hd
        causal = (s_idx <= t_idx) if chains[bd][1] == 0 else (s_idx >= t_idx)
        dmat[bd, hd] = jnp.where(causal, ct[bd][:, r:r + 1] + b8[bd][r:r + 1], -jnp.inf)
        m_t[bd, hd] = jnp.maximum(inter8[bd][r:r + 1], jnp.max(dmat[bd, hd], axis=0, keepdims=True))
    for bd, hd in units:
        s_t[bd, hd] = (s_raw[bd, hd] * jnp.exp(dmat[bd, hd] - m_t[bd, hd])).astype(BF16)
    for bd, hd in units:
        r = ML_HEADS + hd
        a2[bd, hd] = jnp.dot(va[bd, hd], s_t[bd, hd], preferred_element_type=F32)
        vw = (va[bd, hd].astype(F32) * wk8[bd][r:r + 1]).astype(BF16)
        upd[bd, hd] = jnp.dot(vw, kp[bd, hd], preferred_element_type=F32)
    hts = {}
    for bd, hd in units:
        r = ML_HEADS + hd
        inter = inter8[bd][r:r + 1]
        nd = jnp.exp(inter - m_t[bd, hd]) * a1[bd, hd] + a2[bd, hd]
        den = nd[HEAD_DIM:HEAD_DIM + 1, :]
        hts[bd, hd] = nd[0:HEAD_DIM, :] / jnp.maximum(jnp.abs(den), jnp.exp(-m_t[bd, hd]))
        st_ref[bd * ML_HEADS + hd] = a8[bd][r:r + 1] * st[bd, hd] + upd[bd, hd]
    for bd in range(len(chains)):
        refs[bd][4][...] = jnp.concatenate([hts[bd, hd] for hd in range(ML_HEADS)], axis=0).T.astype(BF16)


def _mlstm(mqt, mk, mvt, gates, tri, ctx_len):
    B, TA, _ = mk.shape
    nc = TA // CHUNK
    ncc = ctx_len // CHUNK

    def fwd(i):
        return i

    def bwd(i):
        return jnp.where(i < ncc, ncc - 1 - i, nc - 1 - (i - ncc))

    nb = MLSTM_ROWS if B % MLSTM_ROWS == 0 else 1
    tok = lambda f: pl.BlockSpec((nb, CHUNK, ML_WIDTH), lambda b, i: (b, f(i), 0))
    feat = lambda f, r: pl.BlockSpec((nb, r, CHUNK), lambda b, i: (b, 0, f(i)))
    n_units = 2 * nb * ML_HEADS
    return pl.pallas_call(
        _mlstm_kernel,
        out_shape=[jax.ShapeDtypeStruct((B, TA, ML_WIDTH), BF16)] * 2,
        grid=(B // nb, nc),
        in_specs=[feat(fwd, ML_WIDTH), tok(fwd), feat(fwd, ML_WIDTH), feat(fwd, N_GATES),
                  feat(bwd, ML_WIDTH), tok(bwd), feat(bwd, ML_WIDTH), feat(bwd, N_GATES),
                  pl.BlockSpec((2, CHUNK, 2 * CHUNK), lambda b, i: (0, 0, 0))],
        out_specs=[tok(fwd), tok(bwd)],
        scratch_shapes=[pltpu.VMEM((n_units, VA_ROWS, LANES), F32), pltpu.VMEM((2 * nb, 8, LANES), F32),
                        pltpu.VMEM((2 * nb, CHUNK, LANES), F32)],
        compiler_params=pltpu.CompilerParams(vmem_limit_bytes=VMEM_LIMIT),
        name="mlstm",
    )(mqt, mk, mvt, gates, mqt, mk, mvt, gates, tri)


def _pool_kernel(pp_ref, ic_ref, bd_ref, ps_ref, o_ref, p_ref, s2_ref, s4_ref, s8_ref, *, ctx_len):
    total = pp_ref.shape[0]
    seqs = ((0, ctx_len, POOL_PAD), (ctx_len, total - ctx_len, 2 * POOL_PAD + ctx_len))
    rows = p_ref.shape[0]
    upper = slice(LANES, 2 * LANES)
    for ref in (p_ref, s2_ref, s4_ref, s8_ref):
        ref[...] = jnp.zeros_like(ref)
    for src, n, dst in seqs:
        p_ref[dst:dst + n, :] = pp_ref[src:src + n, :].astype(F32)
    lo, m = 8, rows - 16
    s2_ref[lo:lo + m, :] = p_ref[lo - 1:lo - 1 + m, :] + p_ref[lo:lo + m, :]
    s4_ref[lo:lo + m, :] = s2_ref[lo - 1:lo - 1 + m, :] + s2_ref[lo + 1:lo + 1 + m, :]
    s8_ref[lo:lo + m, :] = s4_ref[lo - 2:lo - 2 + m, upper] + s4_ref[lo + 2:lo + 2 + m, upper]
    for src, n, dst in seqs:
        first_group = lax.broadcasted_iota(jnp.int32, (n, LANES), 1) < POOL_GROUP_DIM
        s16 = s8_ref[dst - 4:dst - 4 + n, :] + s8_ref[dst + 4:dst + 4 + n, :]
        wsum = jnp.concatenate([jnp.where(first_group, s2_ref[dst:dst + n, 0:LANES], s4_ref[dst:dst + n, 0:LANES]),
                                jnp.where(first_group, s8_ref[dst:dst + n, :], s16)], axis=-1)
        p = wsum * ic_ref[src:src + n, :] - p_ref[dst:dst + n, :]
        y = jnp.dot(p.astype(BF16), bd_ref[...], preferred_element_type=F32) * ps_ref[...]
        o_ref[src:src + n, :] = y.astype(BF16)


def _inverse_window_counts(ctx_len, t_latent):
    half = np.repeat(np.array(POOL_WINDOWS) // 2, POOL_GROUP_DIM)[None, :]
    tables = []
    for n in (ctx_len, t_latent):
        t = np.arange(n)[:, None]
        tables.append(1.0 / (np.minimum(t + half, n) - np.maximum(t - half, 0)))
    return jnp.asarray(np.concatenate(tables, axis=0), dtype=F32)


def _pool(pp, inv_cnt, pool_bd, pool_scale, layer, ctx_len):
    B, TA, _ = pp.shape
    rows = TA + 3 * POOL_PAD
    kern = functools.partial(_pool_kernel, ctx_len=ctx_len)
    full = pl.BlockSpec((None, TA, POOL_WIDTH), lambda b: (b, 0, 0))
    return pl.pallas_call(
        kern,
        out_shape=jax.ShapeDtypeStruct((B, TA, POOL_WIDTH), BF16),
        grid=(B,),
        in_specs=[full, pl.BlockSpec((TA, POOL_WIDTH), lambda b: (0, 0)),
                  pl.BlockSpec((None, POOL_WIDTH, POOL_WIDTH), lambda b: (layer, 0, 0)),
                  pl.BlockSpec((None, 1, POOL_WIDTH), lambda b: (layer, 0, 0))],
        out_specs=full,
        scratch_shapes=[pltpu.VMEM((rows, POOL_WIDTH), F32)] * 3 + [pltpu.VMEM((rows, LANES), F32)],
        compiler_params=pltpu.CompilerParams(vmem_limit_bytes=VMEM_LIMIT),
        name="pool_mix",
    )(pp, inv_cnt, pool_bd, pool_scale)


def _mixffn_kernel(*refs, ctx_tiles, split, final):
    if split:
        ctx_ref, x_ref = refs[0:2]
        refs = refs[2:]
        from_ctx = pl.program_id(1) < ctx_tiles
        load = lambda bb: jnp.where(from_ctx, ctx_ref[bb], x_ref[bb])
    else:
        x_ref = refs[0]
        refs = refs[1:]
        load = lambda bb: x_ref[bb]
    (att_ref, hf_ref, hb_ref, mo_ref, pool_ref, mod_ref, mln_ref, wo_ref, nf_ref,
     wg_ref, wu_ref, wd_ref, fn_ref, o_ref) = refs
    nb, rows = o_ref.shape[0], o_ref.shape[1]
    lo_mask = lax.broadcasted_iota(jnp.int32, (rows, LANES), 1) < HEAD_DIM

    cats = []
    for bb in range(nb):
        ml = []
        for c in range(ML_WIDTH // LANES):
            sl = slice(c * LANES, (c + 1) * LANES)
            hs = hf_ref[bb, :, sl].astype(F32) + hb_ref[bb, :, sl].astype(F32)
            hn = hs * lax.rsqrt(_segment_ms(hs, lo_mask) + EPS) * mln_ref[:, sl]
            ml.append((hn * _sigmoid(mo_ref[bb, :, sl].astype(F32))).astype(BF16))
        cats.append(jnp.concatenate([att_ref[bb]] + ml + [pool_ref[bb]], axis=-1))
    mixes = [jnp.dot(cat, wo_ref[...], preferred_element_type=F32) for cat in cats]

    x1s, h2s = [], []
    for bb in range(nb):
        x1 = load(bb) + mod_ref[bb, 2] * mixes[bb]
        ms = jnp.mean(x1 * x1, axis=-1, keepdims=True)
        gain = nf_ref[...] * (1.0 + mod_ref[bb, 4])
        h2s.append((x1 * lax.rsqrt(ms + EPS) * gain + mod_ref[bb, 3]).astype(BF16))
        x1s.append(x1)
    downs = []
    for h2 in h2s:
        down = None
        for lo, hi in FF_SPLITS:
            gate = jnp.dot(h2, wg_ref[:, lo:hi], preferred_element_type=F32)
            up = jnp.dot(h2, wu_ref[:, lo:hi], preferred_element_type=F32)
            act = (gate * _sigmoid(gate) * up).astype(BF16)
            part = jnp.dot(act, wd_ref[lo:hi, :], preferred_element_type=F32)
            down = part if down is None else down + part
        downs.append(down)
    for bb in range(nb):
        out = x1s[bb] + mod_ref[bb, 5] * downs[bb]
        if final:
            ms = jnp.mean(out * out, axis=-1, keepdims=True)
            out = out * lax.rsqrt(ms + EPS) * fn_ref[...]
        o_ref[bb] = out


def _mix_ffn(x_in, att, hf, hb, mo, pool, mod5, layer, ml_norm, w_out_r, norm_ffn, wg, wu, wd, final_norm,
             ctx_tiles, final):
    nb = STEP_ROWS
    first = ctx_tiles if final else 0
    x_specs, x_ops, B, TA = _token_specs(x_in, nb, ctx_tiles, first)
    nt = TA // ROW_TILE - first
    row = lambda w: pl.BlockSpec((nb, ROW_TILE, w), lambda b, i: (b, i + first, 0))
    vec = lambda w: pl.BlockSpec((None, 1, w), lambda b, i: (layer, 0, 0))
    wspec = lambda r, c: pl.BlockSpec((None, r, c), lambda b, i: (layer, 0, 0), pipeline_mode=pl.Buffered(1))

    kern = functools.partial(_mixffn_kernel, ctx_tiles=ctx_tiles, split=len(x_ops) == 2, final=final)
    return pl.pallas_call(
        kern,
        out_shape=jax.ShapeDtypeStruct((B, nt * ROW_TILE, D_MODEL), F32),
        grid=(B // nb, nt),
        in_specs=x_specs + [pl.BlockSpec((nb, ROW_TILE, ATT_WIDTH), lambda b, i: (b, i, 0)),
                            row(ML_WIDTH), row(ML_WIDTH), row(ML_WIDTH), row(POOL_WIDTH),
                            _mod_spec(layer, nb, ctx_tiles, B // nb, first), vec(ML_WIDTH),
                            wspec(D_MODEL, D_MODEL), vec(D_MODEL), wspec(D_MODEL, D_FF), wspec(D_MODEL, D_FF),
                            wspec(D_FF, D_MODEL), pl.BlockSpec((1, D_MODEL), lambda b, i: (0, 0))],
        out_specs=pl.BlockSpec((nb, ROW_TILE, D_MODEL), lambda b, i: (b, i, 0)),
        compiler_params=pltpu.CompilerParams(vmem_limit_bytes=VMEM_LIMIT),
        name="mix_ffn",
    )(*x_ops, att, hf, hb, mo, pool, mod5, ml_norm, w_out_r, norm_ffn, wg, wu, wd, final_norm)


def _rope_tables(t_latent, ctx_len):
    f32 = np.float32
    half = HEAD_DIM // 2
    inv_freq = (f32(1.0) / np.power(f32(ROPE_THETA), np.arange(0, half, 2, dtype=f32) / f32(half))).astype(f32)
    rows = t_latent // GRID_W
    row = np.repeat(np.arange(rows, dtype=f32), GRID_W)
    col = np.tile(np.arange(GRID_W, dtype=f32), rows)
    a_row = row[:, None] * inv_freq
    a_col = col[:, None] * inv_freq
    ang = np.concatenate([a_row, a_row, a_col, a_col], -1).astype(f32)
    cos, sin = np.cos(ang), np.sin(ang)
    first_half = (np.arange(HEAD_DIM) % 32) < 16
    sa = np.where(first_half, -sin, 0.0)
    sb = np.where(first_half, 0.0, sin)
    pad = lambda a, fill: jnp.asarray(np.tile(np.concatenate([np.full((ctx_len, HEAD_DIM), fill, f32), a], 0), (1, 2)),
                                      dtype=F32)
    return pad(cos, 1.0), pad(sa, 0.0), pad(sb, 0.0)


_IN = np.arange(2064)
COLS_TOKEN = np.concatenate([HEAD_PERM, _IN[512:768], _IN[1024:1280], _IN[1536:1792], _IN[1808:2064]])
COLS_FEATURE = np.concatenate([_IN[768:1024], _IN[1280:1536], _IN[1792:1808]])
IN_SCALE = np.where((_IN >= 1024) & (_IN < 1280), HEAD_DIM ** -0.5, 1.0).astype(np.float32)
ROWS_OUT = np.concatenate([HEAD_PERM, np.arange(ATT_WIDTH, D_MODEL)])


def _take_runs(a, index, axis):
    cuts = [0] + [n for n in range(1, len(index)) if index[n] != index[n - 1] + 1] + [len(index)]
    runs = [lax.slice_in_dim(a, int(index[s]), int(index[e - 1]) + 1, axis=axis) for s, e in zip(cuts[:-1], cuts[1:])]
    return jnp.concatenate(runs, axis=axis)


def _prepare_params(w_in, b_gates, q_norm, k_norm, pool_w, w_out, w_ffn_gate, w_ffn_up, w_ffn_down):
    depth = w_in.shape[0]
    groups = len(POOL_WINDOWS)
    eye = jnp.eye(groups, dtype=F32)
    w_in_b = (w_in * IN_SCALE).astype(BF16)
    return dict(
        w_rows=_take_runs(w_in_b, COLS_TOKEN, 2),
        w_feat=jnp.swapaxes(_take_runs(w_in_b, COLS_FEATURE, 2), 1, 2),
        b_gates=jnp.broadcast_to(b_gates[:, :, None], (depth, N_GATES, ROW_TILE)),
        qn=jnp.tile(q_norm, (1, 2))[:, None, :], kn=jnp.tile(k_norm, (1, 2))[:, None, :],
        pool_bd=(pool_w[:, :, :, None, :] * eye[None, :, None, :, None]).reshape(depth, POOL_WIDTH, POOL_WIDTH).astype(BF16),
        w_out=_take_runs(w_out.astype(BF16), ROWS_OUT, 1),
        wg=w_ffn_gate.astype(BF16), wu=w_ffn_up.astype(BF16), wd=w_ffn_down.astype(BF16))


def _cumsum_matrices():
    s = np.arange(CHUNK)[:, None]
    t = np.arange(CHUNK)[None, :]
    ones = np.ones((CHUNK, CHUNK), np.float32)
    prefix = np.concatenate([(s <= t).astype(np.float32), ones], axis=1)
    suffix = np.concatenate([(s >= t).astype(np.float32), ones], axis=1)
    return jnp.asarray(np.stack([prefix, suffix]), dtype=BF16)


def kernel(x, c, ctx, c_ctx, w_ada, b_ada, norm_mix, w_in, b_gates, q_norm, k_norm, ml_norm, pool_w, pool_scale,
           w_out, norm_ffn, w_ffn_gate, w_ffn_up, w_ffn_down, final_norm):
    B, T, _ = x.shape
    ctx_len = ctx.shape[1]
    depth = w_ada.shape[0]
    n_rep = max(STEP_ROWS, INPROJ_ROWS)
    assert ctx_len % ROW_TILE == 0 and T % ROW_TILE == 0 and B % n_rep == 0 and B + n_rep <= MOD_ROWS
    ctx_tiles = ctx_len // ROW_TILE

    c_rep = jnp.broadcast_to(c_ctx[None, :], (n_rep, D_MODEL))
    c_all = jnp.concatenate([c, c_rep, jnp.zeros((MOD_ROWS - B - n_rep, D_MODEL), F32)], 0)
    mod = _modulation(c_all, w_ada, b_ada)
    mod5 = mod.reshape(depth, MOD_ROWS, 6, 1, D_MODEL)
    cos_t, sa_t, sb_t = _rope_tables(T, ctx_len)
    tri = _cumsum_matrices()
    inv_cnt = _inverse_window_counts(ctx_len, T)
    p = _prepare_params(w_in, b_gates, q_norm, k_norm, pool_w, w_out, w_ffn_gate, w_ffn_up, w_ffn_down)
    stack = lambda a: a[:, None, :]

    x_all = (ctx, x)
    for l in range(depth):
        final = l == depth - 1
        q, k, v, mk, mo, pp, mqt, mvt, gates = _in_proj(
            x_all, mod5, l, stack(norm_mix), p["w_rows"], p["w_feat"], p["b_gates"], p["qn"], p["kn"],
            cos_t, sa_t, sb_t, ctx_tiles)
        att = _attention(q, k, v, ctx_len, with_ctx=not final)
        hf, hb = _mlstm(mqt, mk, mvt, gates, tri, ctx_len)
        pool = _pool(pp, inv_cnt, p["pool_bd"], stack(pool_scale), l, ctx_len)
        x_all = _mix_ffn(x_all, att, hf, hb, mo, pool, mod5, l, stack(ml_norm), p["w_out"], stack(norm_ffn),
                         p["wg"], p["wu"], p["wd"], final_norm.reshape(1, -1), ctx_tiles, final)
    return x_all
```

```python
import functools

import jax
import jax.numpy as jnp
import numpy as np
from jax import lax
from jax.experimental import pallas as pl
from jax.experimental.pallas import tpu as pltpu

F32 = jnp.float32
BF16 = jnp.bfloat16

D_MODEL = 1024
HEAD_DIM = 64
GRID_W = 64
ATT_WIDTH = 512
KV_WIDTH = 128
ML_WIDTH = 256
POOL_WIDTH = 256
ATT_HEADS = 8
ML_HEADS = 4
N_GATES = 16
POOL_WINDOWS = (2, 4, 8, 16)
POOL_GROUP_DIM = 64
D_FF = 2816
CHUNK = 128
ROPE_THETA = 10000.0
EPS = 1e-6

LANES = 128
ROW_TILE = 256
MOD_ROWS = 24
POOL_PAD = 16
VMEM_LIMIT = 52 * 1024 * 1024

OFF_Q, OFF_K, OFF_V, OFF_MK, OFF_MO, OFF_PP = 0, 512, 640, 768, 1024, 1280
IN_COLS = OFF_PP + POOL_WIDTH
ROW_MQ, ROW_MV, ROW_G = 0, 256, 512
T_ROWS = ROW_G + N_GATES
Q_SCALE = HEAD_DIM ** -0.5 * float(np.log2(np.e))
VA_ROWS = 80
ATT_GROUP = 2
MLSTM_WAVE = 4
MLSTM_ROWS = 16
STEP_ROWS = 2
INPROJ_ROWS = 4
FF_SPLITS = ((0, 1536), (1536, D_FF))

HEAD_PERM = np.concatenate(
    [np.concatenate([np.arange(c * 64, (c + 1) * 64), np.arange((4 + c) * 64, (5 + c) * 64)]) for c in range(4)])


def _sigmoid(x):
    return 1.0 / (1.0 + jnp.exp(-x))


def _split3(a):
    hi = a.astype(BF16)
    r1 = a - hi.astype(F32)
    mid = r1.astype(BF16)
    lo = (r1 - mid.astype(F32)).astype(BF16)
    return hi, mid, lo


def _mod_kernel(c_ref, w_ref, b_ref, o_ref):
    c = c_ref[...]
    s = c * _sigmoid(c)
    s_hi, s_mid, _ = _split3(s)
    w = w_ref[...]
    w_hi = w.astype(BF16)
    w_mid = (w - w_hi.astype(F32)).astype(BF16)
    dot = functools.partial(jnp.dot, preferred_element_type=F32)
    acc = dot(s_mid, w_hi) + dot(s_hi, w_mid)
    acc = acc + dot(s_hi, w_hi)
    o_ref[...] = acc + b_ref[...]


def _modulation(c_all, w_ada, b_ada):
    depth = w_ada.shape[0]
    tn = 1024
    return pl.pallas_call(
        _mod_kernel,
        out_shape=jax.ShapeDtypeStruct((depth, MOD_ROWS, 6 * D_MODEL), F32),
        grid=(depth, 6 * D_MODEL // tn),
        in_specs=[
            pl.BlockSpec((MOD_ROWS, D_MODEL), lambda l, j: (0, 0)),
            pl.BlockSpec((None, D_MODEL, tn), lambda l, j: (l, 0, j)),
            pl.BlockSpec((None, 1, tn), lambda l, j: (l, 0, j)),
        ],
        out_specs=pl.BlockSpec((None, MOD_ROWS, tn), lambda l, j: (l, 0, j)),
        name="adaln_mod",
    )(c_all, w_ada, b_ada.reshape(depth, 1, 6 * D_MODEL))


def _segment_ms(x, lo_mask):
    x2 = x * x
    s_lo = jnp.sum(jnp.where(lo_mask, x2, 0.0), axis=-1, keepdims=True)
    s_hi = jnp.sum(jnp.where(lo_mask, 0.0, x2), axis=-1, keepdims=True)
    return jnp.where(lo_mask, s_lo, s_hi) * (1.0 / HEAD_DIM)


def _inproj_kernel(*refs, ctx_tiles, split):
    if split:
        ctx_ref, x_ref = refs[0:2]
        refs = refs[2:]
        from_ctx = pl.program_id(1) < ctx_tiles
        load = lambda bb: jnp.where(from_ctx, ctx_ref[bb], x_ref[bb])
    else:
        x_ref = refs[0]
        refs = refs[1:]
        load = lambda bb: x_ref[bb]
    (mod_ref, nm_ref, w_ref, wt_ref, bg_ref, qn_ref, kn_ref, cos_ref, sa_ref, sb_ref,
     q_ref, k_ref, v_ref, mk_ref, mo_ref, pp_ref, mqt_ref, mvt_ref, g_ref) = refs
    nb, rows = q_ref.shape[0], q_ref.shape[1]

    hs = []
    for bb in range(nb):
        x = load(bb)
        ms = jnp.mean(x * x, axis=-1, keepdims=True)
        gain = nm_ref[...] * (1.0 + mod_ref[bb, 1])
        hs.append((x * lax.rsqrt(ms + EPS) * gain + mod_ref[bb, 0]).astype(BF16))
    accs = [jnp.dot(h, w_ref[...], preferred_element_type=F32) for h in hs]
    accs_t = [lax.dot_general(wt_ref[...], h, (((1,), (1,)), ((), ())), preferred_element_type=F32)
              for h in hs]

    lo_mask = lax.broadcasted_iota(jnp.int32, (rows, LANES), 1) < HEAD_DIM
    cos, sa, sb = cos_ref[...], sa_ref[...], sb_ref[...]

    def norm_rope(xc, gain):
        yc = xc * lax.rsqrt(_segment_ms(xc, lo_mask) + EPS) * gain
        return yc * cos + pltpu.roll(yc, LANES - 16, 1) * sa + pltpu.roll(yc, 16, 1) * sb

    for bb in range(nb):
        acc, acc_t = accs[bb], accs_t[bb]
        blk = lambda off, width: acc[:, off:off + width]
        for c in range(ATT_WIDTH // LANES):
            qc = norm_rope(blk(OFF_Q + c * LANES, LANES), qn_ref[...])
            q_ref[bb, :, c * LANES:(c + 1) * LANES] = (qc * Q_SCALE).astype(BF16)
        k_ref[bb] = norm_rope(blk(OFF_K, LANES), kn_ref[...]).astype(BF16)
        v_ref[bb] = blk(OFF_V, KV_WIDTH).astype(BF16)
        mk_ref[bb] = blk(OFF_MK, ML_WIDTH).astype(BF16)
        mo_ref[bb] = blk(OFF_MO, ML_WIDTH).astype(BF16)
        pp_ref[bb] = blk(OFF_PP, POOL_WIDTH).astype(BF16)
        mqt_ref[bb] = acc_t[ROW_MQ:ROW_MQ + ML_WIDTH, :].astype(BF16)
        mvt_ref[bb] = acc_t[ROW_MV:ROW_MV + ML_WIDTH, :].astype(BF16)
        g_ref[bb] = acc_t[ROW_G:ROW_G + N_GATES, :] + bg_ref[...]


def _mod_spec(layer, nb, ctx_tiles, ctx_block, first=0):
    def idx(b, i):
        return (layer, jnp.where(i + first < ctx_tiles, ctx_block, b), 0, 0, 0)
    return pl.BlockSpec((None, nb, 6, 1, D_MODEL), idx)


def _token_specs(x_in, nb, ctx_tiles, first=0):
    if isinstance(x_in, tuple):
        ctx, x = x_in
        specs = [pl.BlockSpec((nb, ROW_TILE, D_MODEL), lambda b, i: (b, jnp.minimum(i, ctx_tiles - 1), 0)),
                 pl.BlockSpec((nb, ROW_TILE, D_MODEL), lambda b, i: (b, jnp.maximum(i - ctx_tiles, 0), 0))]
        return specs, [ctx, x], ctx.shape[0], ctx.shape[1] + x.shape[1]
    specs = [pl.BlockSpec((nb, ROW_TILE, D_MODEL), lambda b, i: (b, i + first, 0))]
    return specs, [x_in], x_in.shape[0], x_in.shape[1]


def _in_proj(x_in, mod5, layer, norm_mix, w_in_r, w_in_t, b_gates_b, qn, kn, cos_t, sa_t, sb_t, ctx_tiles):
    nb = INPROJ_ROWS
    x_specs, x_ops, B, TA = _token_specs(x_in, nb, ctx_tiles)
    nt = TA // ROW_TILE
    row = lambda w: pl.BlockSpec((nb, ROW_TILE, w), lambda b, i: (b, i, 0))
    col = lambda r: pl.BlockSpec((nb, r, ROW_TILE), lambda b, i: (b, 0, i))
    vec = lambda w: pl.BlockSpec((None, 1, w), lambda b, i: (layer, 0, 0))
    full = lambda r, c: pl.BlockSpec((None, r, c), lambda b, i: (layer, 0, 0))
    tab = pl.BlockSpec((ROW_TILE, LANES), lambda b, i: (i, 0))
    row_widths = (ATT_WIDTH, KV_WIDTH, KV_WIDTH, ML_WIDTH, ML_WIDTH, POOL_WIDTH)
    col_heights = (ML_WIDTH, ML_WIDTH)
    out_shapes = [jax.ShapeDtypeStruct((B, TA, w), BF16) for w in row_widths]
    out_shapes += [jax.ShapeDtypeStruct((B, r, TA), BF16) for r in col_heights]
    out_shapes.append(jax.ShapeDtypeStruct((B, N_GATES, TA), F32))
    out_specs = [row(w) for w in row_widths] + [col(r) for r in col_heights] + [col(N_GATES)]
    kern = functools.partial(_inproj_kernel, ctx_tiles=ctx_tiles, split=len(x_ops) == 2)
    return pl.pallas_call(
        kern,
        out_shape=out_shapes,
        grid=(B // nb, nt),
        in_specs=x_specs + [_mod_spec(layer, nb, ctx_tiles, B // nb), vec(D_MODEL), full(D_MODEL, IN_COLS),
                            full(T_ROWS, D_MODEL), full(N_GATES, ROW_TILE), vec(LANES), vec(LANES), tab, tab, tab],
        out_specs=out_specs,
        compiler_params=pltpu.CompilerParams(vmem_limit_bytes=VMEM_LIMIT),
        name="in_proj",
    )(*x_ops, mod5, norm_mix, w_in_r, w_in_t, b_gates_b, qn, kn, cos_t, sa_t, sb_t)


def _attn_kernel(q_ref, k_ref, v_ref, o_ref, k0_ref, k1_ref, va_ref, *, ctx_len, first_tile):
    i = pl.program_id(1)
    total = k_ref.shape[0]

    @pl.when(i == 0)
    def _():
        k = k_ref[...]
        lo = lax.broadcasted_iota(jnp.int32, k.shape, 1) < HEAD_DIM
        zero = jnp.zeros_like(k)
        k0_ref[...] = jnp.where(lo, k, zero)
        k1_ref[...] = jnp.where(lo, zero, k)
        va_ref[:, 0:LANES] = v_ref[...]
        va_ref[:, LANES:2 * LANES] = jnp.ones((total, LANES), BF16)

    rows = q_ref.shape[0]
    lo_mask = lax.broadcasted_iota(jnp.int32, (rows, LANES), 1) < HEAD_DIM

    def block(nk):
        chunks = list(range(ATT_WIDTH // LANES))
        for g in range(0, len(chunks), ATT_GROUP):
            cs = chunks[g:g + ATT_GROUP]
            ss = [lax.dot_general(q_ref[:, c * LANES:(c + 1) * LANES], kr[0:nk, :], (((1,), (1,)), ((), ())),
                                  preferred_element_type=F32) for c in cs for kr in (k0_ref, k1_ref)]
            ms = [jnp.max(s, axis=-1, keepdims=True) for s in ss]
            ps = [jnp.exp2(s - m).astype(BF16) for s, m in zip(ss, ms)]
            acs = [jnp.dot(p, va_ref[0:nk, :], preferred_element_type=F32) for p in ps]
            outs = [a[:, 0:LANES] / a[:, LANES:2 * LANES] for a in acs]
            for n, c in enumerate(cs):
                o_ref[:, c * LANES:(c + 1) * LANES] = jnp.where(lo_mask, outs[2 * n], outs[2 * n + 1]).astype(BF16)

    if first_tile == 0:
        pl.when(i == 0)(lambda: block(ctx_len))
        pl.when(i > 0)(lambda: block(total))
    else:
        block(total)


def _attention(q, k, v, ctx_len, with_ctx):
    B, TA, _ = q.shape
    first = 0 if with_ctx else ctx_len // ROW_TILE
    nt = TA // ROW_TILE - first
    kern = functools.partial(_attn_kernel, ctx_len=ctx_len, first_tile=first)
    return pl.pallas_call(
        kern,
        out_shape=jax.ShapeDtypeStruct((B, nt * ROW_TILE, ATT_WIDTH), BF16),
        grid=(B, nt),
        in_specs=[pl.BlockSpec((None, ROW_TILE, ATT_WIDTH), lambda b, i: (b, i + first, 0)),
                  pl.BlockSpec((None, TA, KV_WIDTH), lambda b, i: (b, 0, 0)),
                  pl.BlockSpec((None, TA, KV_WIDTH), lambda b, i: (b, 0, 0))],
        out_specs=pl.BlockSpec((None, ROW_TILE, ATT_WIDTH), lambda b, i: (b, i, 0)),
        scratch_shapes=[pltpu.VMEM((TA, KV_WIDTH), BF16), pltpu.VMEM((TA, KV_WIDTH), BF16),
                        pltpu.VMEM((TA, 2 * LANES), BF16)],
        compiler_params=pltpu.CompilerParams(vmem_limit_bytes=VMEM_LIMIT),
        name="attention",
    )(q, k, v)


def _mlstm_kernel(qf_ref, kf_ref, vf_ref, gf_ref, qb_ref, kb_ref, vb_ref, gb_ref, tri_ref, hf_ref, hb_ref,
                  st_ref, m_ref, r_ref):
    i = pl.program_id(1)

    @pl.when(i == 0)
    def _():
        st_ref[...] = jnp.zeros_like(st_ref)
        m_ref[...] = jnp.zeros_like(m_ref)
        r_ref[...] = jnp.zeros_like(r_ref)

    L = CHUNK
    s_idx = lax.broadcasted_iota(jnp.int32, (L, L), 0)
    t_idx = lax.broadcasted_iota(jnp.int32, (L, L), 1)
    first_head_rows = lax.broadcasted_iota(jnp.int32, (LANES, L), 0) < HEAD_DIM
    ones_blk = jnp.where(lax.broadcasted_iota(jnp.int32, (VA_ROWS - HEAD_DIM, L), 0) == 0, 1.0, 0.0).astype(BF16)

    dirs = ((qf_ref, kf_ref, vf_ref, gf_ref, hf_ref), (qb_ref, kb_ref, vb_ref, gb_ref, hb_ref))
    chains = [divmod(bd, 2) for bd in range(2 * hf_ref.shape[0])]
    refs = [tuple(ref.at[bb] for ref in dirs[d]) for bb, d in chains]

    heads = range(ML_HEADS)
    qm, kp, va, st, s_raw, a1 = {}, {}, {}, {}, {}, {}
    b8, inter8, a8, wk8, ct = {}, {}, {}, {}, {}
    dmat, m_t, s_t, a2, upd, hts = {}, {}, {}, {}, {}, {}

    def stage_operands(bds):
        for bd in bds:
            q_r, k_r, v_r = refs[bd][0:3]
            for hd in heads:
                pair, second = divmod(hd, 2)
                kp[bd, hd] = k_r[:, pair * LANES:(pair + 1) * LANES]
                q_pair = q_r[pair * LANES:(pair + 1) * LANES, :]
                keep = jnp.logical_not(first_head_rows) if second else first_head_rows
                qm[bd, hd] = jnp.where(keep, q_pair, jnp.zeros_like(q_pair))
                va[bd, hd] = jnp.concatenate([v_r[hd * HEAD_DIM:(hd + 1) * HEAD_DIM, :], ones_blk], axis=0)
                st[bd, hd] = st_ref[bd * ML_HEADS + hd]
                s_raw[bd, hd] = jnp.dot(kp[bd, hd], qm[bd, hd], preferred_element_type=F32)
                a1[bd, hd] = jnp.dot(st[bd, hd].astype(BF16), qm[bd, hd], preferred_element_type=F32)

    def stage_gates(bds):
        g8 = {bd: refs[bd][3][8 * chains[bd][1]:8 * chains[bd][1] + 8, :] for bd in bds}
        lf8 = {bd: jnp.minimum(g8[bd], 0.0) - jnp.log1p(jnp.exp(-jnp.abs(g8[bd]))) for bd in bds}
        parts = {bd: jnp.concatenate([p.astype(F32) for p in _split3(lf8[bd])], axis=0).astype(BF16) for bd in bds}
        cum = {bd: jnp.dot(parts[bd], tri_ref[chains[bd][1]], preferred_element_type=F32) for bd in bds}
        cum = {bd: c[16:24] + c[8:16] + c[0:8] for bd, c in cum.items()}
        li8 = {bd: pltpu.roll(g8[bd], ML_HEADS, 0) for bd in bds}
        for bd in bds:
            b8[bd] = cum[bd][:, 0:L]
            r_ref[bd, 0:8, :] = li8[bd] - b8[bd]
        for bd in bds:
            ct[bd] = r_ref[bd].T
        for bd in bds:
            btot = cum[bd][:, L:2 * L]
            m_prev = m_ref[bd]
            gg = btot - b8[bd] + li8[bd]
            m_new = jnp.maximum(btot + m_prev, jnp.max(gg, axis=-1, keepdims=True))
            a8[bd] = jnp.exp(btot + m_prev - m_new)
            wk8[bd] = jnp.exp(gg - m_new)
            inter8[bd] = b8[bd] + m_prev
            m_ref[bd] = m_new

    def stage_decay(bds):
        for bd in bds:
            causal = (s_idx <= t_idx) if chains[bd][1] == 0 else (s_idx >= t_idx)
            for hd in heads:
                r = ML_HEADS + hd
                dmat[bd, hd] = jnp.where(causal, ct[bd][:, r:r + 1] + b8[bd][r:r + 1], -jnp.inf)
                m_t[bd, hd] = jnp.maximum(inter8[bd][r:r + 1], jnp.max(dmat[bd, hd], axis=0, keepdims=True))

    def stage_scores(bds):
        for bd in bds:
            for hd in heads:
                s_t[bd, hd] = (s_raw[bd, hd] * jnp.exp(dmat[bd, hd] - m_t[bd, hd])).astype(BF16)

    def stage_values(bds):
        for bd in bds:
            for hd in heads:
                r = ML_HEADS + hd
                a2[bd, hd] = jnp.dot(va[bd, hd], s_t[bd, hd], preferred_element_type=F32)
                vw = (va[bd, hd].astype(F32) * wk8[bd][r:r + 1]).astype(BF16)
                upd[bd, hd] = jnp.dot(vw, kp[bd, hd], preferred_element_type=F32)

    def stage_finish(bds):
        for bd in bds:
            for hd in heads:
                r = ML_HEADS + hd
                inter = inter8[bd][r:r + 1]
                nd = jnp.exp(inter - m_t[bd, hd]) * a1[bd, hd] + a2[bd, hd]
                den = nd[HEAD_DIM:HEAD_DIM + 1, :]
                hts[bd, hd] = nd[0:HEAD_DIM, :] / jnp.maximum(jnp.abs(den), jnp.exp(-m_t[bd, hd]))
                st_ref[bd * ML_HEADS + hd] = a8[bd][r:r + 1] * st[bd, hd] + upd[bd, hd]
        for bd in bds:
            refs[bd][4][...] = jnp.concatenate([hts[bd, hd] for hd in heads], axis=0).T.astype(BF16)

    stages = (stage_operands, stage_gates, stage_decay, stage_scores, stage_values, stage_finish)
    waves = [list(range(w, min(w + MLSTM_WAVE, len(chains)))) for w in range(0, len(chains), MLSTM_WAVE)]
    for step in range(len(stages) + len(waves) - 1):
        for w, bds in enumerate(waves):
            if 0 <= step - w < len(stages):
                stages[step - w](bds)


def _mlstm(mqt, mk, mvt, gates, tri, ctx_len):
    B, TA, _ = mk.shape
    nc = TA // CHUNK
    ncc = ctx_len // CHUNK

    def fwd(i):
        return i

    def bwd(i):
        return jnp.where(i < ncc, ncc - 1 - i, nc - 1 - (i - ncc))

    nb = MLSTM_ROWS if B % MLSTM_ROWS == 0 else 1
    tok = lambda f: pl.BlockSpec((nb, CHUNK, ML_WIDTH), lambda b, i: (b, f(i), 0))
    feat = lambda f, r: pl.BlockSpec((nb, r, CHUNK), lambda b, i: (b, 0, f(i)))
    n_units = 2 * nb * ML_HEADS
    return pl.pallas_call(
        _mlstm_kernel,
        out_shape=[jax.ShapeDtypeStruct((B, TA, ML_WIDTH), BF16)] * 2,
        grid=(B // nb, nc),
        in_specs=[feat(fwd, ML_WIDTH), tok(fwd), feat(fwd, ML_WIDTH), feat(fwd, N_GATES),
                  feat(bwd, ML_WIDTH), tok(bwd), feat(bwd, ML_WIDTH), feat(bwd, N_GATES),
                  pl.BlockSpec((2, CHUNK, 2 * CHUNK), lambda b, i: (0, 0, 0))],
        out_specs=[tok(fwd), tok(bwd)],
        scratch_shapes=[pltpu.VMEM((n_units, VA_ROWS, LANES), F32), pltpu.VMEM((2 * nb, 8, LANES), F32),
                        pltpu.VMEM((2 * nb, CHUNK, LANES), F32)],
        compiler_params=pltpu.CompilerParams(vmem_limit_bytes=VMEM_LIMIT),
        name="mlstm",
    )(mqt, mk, mvt, gates, mqt, mk, mvt, gates, tri)


def _pool_kernel(pp_ref, ic_ref, bd_ref, ps_ref, o_ref, p_ref, s2_ref, s4_ref, s8_ref, *, ctx_len):
    total = pp_ref.shape[0]
    seqs = ((0, ctx_len, POOL_PAD), (ctx_len, total - ctx_len, 2 * POOL_PAD + ctx_len))
    rows = p_ref.shape[0]
    upper = slice(LANES, 2 * LANES)
    for ref in (p_ref, s2_ref, s4_ref, s8_ref):
        ref[...] = jnp.zeros_like(ref)
    for src, n, dst in seqs:
        p_ref[dst:dst + n, :] = pp_ref[src:src + n, :].astype(F32)
    lo, m = 8, rows - 16
    s2_ref[lo:lo + m, :] = p_ref[lo - 1:lo - 1 + m, :] + p_ref[lo:lo + m, :]
    s4_ref[lo:lo + m, :] = s2_ref[lo - 1:lo - 1 + m, :] + s2_ref[lo + 1:lo + 1 + m, :]
    s8_ref[lo:lo + m, :] = s4_ref[lo - 2:lo - 2 + m, upper] + s4_ref[lo + 2:lo + 2 + m, upper]
    for src, n, dst in seqs:
        first_group = lax.broadcasted_iota(jnp.int32, (n, LANES), 1) < POOL_GROUP_DIM
        s16 = s8_ref[dst - 4:dst - 4 + n, :] + s8_ref[dst + 4:dst + 4 + n, :]
        wsum = jnp.concatenate([jnp.where(first_group, s2_ref[dst:dst + n, 0:LANES], s4_ref[dst:dst + n, 0:LANES]),
                                jnp.where(first_group, s8_ref[dst:dst + n, :], s16)], axis=-1)
        p = wsum * ic_ref[src:src + n, :] - p_ref[dst:dst + n, :]
        y = jnp.dot(p.astype(BF16), bd_ref[...], preferred_element_type=F32) * ps_ref[...]
        o_ref[src:src + n, :] = y.astype(BF16)


def _inverse_window_counts(ctx_len, t_latent):
    half = np.repeat(np.array(POOL_WINDOWS) // 2, POOL_GROUP_DIM)[None, :]
    tables = []
    for n in (ctx_len, t_latent):
        t = np.arange(n)[:, None]
        tables.append(1.0 / (np.minimum(t + half, n) - np.maximum(t - half, 0)))
    return jnp.asarray(np.concatenate(tables, axis=0), dtype=F32)


def _pool(pp, inv_cnt, pool_bd, pool_scale, layer, ctx_len):
    B, TA, _ = pp.shape
    rows = TA + 3 * POOL_PAD
    kern = functools.partial(_pool_kernel, ctx_len=ctx_len)
    full = pl.BlockSpec((None, TA, POOL_WIDTH), lambda b: (b, 0, 0))
    return pl.pallas_call(
        kern,
        out_shape=jax.ShapeDtypeStruct((B, TA, POOL_WIDTH), BF16),
        grid=(B,),
        in_specs=[full, pl.BlockSpec((TA, POOL_WIDTH), lambda b: (0, 0)),
                  pl.BlockSpec((None, POOL_WIDTH, POOL_WIDTH), lambda b: (layer, 0, 0)),
                  pl.BlockSpec((None, 1, POOL_WIDTH), lambda b: (layer, 0, 0))],
        out_specs=full,
        scratch_shapes=[pltpu.VMEM((rows, POOL_WIDTH), F32)] * 3 + [pltpu.VMEM((rows, LANES), F32)],
        compiler_params=pltpu.CompilerParams(vmem_limit_bytes=VMEM_LIMIT),
        name="pool_mix",
    )(pp, inv_cnt, pool_bd, pool_scale)


def _mixffn_kernel(*refs, ctx_tiles, split, final):
    if split:
        ctx_ref, x_ref = refs[0:2]
        refs = refs[2:]
        from_ctx = pl.program_id(1) < ctx_tiles
        load = lambda bb: jnp.where(from_ctx, ctx_ref[bb], x_ref[bb])
    else:
        x_ref = refs[0]
        refs = refs[1:]
        load = lambda bb: x_ref[bb]
    (att_ref, hf_ref, hb_ref, mo_ref, pool_ref, mod_ref, mln_ref, wo_ref, nf_ref,
     wg_ref, wu_ref, wd_ref, fn_ref, o_ref) = refs
    nb, rows = o_ref.shape[0], o_ref.shape[1]
    lo_mask = lax.broadcasted_iota(jnp.int32, (rows, LANES), 1) < HEAD_DIM

    cats = []
    for bb in range(nb):
        ml = []
        for c in range(ML_WIDTH // LANES):
            sl = slice(c * LANES, (c + 1) * LANES)
            hs = hf_ref[bb, :, sl].astype(F32) + hb_ref[bb, :, sl].astype(F32)
            hn = hs * lax.rsqrt(_segment_ms(hs, lo_mask) + EPS) * mln_ref[:, sl]
            ml.append((hn * _sigmoid(mo_ref[bb, :, sl].astype(F32))).astype(BF16))
        cats.append(jnp.concatenate([att_ref[bb]] + ml + [pool_ref[bb]], axis=-1))
    mixes = [jnp.dot(cat, wo_ref[...], preferred_element_type=F32) for cat in cats]

    x1s, h2s = [], []
    for bb in range(nb):
        x1 = load(bb) + mod_ref[bb, 2] * mixes[bb]
        ms = jnp.mean(x1 * x1, axis=-1, keepdims=True)
        gain = nf_ref[...] * (1.0 + mod_ref[bb, 4])
        h2s.append((x1 * lax.rsqrt(ms + EPS) * gain + mod_ref[bb, 3]).astype(BF16))
        x1s.append(x1)
    downs = []
    for h2 in h2s:
        down = None
        for lo, hi in FF_SPLITS:
            gate = jnp.dot(h2, wg_ref[:, lo:hi], preferred_element_type=F32)
            up = jnp.dot(h2, wu_ref[:, lo:hi], preferred_element_type=F32)
            act = (gate * _sigmoid(gate) * up).astype(BF16)
            part = jnp.dot(act, wd_ref[lo:hi, :], preferred_element_type=F32)
            down = part if down is None else down + part
        downs.append(down)
    for bb in range(nb):
        out = x1s[bb] + mod_ref[bb, 5] * downs[bb]
        if final:
            ms = jnp.mean(out * out, axis=-1, keepdims=True)
            out = out * lax.rsqrt(ms + EPS) * fn_ref[...]
        o_ref[bb] = out


def _mix_ffn(x_in, att, hf, hb, mo, pool, mod5, layer, ml_norm, w_out_r, norm_ffn, wg, wu, wd, final_norm,
             ctx_tiles, final):
    nb = STEP_ROWS
    first = ctx_tiles if final else 0
    x_specs, x_ops, B, TA = _token_specs(x_in, nb, ctx_tiles, first)
    nt = TA // ROW_TILE - first
    row = lambda w: pl.BlockSpec((nb, ROW_TILE, w), lambda b, i: (b, i + first, 0))
    vec = lambda w: pl.BlockSpec((None, 1, w), lambda b, i: (layer, 0, 0))
    wspec = lambda r, c: pl.BlockSpec((None, r, c), lambda b, i: (layer, 0, 0), pipeline_mode=pl.Buffered(1))

    kern = functools.partial(_mixffn_kernel, ctx_tiles=ctx_tiles, split=len(x_ops) == 2, final=final)
    return pl.pallas_call(
        kern,
        out_shape=jax.ShapeDtypeStruct((B, nt * ROW_TILE, D_MODEL), F32),
        grid=(B // nb, nt),
        in_specs=x_specs + [pl.BlockSpec((nb, ROW_TILE, ATT_WIDTH), lambda b, i: (b, i, 0)),
                            row(ML_WIDTH), row(ML_WIDTH), row(ML_WIDTH), row(POOL_WIDTH),
                            _mod_spec(layer, nb, ctx_tiles, B // nb, first), vec(ML_WIDTH),
                            wspec(D_MODEL, D_MODEL), vec(D_MODEL), wspec(D_MODEL, D_FF), wspec(D_MODEL, D_FF),
                            wspec(D_FF, D_MODEL), pl.BlockSpec((1, D_MODEL), lambda b, i: (0, 0))],
        out_specs=pl.BlockSpec((nb, ROW_TILE, D_MODEL), lambda b, i: (b, i, 0)),
        compiler_params=pltpu.CompilerParams(vmem_limit_bytes=VMEM_LIMIT),
        name="mix_ffn",
    )(*x_ops, att, hf, hb, mo, pool, mod5, ml_norm, w_out_r, norm_ffn, wg, wu, wd, final_norm)


def _rope_tables(t_latent, ctx_len):
    f32 = np.float32
    half = HEAD_DIM // 2
    inv_freq = (f32(1.0) / np.power(f32(ROPE_THETA), np.arange(0, half, 2, dtype=f32) / f32(half))).astype(f32)
    rows = t_latent // GRID_W
    row = np.repeat(np.arange(rows, dtype=f32), GRID_W)
    col = np.tile(np.arange(GRID_W, dtype=f32), rows)
    a_row = row[:, None] * inv_freq
    a_col = col[:, None] * inv_freq
    ang = np.concatenate([a_row, a_row, a_col, a_col], -1).astype(f32)
    cos, sin = np.cos(ang), np.sin(ang)
    first_half = (np.arange(HEAD_DIM) % 32) < 16
    sa = np.where(first_half, -sin, 0.0)
    sb = np.where(first_half, 0.0, sin)
    pad = lambda a, fill: jnp.asarray(np.tile(np.concatenate([np.full((ctx_len, HEAD_DIM), fill, f32), a], 0), (1, 2)),
                                      dtype=F32)
    return pad(cos, 1.0), pad(sa, 0.0), pad(sb, 0.0)


_IN = np.arange(2064)
COLS_TOKEN = np.concatenate([HEAD_PERM, _IN[512:768], _IN[1024:1280], _IN[1536:1792], _IN[1808:2064]])
COLS_FEATURE = np.concatenate([_IN[768:1024], _IN[1280:1536], _IN[1792:1808]])
IN_SCALE = np.where((_IN >= 1024) & (_IN < 1280), HEAD_DIM ** -0.5, 1.0).astype(np.float32)
ROWS_OUT = np.concatenate([HEAD_PERM, np.arange(ATT_WIDTH, D_MODEL)])


def _take_runs(a, index, axis):
    cuts = [0] + [n for n in range(1, len(index)) if index[n] != index[n - 1] + 1] + [len(index)]
    runs = [lax.slice_in_dim(a, int(index[s]), int(index[e - 1]) + 1, axis=axis) for s, e in zip(cuts[:-1], cuts[1:])]
    return jnp.concatenate(runs, axis=axis)


def _prepare_params(w_in, b_gates, q_norm, k_norm, pool_w, w_out, w_ffn_gate, w_ffn_up, w_ffn_down):
    depth = w_in.shape[0]
    groups = len(POOL_WINDOWS)
    eye = jnp.eye(groups, dtype=F32)
    w_in_b = (w_in * IN_SCALE).astype(BF16)
    return dict(
        w_rows=_take_runs(w_in_b, COLS_TOKEN, 2),
        w_feat=jnp.swapaxes(_take_runs(w_in_b, COLS_FEATURE, 2), 1, 2),
        b_gates=jnp.broadcast_to(b_gates[:, :, None], (depth, N_GATES, ROW_TILE)),
        qn=jnp.tile(q_norm, (1, 2))[:, None, :], kn=jnp.tile(k_norm, (1, 2))[:, None, :],
        pool_bd=(pool_w[:, :, :, None, :] * eye[None, :, None, :, None]).reshape(depth, POOL_WIDTH, POOL_WIDTH).astype(BF16),
        w_out=_take_runs(w_out.astype(BF16), ROWS_OUT, 1),
        wg=w_ffn_gate.astype(BF16), wu=w_ffn_up.astype(BF16), wd=w_ffn_down.astype(BF16))


def _cumsum_matrices():
    s = np.arange(CHUNK)[:, None]
    t = np.arange(CHUNK)[None, :]
    ones = np.ones((CHUNK, CHUNK), np.float32)
    prefix = np.concatenate([(s <= t).astype(np.float32), ones], axis=1)
    suffix = np.concatenate([(s >= t).astype(np.float32), ones], axis=1)
    return jnp.asarray(np.stack([prefix, suffix]), dtype=BF16)


def kernel(x, c, ctx, c_ctx, w_ada, b_ada, norm_mix, w_in, b_gates, q_norm, k_norm, ml_norm, pool_w, pool_scale,
           w_out, norm_ffn, w_ffn_gate, w_ffn_up, w_ffn_down, final_norm):
    B, T, _ = x.shape
    ctx_len = ctx.shape[1]
    depth = w_ada.shape[0]
    n_rep = max(STEP_ROWS, INPROJ_ROWS)
    assert ctx_len % ROW_TILE == 0 and T % ROW_TILE == 0 and B % n_rep == 0 and B + n_rep <= MOD_ROWS
    ctx_tiles = ctx_len // ROW_TILE

    c_rep = jnp.broadcast_to(c_ctx[None, :], (n_rep, D_MODEL))
    c_all = jnp.concatenate([c, c_rep, jnp.zeros((MOD_ROWS - B - n_rep, D_MODEL), F32)], 0)
    mod = _modulation(c_all, w_ada, b_ada)
    mod5 = mod.reshape(depth, MOD_ROWS, 6, 1, D_MODEL)
    cos_t, sa_t, sb_t = _rope_tables(T, ctx_len)
    tri = _cumsum_matrices()
    inv_cnt = _inverse_window_counts(ctx_len, T)
    p = _prepare_params(w_in, b_gates, q_norm, k_norm, pool_w, w_out, w_ffn_gate, w_ffn_up, w_ffn_down)
    stack = lambda a: a[:, None, :]

    x_all = (ctx, x)
    for l in range(depth):
        final = l == depth - 1
        q, k, v, mk, mo, pp, mqt, mvt, gates = _in_proj(
            x_all, mod5, l, stack(norm_mix), p["w_rows"], p["w_feat"], p["b_gates"], p["qn"], p["kn"],
            cos_t, sa_t, sb_t, ctx_tiles)
        att = _attention(q, k, v, ctx_len, with_ctx=not final)
        hf, hb = _mlstm(mqt, mk, mvt, gates, tri, ctx_len)
        pool = _pool(pp, inv_cnt, p["pool_bd"], stack(pool_scale), l, ctx_len)
        x_all = _mix_ffn(x_all, att, hf, hb, mo, pool, mod5, l, stack(ml_norm), p["w_out"], stack(norm_ffn),
                         p["wg"], p["wu"], p["wd"], final_norm.reshape(1, -1), ctx_tiles, final)
    return x_all
```

```python
import functools

import jax
import jax.numpy as jnp
import numpy as np
from jax import lax
from jax.experimental import pallas as pl
from jax.experimental.pallas import tpu as pltpu

F32 = jnp.float32
BF16 = jnp.bfloat16

D_MODEL = 1024
HEAD_DIM = 64
GRID_W = 64
ATT_WIDTH = 512
KV_WIDTH = 128
ML_WIDTH = 256
POOL_WIDTH = 256
ATT_HEADS = 8
ML_HEADS = 4
N_GATES = 16
POOL_WINDOWS = (2, 4, 8, 16)
POOL_GROUP_DIM = 64
D_FF = 2816
CHUNK = 128
ROPE_THETA = 10000.0
EPS = 1e-6

LANES = 128
ROW_TILE = 256
MOD_ROWS = 24
POOL_PAD = 16
VMEM_LIMIT = 52 * 1024 * 1024

OFF_Q, OFF_K, OFF_V, OFF_MK, OFF_MO, OFF_PP = 0, 512, 640, 768, 1024, 1280
IN_COLS = OFF_PP + POOL_WIDTH
ROW_MQ, ROW_MV, ROW_G = 0, 256, 512
T_ROWS = ROW_G + N_GATES
Q_SCALE = HEAD_DIM ** -0.5 * float(np.log2(np.e))
VA_ROWS = 80
ATT_ROWS = 4
ATT_GROUP = 2
MLSTM_WAVE = 4
MLSTM_ROWS = 16
STEP_ROWS = 2
INPROJ_ROWS = 4
FF_SPLITS = ((0, 1536), (1536, D_FF))

HEAD_PERM = np.concatenate(
    [np.concatenate([np.arange(c * 64, (c + 1) * 64), np.arange((4 + c) * 64, (5 + c) * 64)]) for c in range(4)])


def _sigmoid(x):
    return 1.0 / (1.0 + jnp.exp(-x))


def _split3(a):
    hi = a.astype(BF16)
    r1 = a - hi.astype(F32)
    mid = r1.astype(BF16)
    lo = (r1 - mid.astype(F32)).astype(BF16)
    return hi, mid, lo


def _mod_kernel(c_ref, w_ref, b_ref, o_ref):
    c = c_ref[...]
    s = c * _sigmoid(c)
    s_hi, s_mid, _ = _split3(s)
    w = w_ref[...]
    w_hi = w.astype(BF16)
    w_mid = (w - w_hi.astype(F32)).astype(BF16)
    dot = functools.partial(jnp.dot, preferred_element_type=F32)
    acc = dot(s_mid, w_hi) + dot(s_hi, w_mid)
    acc = acc + dot(s_hi, w_hi)
    o_ref[...] = acc + b_ref[...]


def _modulation(c_all, w_ada, b_ada):
    depth = w_ada.shape[0]
    tn = 1024
    return pl.pallas_call(
        _mod_kernel,
        out_shape=jax.ShapeDtypeStruct((depth, MOD_ROWS, 6 * D_MODEL), F32),
        grid=(depth, 6 * D_MODEL // tn),
        in_specs=[
            pl.BlockSpec((MOD_ROWS, D_MODEL), lambda l, j: (0, 0)),
            pl.BlockSpec((None, D_MODEL, tn), lambda l, j: (l, 0, j)),
            pl.BlockSpec((None, 1, tn), lambda l, j: (l, 0, j)),
        ],
        out_specs=pl.BlockSpec((None, MOD_ROWS, tn), lambda l, j: (l, 0, j)),
        name="adaln_mod",
    )(c_all, w_ada, b_ada.reshape(depth, 1, 6 * D_MODEL))


def _segment_ms(x, lo_mask):
    x2 = x * x
    s_lo = jnp.sum(jnp.where(lo_mask, x2, 0.0), axis=-1, keepdims=True)
    s_hi = jnp.sum(jnp.where(lo_mask, 0.0, x2), axis=-1, keepdims=True)
    return jnp.where(lo_mask, s_lo, s_hi) * (1.0 / HEAD_DIM)


def _inproj_kernel(*refs, ctx_tiles, split):
    if split:
        ctx_ref, x_ref = refs[0:2]
        refs = refs[2:]
        from_ctx = pl.program_id(1) < ctx_tiles
        load = lambda bb: jnp.where(from_ctx, ctx_ref[bb], x_ref[bb])
    else:
        x_ref = refs[0]
        refs = refs[1:]
        load = lambda bb: x_ref[bb]
    (mod_ref, nm_ref, w_ref, wt_ref, bg_ref, qn_ref, kn_ref, cos_ref, sa_ref, sb_ref,
     q_ref, k_ref, v_ref, mk_ref, mo_ref, pp_ref, mqt_ref, mvt_ref, g_ref) = refs
    nb, rows = q_ref.shape[0], q_ref.shape[1]

    hs = []
    for bb in range(nb):
        x = load(bb)
        ms = jnp.mean(x * x, axis=-1, keepdims=True)
        gain = nm_ref[...] * (1.0 + mod_ref[bb, 1])
        hs.append((x * lax.rsqrt(ms + EPS) * gain + mod_ref[bb, 0]).astype(BF16))
    accs = [jnp.dot(h, w_ref[...], preferred_element_type=F32) for h in hs]
    accs_t = [lax.dot_general(wt_ref[...], h, (((1,), (1,)), ((), ())), preferred_element_type=F32)
              for h in hs]

    lo_mask = lax.broadcasted_iota(jnp.int32, (rows, LANES), 1) < HEAD_DIM
    cos, sa, sb = cos_ref[...], sa_ref[...], sb_ref[...]

    def norm_rope(xc, gain):
        yc = xc * lax.rsqrt(_segment_ms(xc, lo_mask) + EPS) * gain
        return yc * cos + pltpu.roll(yc, LANES - 16, 1) * sa + pltpu.roll(yc, 16, 1) * sb

    for bb in range(nb):
        acc, acc_t = accs[bb], accs_t[bb]
        blk = lambda off, width: acc[:, off:off + width]
        for c in range(ATT_WIDTH // LANES):
            qc = norm_rope(blk(OFF_Q + c * LANES, LANES), qn_ref[...])
            q_ref[bb, :, c * LANES:(c + 1) * LANES] = (qc * Q_SCALE).astype(BF16)
        k_ref[bb] = norm_rope(blk(OFF_K, LANES), kn_ref[...]).astype(BF16)
        v_ref[bb] = blk(OFF_V, KV_WIDTH).astype(BF16)
        mk_ref[bb] = blk(OFF_MK, ML_WIDTH).astype(BF16)
        mo_ref[bb] = blk(OFF_MO, ML_WIDTH).astype(BF16)
        pp_ref[bb] = blk(OFF_PP, POOL_WIDTH).astype(BF16)
        mqt_ref[bb] = acc_t[ROW_MQ:ROW_MQ + ML_WIDTH, :].astype(BF16)
        mvt_ref[bb] = acc_t[ROW_MV:ROW_MV + ML_WIDTH, :].astype(BF16)
        g_ref[bb] = acc_t[ROW_G:ROW_G + N_GATES, :] + bg_ref[...]


def _mod_spec(layer, nb, ctx_tiles, ctx_block, first=0):
    def idx(b, i):
        return (layer, jnp.where(i + first < ctx_tiles, ctx_block, b), 0, 0, 0)
    return pl.BlockSpec((None, nb, 6, 1, D_MODEL), idx)


def _token_specs(x_in, nb, ctx_tiles, first=0):
    if isinstance(x_in, tuple):
        ctx, x = x_in
        specs = [pl.BlockSpec((nb, ROW_TILE, D_MODEL), lambda b, i: (b, jnp.minimum(i, ctx_tiles - 1), 0)),
                 pl.BlockSpec((nb, ROW_TILE, D_MODEL), lambda b, i: (b, jnp.maximum(i - ctx_tiles, 0), 0))]
        return specs, [ctx, x], ctx.shape[0], ctx.shape[1] + x.shape[1]
    specs = [pl.BlockSpec((nb, ROW_TILE, D_MODEL), lambda b, i: (b, i + first, 0))]
    return specs, [x_in], x_in.shape[0], x_in.shape[1]


def _in_proj(x_in, mod5, layer, norm_mix, w_in_r, w_in_t, b_gates_b, qn, kn, cos_t, sa_t, sb_t, ctx_tiles):
    nb = INPROJ_ROWS
    x_specs, x_ops, B, TA = _token_specs(x_in, nb, ctx_tiles)
    nt = TA // ROW_TILE
    row = lambda w: pl.BlockSpec((nb, ROW_TILE, w), lambda b, i: (b, i, 0))
    col = lambda r: pl.BlockSpec((nb, r, ROW_TILE), lambda b, i: (b, 0, i))
    vec = lambda w: pl.BlockSpec((None, 1, w), lambda b, i: (layer, 0, 0))
    full = lambda r, c: pl.BlockSpec((None, r, c), lambda b, i: (layer, 0, 0))
    tab = pl.BlockSpec((ROW_TILE, LANES), lambda b, i: (i, 0))
    row_widths = (ATT_WIDTH, KV_WIDTH, KV_WIDTH, ML_WIDTH, ML_WIDTH, POOL_WIDTH)
    col_heights = (ML_WIDTH, ML_WIDTH)
    out_shapes = [jax.ShapeDtypeStruct((B, TA, w), BF16) for w in row_widths]
    out_shapes += [jax.ShapeDtypeStruct((B, r, TA), BF16) for r in col_heights]
    out_shapes.append(jax.ShapeDtypeStruct((B, N_GATES, TA), F32))
    out_specs = [row(w) for w in row_widths] + [col(r) for r in col_heights] + [col(N_GATES)]
    kern = functools.partial(_inproj_kernel, ctx_tiles=ctx_tiles, split=len(x_ops) == 2)
    return pl.pallas_call(
        kern,
        out_shape=out_shapes,
        grid=(B // nb, nt),
        in_specs=x_specs + [_mod_spec(layer, nb, ctx_tiles, B // nb), vec(D_MODEL), full(D_MODEL, IN_COLS),
                            full(T_ROWS, D_MODEL), full(N_GATES, ROW_TILE), vec(LANES), vec(LANES), tab, tab, tab],
        out_specs=out_specs,
        compiler_params=pltpu.CompilerParams(vmem_limit_bytes=VMEM_LIMIT),
        name="in_proj",
    )(*x_ops, mod5, norm_mix, w_in_r, w_in_t, b_gates_b, qn, kn, cos_t, sa_t, sb_t)


def _attn_kernel(q_ref, k_ref, v_ref, o_ref, k0_ref, k1_ref, va_ref, *, ctx_len, first_tile):
    i = pl.program_id(1)
    nb, total = k_ref.shape[0], k_ref.shape[1]

    @pl.when(i == 0)
    def _():
        for bb in range(nb):
            k = k_ref[bb]
            lo = lax.broadcasted_iota(jnp.int32, k.shape, 1) < HEAD_DIM
            zero = jnp.zeros_like(k)
            k0_ref[bb] = jnp.where(lo, k, zero)
            k1_ref[bb] = jnp.where(lo, zero, k)
            va_ref[bb, :, 0:LANES] = v_ref[bb]
            va_ref[bb, :, LANES:2 * LANES] = jnp.ones((total, LANES), BF16)

    rows = q_ref.shape[1]
    lo_mask = lax.broadcasted_iota(jnp.int32, (rows, LANES), 1) < HEAD_DIM

    def block(nk):
        chunks = list(range(ATT_WIDTH // LANES))
        for bb in range(nb):
            for g in range(0, len(chunks), ATT_GROUP):
                cs = chunks[g:g + ATT_GROUP]
                ss = [lax.dot_general(q_ref[bb, :, c * LANES:(c + 1) * LANES], kr[bb, 0:nk, :], (((1,), (1,)), ((), ())),
                                      preferred_element_type=F32) for c in cs for kr in (k0_ref, k1_ref)]
                ms = [jnp.max(s, axis=-1, keepdims=True) for s in ss]
                ps = [jnp.exp2(s - m).astype(BF16) for s, m in zip(ss, ms)]
                acs = [jnp.dot(p, va_ref[bb, 0:nk, :], preferred_element_type=F32) for p in ps]
                outs = [a[:, 0:LANES] / a[:, LANES:2 * LANES] for a in acs]
                for n, c in enumerate(cs):
                    o_ref[bb, :, c * LANES:(c + 1) * LANES] = jnp.where(lo_mask, outs[2 * n], outs[2 * n + 1]).astype(BF16)

    if first_tile == 0:
        pl.when(i == 0)(lambda: block(ctx_len))
        pl.when(i > 0)(lambda: block(total))
    else:
        block(total)


def _attention(q, k, v, ctx_len, with_ctx):
    B, TA, _ = q.shape
    first = 0 if with_ctx else ctx_len // ROW_TILE
    nt = TA // ROW_TILE - first
    nb = ATT_ROWS if B % ATT_ROWS == 0 else 1
    kern = functools.partial(_attn_kernel, ctx_len=ctx_len, first_tile=first)
    return pl.pallas_call(
        kern,
        out_shape=jax.ShapeDtypeStruct((B, nt * ROW_TILE, ATT_WIDTH), BF16),
        grid=(B // nb, nt),
        in_specs=[pl.BlockSpec((nb, ROW_TILE, ATT_WIDTH), lambda b, i: (b, i + first, 0)),
                  pl.BlockSpec((nb, TA, KV_WIDTH), lambda b, i: (b, 0, 0)),
                  pl.BlockSpec((nb, TA, KV_WIDTH), lambda b, i: (b, 0, 0))],
        out_specs=pl.BlockSpec((nb, ROW_TILE, ATT_WIDTH), lambda b, i: (b, i, 0)),
        scratch_shapes=[pltpu.VMEM((nb, TA, KV_WIDTH), BF16), pltpu.VMEM((nb, TA, KV_WIDTH), BF16),
                        pltpu.VMEM((nb, TA, 2 * LANES), BF16)],
        compiler_params=pltpu.CompilerParams(vmem_limit_bytes=VMEM_LIMIT),
        name="attention",
    )(q, k, v)


def _mlstm_kernel(qf_ref, kf_ref, vf_ref, gf_ref, qb_ref, kb_ref, vb_ref, gb_ref, tri_ref, hf_ref, hb_ref,
                  st_ref, m_ref, r_ref):
    i = pl.program_id(1)

    @pl.when(i == 0)
    def _():
        st_ref[...] = jnp.zeros_like(st_ref)
        m_ref[...] = jnp.zeros_like(m_ref)
        r_ref[...] = jnp.zeros_like(r_ref)

    L = CHUNK
    s_idx = lax.broadcasted_iota(jnp.int32, (L, L), 0)
    t_idx = lax.broadcasted_iota(jnp.int32, (L, L), 1)
    first_head_rows = lax.broadcasted_iota(jnp.int32, (LANES, L), 0) < HEAD_DIM
    ones_blk = jnp.where(lax.broadcasted_iota(jnp.int32, (VA_ROWS - HEAD_DIM, L), 0) == 0, 1.0, 0.0).astype(BF16)

    dirs = ((qf_ref, kf_ref, vf_ref, gf_ref, hf_ref), (qb_ref, kb_ref, vb_ref, gb_ref, hb_ref))
    chains = [divmod(bd, 2) for bd in range(2 * hf_ref.shape[0])]
    refs = [tuple(ref.at[bb] for ref in dirs[d]) for bb, d in chains]

    heads = range(ML_HEADS)
    qm, kp, va, st, s_raw, a1 = {}, {}, {}, {}, {}, {}
    b8, inter8, a8, wk8, ct = {}, {}, {}, {}, {}
    dmat, m_t, s_t, a2, upd, hts = {}, {}, {}, {}, {}, {}

    def stage_operands(bds):
        for bd in bds:
            q_r, k_r, v_r = refs[bd][0:3]
            for hd in heads:
                pair, second = divmod(hd, 2)
                kp[bd, hd] = k_r[:, pair * LANES:(pair + 1) * LANES]
                q_pair = q_r[pair * LANES:(pair + 1) * LANES, :]
                keep = jnp.logical_not(first_head_rows) if second else first_head_rows
                qm[bd, hd] = jnp.where(keep, q_pair, jnp.zeros_like(q_pair))
                va[bd, hd] = jnp.concatenate([v_r[hd * HEAD_DIM:(hd + 1) * HEAD_DIM, :], ones_blk], axis=0)
                st[bd, hd] = st_ref[bd * ML_HEADS + hd]
                s_raw[bd, hd] = jnp.dot(kp[bd, hd], qm[bd, hd], preferred_element_type=F32)
                a1[bd, hd] = jnp.dot(st[bd, hd].astype(BF16), qm[bd, hd], preferred_element_type=F32)

    def stage_gates(bds):
        g8 = {bd: refs[bd][3][8 * chains[bd][1]:8 * chains[bd][1] + 8, :] for bd in bds}
        lf8 = {bd: jnp.minimum(g8[bd], 0.0) - jnp.log1p(jnp.exp(-jnp.abs(g8[bd]))) for bd in bds}
        parts = {bd: jnp.concatenate([p.astype(F32) for p in _split3(lf8[bd])], axis=0).astype(BF16) for bd in bds}
        cum = {bd: jnp.dot(parts[bd], tri_ref[chains[bd][1]], preferred_element_type=F32) for bd in bds}
        cum = {bd: c[16:24] + c[8:16] + c[0:8] for bd, c in cum.items()}
        li8 = {bd: pltpu.roll(g8[bd], ML_HEADS, 0) for bd in bds}
        for bd in bds:
            b8[bd] = cum[bd][:, 0:L]
            r_ref[bd, 0:8, :] = li8[bd] - b8[bd]
        for bd in bds:
            ct[bd] = r_ref[bd].T
        for bd in bds:
            btot = cum[bd][:, L:2 * L]
            m_prev = m_ref[bd]
            gg = btot - b8[bd] + li8[bd]
            m_new = jnp.maximum(btot + m_prev, jnp.max(gg, axis=-1, keepdims=True))
            a8[bd] = jnp.exp(btot + m_prev - m_new)
            wk8[bd] = jnp.exp(gg - m_new)
            inter8[bd] = b8[bd] + m_prev
            m_ref[bd] = m_new

    def stage_decay(bds):
        for bd in bds:
            causal = (s_idx <= t_idx) if chains[bd][1] == 0 else (s_idx >= t_idx)
            for hd in heads:
                r = ML_HEADS + hd
                dmat[bd, hd] = jnp.where(causal, ct[bd][:, r:r + 1] + b8[bd][r:r + 1], -jnp.inf)
                m_t[bd, hd] = jnp.maximum(inter8[bd][r:r + 1], jnp.max(dmat[bd, hd], axis=0, keepdims=True))

    def stage_scores(bds):
        for bd in bds:
            for hd in heads:
                s_t[bd, hd] = (s_raw[bd, hd] * jnp.exp(dmat[bd, hd] - m_t[bd, hd])).astype(BF16)

    def stage_values(bds):
        for bd in bds:
            for hd in heads:
                r = ML_HEADS + hd
                a2[bd, hd] = jnp.dot(va[bd, hd], s_t[bd, hd], preferred_element_type=F32)
                vw = (va[bd, hd].astype(F32) * wk8[bd][r:r + 1]).astype(BF16)
                upd[bd, hd] = jnp.dot(vw, kp[bd, hd], preferred_element_type=F32)

    def stage_finish(bds):
        for bd in bds:
            for hd in heads:
                r = ML_HEADS + hd
                inter = inter8[bd][r:r + 1]
                nd = jnp.exp(inter - m_t[bd, hd]) * a1[bd, hd] + a2[bd, hd]
                den = nd[HEAD_DIM:HEAD_DIM + 1, :]
                hts[bd, hd] = nd[0:HEAD_DIM, :] / jnp.maximum(jnp.abs(den), jnp.exp(-m_t[bd, hd]))
                st_ref[bd * ML_HEADS + hd] = a8[bd][r:r + 1] * st[bd, hd] + upd[bd, hd]
        for bd in bds:
            refs[bd][4][...] = jnp.concatenate([hts[bd, hd] for hd in heads], axis=0).T.astype(BF16)

    stages = (stage_operands, stage_gates, stage_decay, stage_scores, stage_values, stage_finish)
    waves = [list(range(w, min(w + MLSTM_WAVE, len(chains)))) for w in range(0, len(chains), MLSTM_WAVE)]
    for step in range(len(stages) + len(waves) - 1):
        for w, bds in enumerate(waves):
            if 0 <= step - w < len(stages):
                stages[step - w](bds)


def _mlstm(mqt, mk, mvt, gates, tri, ctx_len):
    B, TA, _ = mk.shape
    nc = TA // CHUNK
    ncc = ctx_len // CHUNK

    def fwd(i):
        return i

    def bwd(i):
        return jnp.where(i < ncc, ncc - 1 - i, nc - 1 - (i - ncc))

    nb = MLSTM_ROWS if B % MLSTM_ROWS == 0 else 1
    tok = lambda f: pl.BlockSpec((nb, CHUNK, ML_WIDTH), lambda b, i: (b, f(i), 0))
    feat = lambda f, r: pl.BlockSpec((nb, r, CHUNK), lambda b, i: (b, 0, f(i)))
    n_units = 2 * nb * ML_HEADS
    return pl.pallas_call(
        _mlstm_kernel,
        out_shape=[jax.ShapeDtypeStruct((B, TA, ML_WIDTH), BF16)] * 2,
        grid=(B // nb, nc),
        in_specs=[feat(fwd, ML_WIDTH), tok(fwd), feat(fwd, ML_WIDTH), feat(fwd, N_GATES),
                  feat(bwd, ML_WIDTH), tok(bwd), feat(bwd, ML_WIDTH), feat(bwd, N_GATES),
                  pl.BlockSpec((2, CHUNK, 2 * CHUNK), lambda b, i: (0, 0, 0))],
        out_specs=[tok(fwd), tok(bwd)],
        scratch_shapes=[pltpu.VMEM((n_units, VA_ROWS, LANES), F32), pltpu.VMEM((2 * nb, 8, LANES), F32),
                        pltpu.VMEM((2 * nb, CHUNK, LANES), F32)],
        compiler_params=pltpu.CompilerParams(vmem_limit_bytes=VMEM_LIMIT),
        name="mlstm",
    )(mqt, mk, mvt, gates, mqt, mk, mvt, gates, tri)


def _pool_kernel(pp_ref, ic_ref, bd_ref, ps_ref, o_ref, p_ref, s2_ref, s4_ref, s8_ref, *, ctx_len):
    total = pp_ref.shape[0]
    seqs = ((0, ctx_len, POOL_PAD), (ctx_len, total - ctx_len, 2 * POOL_PAD + ctx_len))
    rows = p_ref.shape[0]
    upper = slice(LANES, 2 * LANES)
    for ref in (p_ref, s2_ref, s4_ref, s8_ref):
        ref[...] = jnp.zeros_like(ref)
    for src, n, dst in seqs:
        p_ref[dst:dst + n, :] = pp_ref[src:src + n, :].astype(F32)
    lo, m = 8, rows - 16
    s2_ref[lo:lo + m, :] = p_ref[lo - 1:lo - 1 + m, :] + p_ref[lo:lo + m, :]
    s4_ref[lo:lo + m, :] = s2_ref[lo - 1:lo - 1 + m, :] + s2_ref[lo + 1:lo + 1 + m, :]
    s8_ref[lo:lo + m, :] = s4_ref[lo - 2:lo - 2 + m, upper] + s4_ref[lo + 2:lo + 2 + m, upper]
    for src, n, dst in seqs:
        first_group = lax.broadcasted_iota(jnp.int32, (n, LANES), 1) < POOL_GROUP_DIM
        s16 = s8_ref[dst - 4:dst - 4 + n, :] + s8_ref[dst + 4:dst + 4 + n, :]
        wsum = jnp.concatenate([jnp.where(first_group, s2_ref[dst:dst + n, 0:LANES], s4_ref[dst:dst + n, 0:LANES]),
                                jnp.where(first_group, s8_ref[dst:dst + n, :], s16)], axis=-1)
        p = wsum * ic_ref[src:src + n, :] - p_ref[dst:dst + n, :]
        y = jnp.dot(p.astype(BF16), bd_ref[...], preferred_element_type=F32) * ps_ref[...]
        o_ref[src:src + n, :] = y.astype(BF16)


def _inverse_window_counts(ctx_len, t_latent):
    half = np.repeat(np.array(POOL_WINDOWS) // 2, POOL_GROUP_DIM)[None, :]
    tables = []
    for n in (ctx_len, t_latent):
        t = np.arange(n)[:, None]
        tables.append(1.0 / (np.minimum(t + half, n) - np.maximum(t - half, 0)))
    return jnp.asarray(np.concatenate(tables, axis=0), dtype=F32)


def _pool(pp, inv_cnt, pool_bd, pool_scale, layer, ctx_len):
    B, TA, _ = pp.shape
    rows = TA + 3 * POOL_PAD
    kern = functools.partial(_pool_kernel, ctx_len=ctx_len)
    full = pl.BlockSpec((None, TA, POOL_WIDTH), lambda b: (b, 0, 0))
    return pl.pallas_call(
        kern,
        out_shape=jax.ShapeDtypeStruct((B, TA, POOL_WIDTH), BF16),
        grid=(B,),
        in_specs=[full, pl.BlockSpec((TA, POOL_WIDTH), lambda b: (0, 0)),
                  pl.BlockSpec((None, POOL_WIDTH, POOL_WIDTH), lambda b: (layer, 0, 0)),
                  pl.BlockSpec((None, 1, POOL_WIDTH), lambda b: (layer, 0, 0))],
        out_specs=full,
        scratch_shapes=[pltpu.VMEM((rows, POOL_WIDTH), F32)] * 3 + [pltpu.VMEM((rows, LANES), F32)],
        compiler_params=pltpu.CompilerParams(vmem_limit_bytes=VMEM_LIMIT),
        name="pool_mix",
    )(pp, inv_cnt, pool_bd, pool_scale)


def _mixffn_kernel(*refs, ctx_tiles, split, final):
    if split:
        ctx_ref, x_ref = refs[0:2]
        refs = refs[2:]
        from_ctx = pl.program_id(1) < ctx_tiles
        load = lambda bb: jnp.where(from_ctx, ctx_ref[bb], x_ref[bb])
    else:
        x_ref = refs[0]
        refs = refs[1:]
        load = lambda bb: x_ref[bb]
    (att_ref, hf_ref, hb_ref, mo_ref, pool_ref, mod_ref, mln_ref, wo_ref, nf_ref,
     wg_ref, wu_ref, wd_ref, fn_ref, o_ref) = refs
    nb, rows = o_ref.shape[0], o_ref.shape[1]
    lo_mask = lax.broadcasted_iota(jnp.int32, (rows, LANES), 1) < HEAD_DIM

    cats = []
    for bb in range(nb):
        ml = []
        for c in range(ML_WIDTH // LANES):
            sl = slice(c * LANES, (c + 1) * LANES)
            hs = hf_ref[bb, :, sl].astype(F32) + hb_ref[bb, :, sl].astype(F32)
            hn = hs * lax.rsqrt(_segment_ms(hs, lo_mask) + EPS) * mln_ref[:, sl]
            ml.append((hn * _sigmoid(mo_ref[bb, :, sl].astype(F32))).astype(BF16))
        cats.append(jnp.concatenate([att_ref[bb]] + ml + [pool_ref[bb]], axis=-1))
    mixes = [jnp.dot(cat, wo_ref[...], preferred_element_type=F32) for cat in cats]

    x1s, h2s = [], []
    for bb in range(nb):
        x1 = load(bb) + mod_ref[bb, 2] * mixes[bb]
        ms = jnp.mean(x1 * x1, axis=-1, keepdims=True)
        gain = nf_ref[...] * (1.0 + mod_ref[bb, 4])
        h2s.append((x1 * lax.rsqrt(ms + EPS) * gain + mod_ref[bb, 3]).astype(BF16))
        x1s.append(x1)
    downs = []
    for h2 in h2s:
        down = None
        for lo, hi in FF_SPLITS:
            gate = jnp.dot(h2, wg_ref[:, lo:hi], preferred_element_type=F32)
            up = jnp.dot(h2, wu_ref[:, lo:hi], preferred_element_type=F32)
            act = (gate * _sigmoid(gate) * up).astype(BF16)
            part = jnp.dot(act, wd_ref[lo:hi, :], preferred_element_type=F32)
            down = part if down is None else down + part
        downs.append(down)
    for bb in range(nb):
        out = x1s[bb] + mod_ref[bb, 5] * downs[bb]
        if final:
            ms = jnp.mean(out * out, axis=-1, keepdims=True)
            out = out * lax.rsqrt(ms + EPS) * fn_ref[...]
        o_ref[bb] = out


def _mix_ffn(x_in, att, hf, hb, mo, pool, mod5, layer, ml_norm, w_out_r, norm_ffn, wg, wu, wd, final_norm,
             ctx_tiles, final):
    nb = STEP_ROWS
    first = ctx_tiles if final else 0
    x_specs, x_ops, B, TA = _token_specs(x_in, nb, ctx_tiles, first)
    nt = TA // ROW_TILE - first
    row = lambda w: pl.BlockSpec((nb, ROW_TILE, w), lambda b, i: (b, i + first, 0))
    vec = lambda w: pl.BlockSpec((None, 1, w), lambda b, i: (layer, 0, 0))
    wspec = lambda r, c: pl.BlockSpec((None, r, c), lambda b, i: (layer, 0, 0), pipeline_mode=pl.Buffered(1))

    kern = functools.partial(_mixffn_kernel, ctx_tiles=ctx_tiles, split=len(x_ops) == 2, final=final)
    return pl.pallas_call(
        kern,
        out_shape=jax.ShapeDtypeStruct((B, nt * ROW_TILE, D_MODEL), F32),
        grid=(B // nb, nt),
        in_specs=x_specs + [pl.BlockSpec((nb, ROW_TILE, ATT_WIDTH), lambda b, i: (b, i, 0)),
                            row(ML_WIDTH), row(ML_WIDTH), row(ML_WIDTH), row(POOL_WIDTH),
                            _mod_spec(layer, nb, ctx_tiles, B // nb, first), vec(ML_WIDTH),
                            wspec(D_MODEL, D_MODEL), vec(D_MODEL), wspec(D_MODEL, D_FF), wspec(D_MODEL, D_FF),
                            wspec(D_FF, D_MODEL), pl.BlockSpec((1, D_MODEL), lambda b, i: (0, 0))],
        out_specs=pl.BlockSpec((nb, ROW_TILE, D_MODEL), lambda b, i: (b, i, 0)),
        compiler_params=pltpu.CompilerParams(vmem_limit_bytes=VMEM_LIMIT),
        name="mix_ffn",
    )(*x_ops, att, hf, hb, mo, pool, mod5, ml_norm, w_out_r, norm_ffn, wg, wu, wd, final_norm)


def _rope_tables(t_latent, ctx_len):
    f32 = np.float32
    half = HEAD_DIM // 2
    inv_freq = (f32(1.0) / np.power(f32(ROPE_THETA), np.arange(0, half, 2, dtype=f32) / f32(half))).astype(f32)
    rows = t_latent // GRID_W
    row = np.repeat(np.arange(rows, dtype=f32), GRID_W)
    col = np.tile(np.arange(GRID_W, dtype=f32), rows)
    a_row = row[:, None] * inv_freq
    a_col = col[:, None] * inv_freq
    ang = np.concatenate([a_row, a_row, a_col, a_col], -1).astype(f32)
    cos, sin = np.cos(ang), np.sin(ang)
    first_half = (np.arange(HEAD_DIM) % 32) < 16
    sa = np.where(first_half, -sin, 0.0)
    sb = np.where(first_half, 0.0, sin)
    pad = lambda a, fill: jnp.asarray(np.tile(np.concatenate([np.full((ctx_len, HEAD_DIM), fill, f32), a], 0), (1, 2)),
                                      dtype=F32)
    return pad(cos, 1.0), pad(sa, 0.0), pad(sb, 0.0)


_IN = np.arange(2064)
COLS_TOKEN = np.concatenate([HEAD_PERM, _IN[512:768], _IN[1024:1280], _IN[1536:1792], _IN[1808:2064]])
COLS_FEATURE = np.concatenate([_IN[768:1024], _IN[1280:1536], _IN[1792:1808]])
IN_SCALE = np.where((_IN >= 1024) & (_IN < 1280), HEAD_DIM ** -0.5, 1.0).astype(np.float32)
ROWS_OUT = np.concatenate([HEAD_PERM, np.arange(ATT_WIDTH, D_MODEL)])


def _take_runs(a, index, axis):
    cuts = [0] + [n for n in range(1, len(index)) if index[n] != index[n - 1] + 1] + [len(index)]
    runs = [lax.slice_in_dim(a, int(index[s]), int(index[e - 1]) + 1, axis=axis) for s, e in zip(cuts[:-1], cuts[1:])]
    return jnp.concatenate(runs, axis=axis)


def _prepare_params(w_in, b_gates, q_norm, k_norm, pool_w, w_out, w_ffn_gate, w_ffn_up, w_ffn_down):
    depth = w_in.shape[0]
    groups = len(POOL_WINDOWS)
    eye = jnp.eye(groups, dtype=F32)
    w_in_b = (w_in * IN_SCALE).astype(BF16)
    return dict(
        w_rows=_take_runs(w_in_b, COLS_TOKEN, 2),
        w_feat=jnp.swapaxes(_take_runs(w_in_b, COLS_FEATURE, 2), 1, 2),
        b_gates=jnp.broadcast_to(b_gates[:, :, None], (depth, N_GATES, ROW_TILE)),
        qn=jnp.tile(q_norm, (1, 2))[:, None, :], kn=jnp.tile(k_norm, (1, 2))[:, None, :],
        pool_bd=(pool_w[:, :, :, None, :] * eye[None, :, None, :, None]).reshape(depth, POOL_WIDTH, POOL_WIDTH).astype(BF16),
        w_out=_take_runs(w_out.astype(BF16), ROWS_OUT, 1),
        wg=w_ffn_gate.astype(BF16), wu=w_ffn_up.astype(BF16), wd=w_ffn_down.astype(BF16))


def _cumsum_matrices():
    s = np.arange(CHUNK)[:, None]
    t = np.arange(CHUNK)[None, :]
    ones = np.ones((CHUNK, CHUNK), np.float32)
    prefix = np.concatenate([(s <= t).astype(np.float32), ones], axis=1)
    suffix = np.concatenate([(s >= t).astype(np.float32), ones], axis=1)
    return jnp.asarray(np.stack([prefix, suffix]), dtype=BF16)


def kernel(x, c, ctx, c_ctx, w_ada, b_ada, norm_mix, w_in, b_gates, q_norm, k_norm, ml_norm, pool_w, pool_scale,
           w_out, norm_ffn, w_ffn_gate, w_ffn_up, w_ffn_down, final_norm):
    B, T, _ = x.shape
    ctx_len = ctx.shape[1]
    depth = w_ada.shape[0]
    n_rep = max(STEP_ROWS, INPROJ_ROWS)
    assert ctx_len % ROW_TILE == 0 and T % ROW_TILE == 0 and B % n_rep == 0 and B + n_rep <= MOD_ROWS
    ctx_tiles = ctx_len // ROW_TILE

    c_rep = jnp.broadcast_to(c_ctx[None, :], (n_rep, D_MODEL))
    c_all = jnp.concatenate([c, c_rep, jnp.zeros((MOD_ROWS - B - n_rep, D_MODEL), F32)], 0)
    mod = _modulation(c_all, w_ada, b_ada)
    mod5 = mod.reshape(depth, MOD_ROWS, 6, 1, D_MODEL)
    cos_t, sa_t, sb_t = _rope_tables(T, ctx_len)
    tri = _cumsum_matrices()
    inv_cnt = _inverse_window_counts(ctx_len, T)
    p = _prepare_params(w_in, b_gates, q_norm, k_norm, pool_w, w_out, w_ffn_gate, w_ffn_up, w_ffn_down)
    stack = lambda a: a[:, None, :]

    x_all = (ctx, x)
    for l in range(depth):
        final = l == depth - 1
        q, k, v, mk, mo, pp, mqt, mvt, gates = _in_proj(
            x_all, mod5, l, stack(norm_mix), p["w_rows"], p["w_feat"], p["b_gates"], p["qn"], p["kn"],
            cos_t, sa_t, sb_t, ctx_tiles)
        att = _attention(q, k, v, ctx_len, with_ctx=not final)
        hf, hb = _mlstm(mqt, mk, mvt, gates, tri, ctx_len)
        pool = _pool(pp, inv_cnt, p["pool_bd"], stack(pool_scale), l, ctx_len)
        x_all = _mix_ffn(x_all, att, hf, hb, mo, pool, mod5, l, stack(ml_norm), p["w_out"], stack(norm_ffn),
                         p["wg"], p["wu"], p["wd"], final_norm.reshape(1, -1), ctx_tiles, final)
    return x_all
```

```python
import functools

import jax
import jax.numpy as jnp
import numpy as np
from jax import lax
from jax.experimental import pallas as pl
from jax.experimental.pallas import tpu as pltpu

F32 = jnp.float32
BF16 = jnp.bfloat16

D_MODEL = 1024
HEAD_DIM = 64
GRID_W = 64
ATT_WIDTH = 512
KV_WIDTH = 128
ML_WIDTH = 256
POOL_WIDTH = 256
ATT_HEADS = 8
ML_HEADS = 4
N_GATES = 16
POOL_WINDOWS = (2, 4, 8, 16)
POOL_GROUP_DIM = 64
D_FF = 2816
CHUNK = 128
ROPE_THETA = 10000.0
EPS = 1e-6

LANES = 128
ROW_TILE = 256
MOD_ROWS = 24
POOL_PAD = 16
VMEM_LIMIT = 52 * 1024 * 1024

OFF_Q, OFF_K, OFF_V, OFF_MK, OFF_MO, OFF_PP = 0, 512, 640, 768, 1024, 1280
IN_COLS = OFF_PP + POOL_WIDTH
ROW_MQ, ROW_MV, ROW_G = 0, 256, 512
T_ROWS = ROW_G + N_GATES
Q_SCALE = HEAD_DIM ** -0.5 * float(np.log2(np.e))
VA_ROWS = 80
ATT_ROWS = 4
ATT_GROUP = 2
MLSTM_WAVE = 4
MLSTM_ROWS = 16
STEP_ROWS = 2
INPROJ_ROWS = 4
FF_SPLITS = ((0, 1536), (1536, D_FF))

HEAD_PERM = np.concatenate(
    [np.concatenate([np.arange(c * 64, (c + 1) * 64), np.arange((4 + c) * 64, (5 + c) * 64)]) for c in range(4)])


def _sigmoid(x):
    return 1.0 / (1.0 + jnp.exp(-x))


def _split3(a):
    hi = a.astype(BF16)
    r1 = a - hi.astype(F32)
    mid = r1.astype(BF16)
    lo = (r1 - mid.astype(F32)).astype(BF16)
    return hi, mid, lo


def _mod_kernel(c_ref, w_ref, b_ref, o_ref):
    c = c_ref[...]
    s = c * _sigmoid(c)
    s_hi, s_mid, _ = _split3(s)
    w = w_ref[...]
    w_hi = w.astype(BF16)
    w_mid = (w - w_hi.astype(F32)).astype(BF16)
    dot = functools.partial(jnp.dot, preferred_element_type=F32)
    acc = dot(s_mid, w_hi) + dot(s_hi, w_mid)
    acc = acc + dot(s_hi, w_hi)
    o_ref[...] = acc + b_ref[...]


def _modulation(c_all, w_ada, b_ada):
    depth = w_ada.shape[0]
    tn = 1024
    return pl.pallas_call(
        _mod_kernel,
        out_shape=jax.ShapeDtypeStruct((depth, MOD_ROWS, 6 * D_MODEL), F32),
        grid=(depth, 6 * D_MODEL // tn),
        in_specs=[
            pl.BlockSpec((MOD_ROWS, D_MODEL), lambda l, j: (0, 0)),
            pl.BlockSpec((None, D_MODEL, tn), lambda l, j: (l, 0, j)),
            pl.BlockSpec((None, 1, tn), lambda l, j: (l, 0, j)),
        ],
        out_specs=pl.BlockSpec((None, MOD_ROWS, tn), lambda l, j: (l, 0, j)),
        name="adaln_mod",
    )(c_all, w_ada, b_ada.reshape(depth, 1, 6 * D_MODEL))


def _segment_ms(x, lo_mask):
    x2 = x * x
    s_lo = jnp.sum(jnp.where(lo_mask, x2, 0.0), axis=-1, keepdims=True)
    s_hi = jnp.sum(jnp.where(lo_mask, 0.0, x2), axis=-1, keepdims=True)
    return jnp.where(lo_mask, s_lo, s_hi) * (1.0 / HEAD_DIM)


def _inproj_kernel(*refs, ctx_tiles, split):
    if split:
        ctx_ref, x_ref = refs[0:2]
        refs = refs[2:]
        from_ctx = pl.program_id(1) < ctx_tiles
        load = lambda bb: jnp.where(from_ctx, ctx_ref[bb], x_ref[bb])
    else:
        x_ref = refs[0]
        refs = refs[1:]
        load = lambda bb: x_ref[bb]
    (mod_ref, nm_ref, w_ref, wt_ref, bg_ref, qn_ref, kn_ref, cos_ref, sa_ref, sb_ref,
     q_ref, k_ref, v_ref, mk_ref, mo_ref, pp_ref, mqt_ref, mvt_ref, g_ref) = refs
    nb, rows = q_ref.shape[0], q_ref.shape[1]

    hs = []
    for bb in range(nb):
        x = load(bb)
        ms = jnp.mean(x * x, axis=-1, keepdims=True)
        gain = nm_ref[...] * (1.0 + mod_ref[bb, 1])
        hs.append((x * lax.rsqrt(ms + EPS) * gain + mod_ref[bb, 0]).astype(BF16))
    accs = [jnp.dot(h, w_ref[...], preferred_element_type=F32) for h in hs]
    accs_t = [lax.dot_general(wt_ref[...], h, (((1,), (1,)), ((), ())), preferred_element_type=F32)
              for h in hs]

    lo_mask = lax.broadcasted_iota(jnp.int32, (rows, LANES), 1) < HEAD_DIM
    cos, sa, sb = cos_ref[...], sa_ref[...], sb_ref[...]

    def norm_rope(xc, gain):
        yc = xc * lax.rsqrt(_segment_ms(xc, lo_mask) + EPS) * gain
        return yc * cos + pltpu.roll(yc, LANES - 16, 1) * sa + pltpu.roll(yc, 16, 1) * sb

    for bb in range(nb):
        acc, acc_t = accs[bb], accs_t[bb]
        blk = lambda off, width: acc[:, off:off + width]
        for c in range(ATT_WIDTH // LANES):
            qc = norm_rope(blk(OFF_Q + c * LANES, LANES), qn_ref[...])
            q_ref[bb, :, c * LANES:(c + 1) * LANES] = (qc * Q_SCALE).astype(BF16)
        k_ref[bb] = norm_rope(blk(OFF_K, LANES), kn_ref[...]).astype(BF16)
        v_ref[bb] = blk(OFF_V, KV_WIDTH).astype(BF16)
        mk_ref[bb] = blk(OFF_MK, ML_WIDTH).astype(BF16)
        mo_ref[bb] = blk(OFF_MO, ML_WIDTH).astype(BF16)
        pp_ref[bb] = blk(OFF_PP, POOL_WIDTH).astype(BF16)
        mqt_ref[bb] = acc_t[ROW_MQ:ROW_MQ + ML_WIDTH, :].astype(BF16)
        mvt_ref[bb] = acc_t[ROW_MV:ROW_MV + ML_WIDTH, :].astype(BF16)
        g_ref[bb] = acc_t[ROW_G:ROW_G + N_GATES, :] + bg_ref[...]


def _mod_spec(layer, nb, ctx_tiles, ctx_block, first=0):
    def idx(b, i):
        return (layer, jnp.where(i + first < ctx_tiles, ctx_block, b), 0, 0, 0)
    return pl.BlockSpec((None, nb, 6, 1, D_MODEL), idx)


def _token_specs(x_in, nb, ctx_tiles, first=0):
    if isinstance(x_in, tuple):
        ctx, x = x_in
        specs = [pl.BlockSpec((nb, ROW_TILE, D_MODEL), lambda b, i: (b, jnp.minimum(i, ctx_tiles - 1), 0)),
                 pl.BlockSpec((nb, ROW_TILE, D_MODEL), lambda b, i: (b, jnp.maximum(i - ctx_tiles, 0), 0))]
        return specs, [ctx, x], ctx.shape[0], ctx.shape[1] + x.shape[1]
    specs = [pl.BlockSpec((nb, ROW_TILE, D_MODEL), lambda b, i: (b, i + first, 0))]
    return specs, [x_in], x_in.shape[0], x_in.shape[1]


def _in_proj(x_in, mod5, layer, norm_mix, w_in_r, w_in_t, b_gates_b, qn, kn, cos_t, sa_t, sb_t, ctx_tiles):
    nb = INPROJ_ROWS
    x_specs, x_ops, B, TA = _token_specs(x_in, nb, ctx_tiles)
    nt = TA // ROW_TILE
    row = lambda w: pl.BlockSpec((nb, ROW_TILE, w), lambda b, i: (b, i, 0))
    col = lambda r: pl.BlockSpec((nb, r, ROW_TILE), lambda b, i: (b, 0, i))
    vec = lambda w: pl.BlockSpec((None, 1, w), lambda b, i: (layer, 0, 0))
    full = lambda r, c: pl.BlockSpec((None, r, c), lambda b, i: (layer, 0, 0))
    tab = pl.BlockSpec((ROW_TILE, LANES), lambda b, i: (i, 0))
    row_widths = (ATT_WIDTH, KV_WIDTH, KV_WIDTH, ML_WIDTH, ML_WIDTH, POOL_WIDTH)
    col_heights = (ML_WIDTH, ML_WIDTH)
    out_shapes = [jax.ShapeDtypeStruct((B, TA, w), BF16) for w in row_widths]
    out_shapes += [jax.ShapeDtypeStruct((B, r, TA), BF16) for r in col_heights]
    out_shapes.append(jax.ShapeDtypeStruct((B, N_GATES, TA), F32))
    out_specs = [row(w) for w in row_widths] + [col(r) for r in col_heights] + [col(N_GATES)]
    kern = functools.partial(_inproj_kernel, ctx_tiles=ctx_tiles, split=len(x_ops) == 2)
    return pl.pallas_call(
        kern,
        out_shape=out_shapes,
        grid=(B // nb, nt),
        in_specs=x_specs + [_mod_spec(layer, nb, ctx_tiles, B // nb), vec(D_MODEL), full(D_MODEL, IN_COLS),
                            full(T_ROWS, D_MODEL), full(N_GATES, ROW_TILE), vec(LANES), vec(LANES), tab, tab, tab],
        out_specs=out_specs,
        compiler_params=pltpu.CompilerParams(vmem_limit_bytes=VMEM_LIMIT),
        name="in_proj",
    )(*x_ops, mod5, norm_mix, w_in_r, w_in_t, b_gates_b, qn, kn, cos_t, sa_t, sb_t)


def _attn_kernel(q_ref, k_ref, v_ref, o_ref, k0_ref, k1_ref, va_ref, *, ctx_len, first_tile):
    i = pl.program_id(1)
    nb, total = k_ref.shape[0], k_ref.shape[1]

    @pl.when(i == 0)
    def _():
        for bb in range(nb):
            k = k_ref[bb]
            lo = lax.broadcasted_iota(jnp.int32, k.shape, 1) < HEAD_DIM
            zero = jnp.zeros_like(k)
            k0_ref[bb] = jnp.where(lo, k, zero)
            k1_ref[bb] = jnp.where(lo, zero, k)
            va_ref[bb, :, 0:LANES] = v_ref[bb]
            va_ref[bb, :, LANES:2 * LANES] = jnp.ones((total, LANES), BF16)

    rows = q_ref.shape[1]
    lo_mask = lax.broadcasted_iota(jnp.int32, (rows, LANES), 1) < HEAD_DIM

    def block(nk):
        chunks = list(range(ATT_WIDTH // LANES))
        for bb in range(nb):
            for g in range(0, len(chunks), ATT_GROUP):
                cs = chunks[g:g + ATT_GROUP]
                ss = [lax.dot_general(q_ref[bb, :, c * LANES:(c + 1) * LANES], kr[bb, 0:nk, :], (((1,), (1,)), ((), ())),
                                      preferred_element_type=F32) for c in cs for kr in (k0_ref, k1_ref)]
                ms = [jnp.max(s, axis=-1, keepdims=True) for s in ss]
                ps = [jnp.exp2(s - m).astype(BF16) for s, m in zip(ss, ms)]
                acs = [jnp.dot(p, va_ref[bb, 0:nk, :], preferred_element_type=F32) for p in ps]
                outs = [a[:, 0:LANES] / a[:, LANES:2 * LANES] for a in acs]
                for n, c in enumerate(cs):
                    o_ref[bb, :, c * LANES:(c + 1) * LANES] = jnp.where(lo_mask, outs[2 * n], outs[2 * n + 1]).astype(BF16)

    if first_tile == 0:
        pl.when(i == 0)(lambda: block(ctx_len))
        pl.when(i > 0)(lambda: block(total))
    else:
        block(total)


def _attention(q, k, v, ctx_len, with_ctx):
    B, TA, _ = q.shape
    first = 0 if with_ctx else ctx_len // ROW_TILE
    nt = TA // ROW_TILE - first
    nb = ATT_ROWS if B % ATT_ROWS == 0 else 1
    kern = functools.partial(_attn_kernel, ctx_len=ctx_len, first_tile=first)
    return pl.pallas_call(
        kern,
        out_shape=jax.ShapeDtypeStruct((B, nt * ROW_TILE, ATT_WIDTH), BF16),
        grid=(B // nb, nt),
        in_specs=[pl.BlockSpec((nb, ROW_TILE, ATT_WIDTH), lambda b, i: (b, i + first, 0)),
                  pl.BlockSpec((nb, TA, KV_WIDTH), lambda b, i: (b, 0, 0)),
                  pl.BlockSpec((nb, TA, KV_WIDTH), lambda b, i: (b, 0, 0))],
        out_specs=pl.BlockSpec((nb, ROW_TILE, ATT_WIDTH), lambda b, i: (b, i, 0)),
        scratch_shapes=[pltpu.VMEM((nb, TA, KV_WIDTH), BF16), pltpu.VMEM((nb, TA, KV_WIDTH), BF16),
                        pltpu.VMEM((nb, TA, 2 * LANES), BF16)],
        compiler_params=pltpu.CompilerParams(vmem_limit_bytes=VMEM_LIMIT),
        name="attention",
    )(q, k, v)


def _mlstm_kernel(qf_ref, kf_ref, vf_ref, gf_ref, qb_ref, kb_ref, vb_ref, gb_ref, tri_ref, hf_ref, hb_ref,
                  st_ref, m_ref, r_ref):
    i = pl.program_id(1)

    @pl.when(i == 0)
    def _():
        st_ref[...] = jnp.zeros_like(st_ref)
        m_ref[...] = jnp.zeros_like(m_ref)
        r_ref[...] = jnp.zeros_like(r_ref)

    L = CHUNK
    s_idx = lax.broadcasted_iota(jnp.int32, (L, L), 0)
    t_idx = lax.broadcasted_iota(jnp.int32, (L, L), 1)
    first_head_rows = lax.broadcasted_iota(jnp.int32, (LANES, L), 0) < HEAD_DIM
    ones_blk = jnp.where(lax.broadcasted_iota(jnp.int32, (VA_ROWS - HEAD_DIM, L), 0) == 0, 1.0, 0.0).astype(BF16)

    dirs = ((qf_ref, kf_ref, vf_ref, gf_ref, hf_ref), (qb_ref, kb_ref, vb_ref, gb_ref, hb_ref))
    chains = [divmod(bd, 2) for bd in range(2 * hf_ref.shape[0])]
    refs = [tuple(ref.at[bb] for ref in dirs[d]) for bb, d in chains]

    heads = range(ML_HEADS)
    qm, kp, va, st, s_raw, a1 = {}, {}, {}, {}, {}, {}
    b8, inter8, a8, wk8, ct = {}, {}, {}, {}, {}
    dmat, m_t, s_t, a2, upd, hts = {}, {}, {}, {}, {}, {}

    def stage_operands(bds):
        for bd in bds:
            q_r, k_r, v_r = refs[bd][0:3]
            for hd in heads:
                pair, second = divmod(hd, 2)
                kp[bd, hd] = k_r[:, pair * LANES:(pair + 1) * LANES]
                q_pair = q_r[pair * LANES:(pair + 1) * LANES, :]
                keep = jnp.logical_not(first_head_rows) if second else first_head_rows
                qm[bd, hd] = jnp.where(keep, q_pair, jnp.zeros_like(q_pair))
                va[bd, hd] = jnp.concatenate([v_r[hd * HEAD_DIM:(hd + 1) * HEAD_DIM, :], ones_blk], axis=0)
                st[bd, hd] = st_ref[bd * ML_HEADS + hd]
                s_raw[bd, hd] = jnp.dot(kp[bd, hd], qm[bd, hd], preferred_element_type=F32)
                a1[bd, hd] = jnp.dot(st[bd, hd].astype(BF16), qm[bd, hd], preferred_element_type=F32)

    def stage_gates(bds):
        g8 = {bd: refs[bd][3][8 * chains[bd][1]:8 * chains[bd][1] + 8, :] for bd in bds}
        lf8 = {bd: jnp.minimum(g8[bd], 0.0) - jnp.log1p(jnp.exp(-jnp.abs(g8[bd]))) for bd in bds}
        parts = {bd: jnp.concatenate([p.astype(F32) for p in _split3(lf8[bd])], axis=0).astype(BF16) for bd in bds}
        cum = {bd: jnp.dot(parts[bd], tri_ref[chains[bd][1]], preferred_element_type=F32) for bd in bds}
        cum = {bd: c[16:24] + c[8:16] + c[0:8] for bd, c in cum.items()}
        li8 = {bd: pltpu.roll(g8[bd], ML_HEADS, 0) for bd in bds}
        for bd in bds:
            b8[bd] = cum[bd][:, 0:L]
            r_ref[bd, 0:8, :] = li8[bd] - b8[bd]
        for bd in bds:
            ct[bd] = r_ref[bd].T
        for bd in bds:
            btot = cum[bd][:, L:2 * L]
            m_prev = m_ref[bd]
            gg = btot - b8[bd] + li8[bd]
            m_new = jnp.maximum(btot + m_prev, jnp.max(gg, axis=-1, keepdims=True))
            a8[bd] = jnp.exp(btot + m_prev - m_new)
            wk8[bd] = jnp.exp(gg - m_new)
            inter8[bd] = b8[bd] + m_prev
            m_ref[bd] = m_new

    def stage_decay(bds):
        for bd in bds:
            causal = (s_idx <= t_idx) if chains[bd][1] == 0 else (s_idx >= t_idx)
            for hd in heads:
                r = ML_HEADS + hd
                dmat[bd, hd] = jnp.where(causal, ct[bd][:, r:r + 1] + b8[bd][r:r + 1], -jnp.inf)
                m_t[bd, hd] = jnp.maximum(inter8[bd][r:r + 1], jnp.max(dmat[bd, hd], axis=0, keepdims=True))

    def stage_scores(bds):
        for bd in bds:
            for hd in heads:
                s_t[bd, hd] = (s_raw[bd, hd] * jnp.exp(dmat[bd, hd] - m_t[bd, hd])).astype(BF16)

    def stage_values(bds):
        for bd in bds:
            for hd in heads:
                r = ML_HEADS + hd
                a2[bd, hd] = jnp.dot(va[bd, hd], s_t[bd, hd], preferred_element_type=F32)
                vw = (va[bd, hd].astype(F32) * wk8[bd][r:r + 1]).astype(BF16)
                upd[bd, hd] = jnp.dot(vw, kp[bd, hd], preferred_element_type=F32)

    def stage_finish(bds):
        for bd in bds:
            for hd in heads:
                r = ML_HEADS + hd
                inter = inter8[bd][r:r + 1]
                nd = jnp.exp(inter - m_t[bd, hd]) * a1[bd, hd] + a2[bd, hd]
                den = nd[HEAD_DIM:HEAD_DIM + 1, :]
                hts[bd, hd] = nd[0:HEAD_DIM, :] / jnp.maximum(jnp.abs(den), jnp.exp(-m_t[bd, hd]))
                st_ref[bd * ML_HEADS + hd] = a8[bd][r:r + 1] * st[bd, hd] + upd[bd, hd]
        for bd in bds:
            refs[bd][4][...] = jnp.concatenate([hts[bd, hd] for hd in heads], axis=0).T.astype(BF16)

    stages = (stage_gates, stage_operands, stage_decay, stage_scores, stage_values, stage_finish)
    waves = [list(range(w, min(w + MLSTM_WAVE, len(chains)))) for w in range(0, len(chains), MLSTM_WAVE)]
    for step in range(len(stages) + len(waves) - 1):
        for w, bds in enumerate(waves):
            if 0 <= step - w < len(stages):
                stages[step - w](bds)


def _mlstm(mqt, mk, mvt, gates, tri, ctx_len):
    B, TA, _ = mk.shape
    nc = TA // CHUNK
    ncc = ctx_len // CHUNK

    def fwd(i):
        return i

    def bwd(i):
        return jnp.where(i < ncc, ncc - 1 - i, nc - 1 - (i - ncc))

    nb = MLSTM_ROWS if B % MLSTM_ROWS == 0 else 1
    tok = lambda f: pl.BlockSpec((nb, CHUNK, ML_WIDTH), lambda b, i: (b, f(i), 0))
    feat = lambda f, r: pl.BlockSpec((nb, r, CHUNK), lambda b, i: (b, 0, f(i)))
    n_units = 2 * nb * ML_HEADS
    return pl.pallas_call(
        _mlstm_kernel,
        out_shape=[jax.ShapeDtypeStruct((B, TA, ML_WIDTH), BF16)] * 2,
        grid=(B // nb, nc),
        in_specs=[feat(fwd, ML_WIDTH), tok(fwd), feat(fwd, ML_WIDTH), feat(fwd, N_GATES),
                  feat(bwd, ML_WIDTH), tok(bwd), feat(bwd, ML_WIDTH), feat(bwd, N_GATES),
                  pl.BlockSpec((2, CHUNK, 2 * CHUNK), lambda b, i: (0, 0, 0))],
        out_specs=[tok(fwd), tok(bwd)],
        scratch_shapes=[pltpu.VMEM((n_units, VA_ROWS, LANES), F32), pltpu.VMEM((2 * nb, 8, LANES), F32),
                        pltpu.VMEM((2 * nb, CHUNK, LANES), F32)],
        compiler_params=pltpu.CompilerParams(vmem_limit_bytes=VMEM_LIMIT),
        name="mlstm",
    )(mqt, mk, mvt, gates, mqt, mk, mvt, gates, tri)


def _pool_kernel(pp_ref, ic_ref, bd_ref, ps_ref, o_ref, p_ref, s2_ref, s4_ref, s8_ref, *, ctx_len):
    total = pp_ref.shape[0]
    seqs = ((0, ctx_len, POOL_PAD), (ctx_len, total - ctx_len, 2 * POOL_PAD + ctx_len))
    rows = p_ref.shape[0]
    upper = slice(LANES, 2 * LANES)
    for ref in (p_ref, s2_ref, s4_ref, s8_ref):
        ref[...] = jnp.zeros_like(ref)
    for src, n, dst in seqs:
        p_ref[dst:dst + n, :] = pp_ref[src:src + n, :].astype(F32)
    lo, m = 8, rows - 16
    s2_ref[lo:lo + m, :] = p_ref[lo - 1:lo - 1 + m, :] + p_ref[lo:lo + m, :]
    s4_ref[lo:lo + m, :] = s2_ref[lo - 1:lo - 1 + m, :] + s2_ref[lo + 1:lo + 1 + m, :]
    s8_ref[lo:lo + m, :] = s4_ref[lo - 2:lo - 2 + m, upper] + s4_ref[lo + 2:lo + 2 + m, upper]
    for src, n, dst in seqs:
        first_group = lax.broadcasted_iota(jnp.int32, (n, LANES), 1) < POOL_GROUP_DIM
        s16 = s8_ref[dst - 4:dst - 4 + n, :] + s8_ref[dst + 4:dst + 4 + n, :]
        wsum = jnp.concatenate([jnp.where(first_group, s2_ref[dst:dst + n, 0:LANES], s4_ref[dst:dst + n, 0:LANES]),
                                jnp.where(first_group, s8_ref[dst:dst + n, :], s16)], axis=-1)
        p = wsum * ic_ref[src:src + n, :] - p_ref[dst:dst + n, :]
        y = jnp.dot(p.astype(BF16), bd_ref[...], preferred_element_type=F32) * ps_ref[...]
        o_ref[src:src + n, :] = y.astype(BF16)


def _inverse_window_counts(ctx_len, t_latent):
    half = np.repeat(np.array(POOL_WINDOWS) // 2, POOL_GROUP_DIM)[None, :]
    tables = []
    for n in (ctx_len, t_latent):
        t = np.arange(n)[:, None]
        tables.append(1.0 / (np.minimum(t + half, n) - np.maximum(t - half, 0)))
    return jnp.asarray(np.concatenate(tables, axis=0), dtype=F32)


def _pool(pp, inv_cnt, pool_bd, pool_scale, layer, ctx_len):
    B, TA, _ = pp.shape
    rows = TA + 3 * POOL_PAD
    kern = functools.partial(_pool_kernel, ctx_len=ctx_len)
    full = pl.BlockSpec((None, TA, POOL_WIDTH), lambda b: (b, 0, 0))
    return pl.pallas_call(
        kern,
        out_shape=jax.ShapeDtypeStruct((B, TA, POOL_WIDTH), BF16),
        grid=(B,),
        in_specs=[full, pl.BlockSpec((TA, POOL_WIDTH), lambda b: (0, 0)),
                  pl.BlockSpec((None, POOL_WIDTH, POOL_WIDTH), lambda b: (layer, 0, 0)),
                  pl.BlockSpec((None, 1, POOL_WIDTH), lambda b: (layer, 0, 0))],
        out_specs=full,
        scratch_shapes=[pltpu.VMEM((rows, POOL_WIDTH), F32)] * 3 + [pltpu.VMEM((rows, LANES), F32)],
        compiler_params=pltpu.CompilerParams(vmem_limit_bytes=VMEM_LIMIT),
        name="pool_mix",
    )(pp, inv_cnt, pool_bd, pool_scale)


def _mixffn_kernel(*refs, ctx_tiles, split, final):
    if split:
        ctx_ref, x_ref = refs[0:2]
        refs = refs[2:]
        from_ctx = pl.program_id(1) < ctx_tiles
        load = lambda bb: jnp.where(from_ctx, ctx_ref[bb], x_ref[bb])
    else:
        x_ref = refs[0]
        refs = refs[1:]
        load = lambda bb: x_ref[bb]
    (att_ref, hf_ref, hb_ref, mo_ref, pool_ref, mod_ref, mln_ref, wo_ref, nf_ref,
     wg_ref, wu_ref, wd_ref, fn_ref, o_ref) = refs
    nb, rows = o_ref.shape[0], o_ref.shape[1]
    lo_mask = lax.broadcasted_iota(jnp.int32, (rows, LANES), 1) < HEAD_DIM

    cats = []
    for bb in range(nb):
        ml = []
        for c in range(ML_WIDTH // LANES):
            sl = slice(c * LANES, (c + 1) * LANES)
            hs = hf_ref[bb, :, sl].astype(F32) + hb_ref[bb, :, sl].astype(F32)
            hn = hs * lax.rsqrt(_segment_ms(hs, lo_mask) + EPS) * mln_ref[:, sl]
            ml.append((hn * _sigmoid(mo_ref[bb, :, sl].astype(F32))).astype(BF16))
        cats.append(jnp.concatenate([att_ref[bb]] + ml + [pool_ref[bb]], axis=-1))
    mixes = [jnp.dot(cat, wo_ref[...], preferred_element_type=F32) for cat in cats]

    x1s, h2s = [], []
    for bb in range(nb):
        x1 = load(bb) + mod_ref[bb, 2] * mixes[bb]
        ms = jnp.mean(x1 * x1, axis=-1, keepdims=True)
        gain = nf_ref[...] * (1.0 + mod_ref[bb, 4])
        h2s.append((x1 * lax.rsqrt(ms + EPS) * gain + mod_ref[bb, 3]).astype(BF16))
        x1s.append(x1)
    downs = []
    for h2 in h2s:
        down = None
        for lo, hi in FF_SPLITS:
            gate = jnp.dot(h2, wg_ref[:, lo:hi], preferred_element_type=F32)
            up = jnp.dot(h2, wu_ref[:, lo:hi], preferred_element_type=F32)
            act = (gate * _sigmoid(gate) * up).astype(BF16)
            part = jnp.dot(act, wd_ref[lo:hi, :], preferred_element_type=F32)
            down = part if down is None else down + part
        downs.append(down)
    for bb in range(nb):
        out = x1s[bb] + mod_ref[bb, 5] * downs[bb]
        if final:
            ms = jnp.mean(out * out, axis=-1, keepdims=True)
            out = out * lax.rsqrt(ms + EPS) * fn_ref[...]
        o_ref[bb] = out


def _mix_ffn(x_in, att, hf, hb, mo, pool, mod5, layer, ml_norm, w_out_r, norm_ffn, wg, wu, wd, final_norm,
             ctx_tiles, final):
    nb = STEP_ROWS
    first = ctx_tiles if final else 0
    x_specs, x_ops, B, TA = _token_specs(x_in, nb, ctx_tiles, first)
    nt = TA // ROW_TILE - first
    row = lambda w: pl.BlockSpec((nb, ROW_TILE, w), lambda b, i: (b, i + first, 0))
    vec = lambda w: pl.BlockSpec((None, 1, w), lambda b, i: (layer, 0, 0))
    wspec = lambda r, c: pl.BlockSpec((None, r, c), lambda b, i: (layer, 0, 0), pipeline_mode=pl.Buffered(1))

    kern = functools.partial(_mixffn_kernel, ctx_tiles=ctx_tiles, split=len(x_ops) == 2, final=final)
    return pl.pallas_call(
        kern,
        out_shape=jax.ShapeDtypeStruct((B, nt * ROW_TILE, D_MODEL), F32),
        grid=(B // nb, nt),
        in_specs=x_specs + [pl.BlockSpec((nb, ROW_TILE, ATT_WIDTH), lambda b, i: (b, i, 0)),
                            row(ML_WIDTH), row(ML_WIDTH), row(ML_WIDTH), row(POOL_WIDTH),
                            _mod_spec(layer, nb, ctx_tiles, B // nb, first), vec(ML_WIDTH),
                            wspec(D_MODEL, D_MODEL), vec(D_MODEL), wspec(D_MODEL, D_FF), wspec(D_MODEL, D_FF),
                            wspec(D_FF, D_MODEL), pl.BlockSpec((1, D_MODEL), lambda b, i: (0, 0))],
        out_specs=pl.BlockSpec((nb, ROW_TILE, D_MODEL), lambda b, i: (b, i, 0)),
        compiler_params=pltpu.CompilerParams(vmem_limit_bytes=VMEM_LIMIT),
        name="mix_ffn",
    )(*x_ops, att, hf, hb, mo, pool, mod5, ml_norm, w_out_r, norm_ffn, wg, wu, wd, final_norm)


def _rope_tables(t_latent, ctx_len):
    f32 = np.float32
    half = HEAD_DIM // 2
    inv_freq = (f32(1.0) / np.power(f32(ROPE_THETA), np.arange(0, half, 2, dtype=f32) / f32(half))).astype(f32)
    rows = t_latent // GRID_W
    row = np.repeat(np.arange(rows, dtype=f32), GRID_W)
    col = np.tile(np.arange(GRID_W, dtype=f32), rows)
    a_row = row[:, None] * inv_freq
    a_col = col[:, None] * inv_freq
    ang = np.concatenate([a_row, a_row, a_col, a_col], -1).astype(f32)
    cos, sin = np.cos(ang), np.sin(ang)
    first_half = (np.arange(HEAD_DIM) % 32) < 16
    sa = np.where(first_half, -sin, 0.0)
    sb = np.where(first_half, 0.0, sin)
    pad = lambda a, fill: jnp.asarray(np.tile(np.concatenate([np.full((ctx_len, HEAD_DIM), fill, f32), a], 0), (1, 2)),
                                      dtype=F32)
    return pad(cos, 1.0), pad(sa, 0.0), pad(sb, 0.0)


_IN = np.arange(2064)
COLS_TOKEN = np.concatenate([HEAD_PERM, _IN[512:768], _IN[1024:1280], _IN[1536:1792], _IN[1808:2064]])
COLS_FEATURE = np.concatenate([_IN[768:1024], _IN[1280:1536], _IN[1792:1808]])
IN_SCALE = np.where((_IN >= 1024) & (_IN < 1280), HEAD_DIM ** -0.5, 1.0).astype(np.float32)
ROWS_OUT = np.concatenate([HEAD_PERM, np.arange(ATT_WIDTH, D_MODEL)])


def _take_runs(a, index, axis):
    cuts = [0] + [n for n in range(1, len(index)) if index[n] != index[n - 1] + 1] + [len(index)]
    runs = [lax.slice_in_dim(a, int(index[s]), int(index[e - 1]) + 1, axis=axis) for s, e in zip(cuts[:-1], cuts[1:])]
    return jnp.concatenate(runs, axis=axis)


def _prepare_params(w_in, b_gates, q_norm, k_norm, pool_w, w_out, w_ffn_gate, w_ffn_up, w_ffn_down):
    depth = w_in.shape[0]
    groups = len(POOL_WINDOWS)
    eye = jnp.eye(groups, dtype=F32)
    w_in_b = (w_in * IN_SCALE).astype(BF16)
    return dict(
        w_rows=_take_runs(w_in_b, COLS_TOKEN, 2),
        w_feat=jnp.swapaxes(_take_runs(w_in_b, COLS_FEATURE, 2), 1, 2),
        b_gates=jnp.broadcast_to(b_gates[:, :, None], (depth, N_GATES, ROW_TILE)),
        qn=jnp.tile(q_norm, (1, 2))[:, None, :], kn=jnp.tile(k_norm, (1, 2))[:, None, :],
        pool_bd=(pool_w[:, :, :, None, :] * eye[None, :, None, :, None]).reshape(depth, POOL_WIDTH, POOL_WIDTH).astype(BF16),
        w_out=_take_runs(w_out.astype(BF16), ROWS_OUT, 1),
        wg=w_ffn_gate.astype(BF16), wu=w_ffn_up.astype(BF16), wd=w_ffn_down.astype(BF16))


def _cumsum_matrices():
    s = np.arange(CHUNK)[:, None]
    t = np.arange(CHUNK)[None, :]
    ones = np.ones((CHUNK, CHUNK), np.float32)
    prefix = np.concatenate([(s <= t).astype(np.float32), ones], axis=1)
    suffix = np.concatenate([(s >= t).astype(np.float32), ones], axis=1)
    return jnp.asarray(np.stack([prefix, suffix]), dtype=BF16)


def kernel(x, c, ctx, c_ctx, w_ada, b_ada, norm_mix, w_in, b_gates, q_norm, k_norm, ml_norm, pool_w, pool_scale,
           w_out, norm_ffn, w_ffn_gate, w_ffn_up, w_ffn_down, final_norm):
    B, T, _ = x.shape
    ctx_len = ctx.shape[1]
    depth = w_ada.shape[0]
    n_rep = max(STEP_ROWS, INPROJ_ROWS)
    assert ctx_len % ROW_TILE == 0 and T % ROW_TILE == 0 and B % n_rep == 0 and B + n_rep <= MOD_ROWS
    ctx_tiles = ctx_len // ROW_TILE

    c_rep = jnp.broadcast_to(c_ctx[None, :], (n_rep, D_MODEL))
    c_all = jnp.concatenate([c, c_rep, jnp.zeros((MOD_ROWS - B - n_rep, D_MODEL), F32)], 0)
    mod = _modulation(c_all, w_ada, b_ada)
    mod5 = mod.reshape(depth, MOD_ROWS, 6, 1, D_MODEL)
    cos_t, sa_t, sb_t = _rope_tables(T, ctx_len)
    tri = _cumsum_matrices()
    inv_cnt = _inverse_window_counts(ctx_len, T)
    p = _prepare_params(w_in, b_gates, q_norm, k_norm, pool_w, w_out, w_ffn_gate, w_ffn_up, w_ffn_down)
    stack = lambda a: a[:, None, :]

    x_all = (ctx, x)
    for l in range(depth):
        final = l == depth - 1
        q, k, v, mk, mo, pp, mqt, mvt, gates = _in_proj(
            x_all, mod5, l, stack(norm_mix), p["w_rows"], p["w_feat"], p["b_gates"], p["qn"], p["kn"],
            cos_t, sa_t, sb_t, ctx_tiles)
        att = _attention(q, k, v, ctx_len, with_ctx=not final)
        hf, hb = _mlstm(mqt, mk, mvt, gates, tri, ctx_len)
        pool = _pool(pp, inv_cnt, p["pool_bd"], stack(pool_scale), l, ctx_len)
        x_all = _mix_ffn(x_all, att, hf, hb, mo, pool, mod5, l, stack(ml_norm), p["w_out"], stack(norm_ffn),
                         p["wg"], p["wu"], p["wd"], final_norm.reshape(1, -1), ctx_tiles, final)
    return x_all
```

```python
import functools

import jax
import jax.numpy as jnp
import numpy as np
from jax import lax
from jax.experimental import pallas as pl
from jax.experimental.pallas import tpu as pltpu

F32 = jnp.float32
BF16 = jnp.bfloat16

D_MODEL = 1024
HEAD_DIM = 64
GRID_W = 64
ATT_WIDTH = 512
KV_WIDTH = 128
ML_WIDTH = 256
POOL_WIDTH = 256
ATT_HEADS = 8
ML_HEADS = 4
N_GATES = 16
POOL_WINDOWS = (2, 4, 8, 16)
POOL_GROUP_DIM = 64
D_FF = 2816
CHUNK = 128
ROPE_THETA = 10000.0
EPS = 1e-6

LANES = 128
ROW_TILE = 256
MOD_ROWS = 24
POOL_PAD = 16
VMEM_LIMIT = 52 * 1024 * 1024

OFF_Q, OFF_K, OFF_V, OFF_MK, OFF_MO, OFF_PP = 0, 512, 640, 768, 1024, 1280
IN_COLS = OFF_PP + POOL_WIDTH
ROW_MQ, ROW_MV, ROW_G = 0, 256, 512
T_ROWS = ROW_G + N_GATES
Q_SCALE = HEAD_DIM ** -0.5 * float(np.log2(np.e))
VA_ROWS = 80
ATT_ROWS = 4
ATT_GROUP = 2
MLSTM_WAVE = 4
MLSTM_ROWS = 16
STEP_ROWS = 2
INPROJ_ROWS = 4
FF_SPLITS = ((0, 1536), (1536, D_FF))

HEAD_PERM = np.concatenate(
    [np.concatenate([np.arange(c * 64, (c + 1) * 64), np.arange((4 + c) * 64, (5 + c) * 64)]) for c in range(4)])


def _sigmoid(x):
    return 1.0 / (1.0 + jnp.exp(-x))


def _split3(a):
    hi = a.astype(BF16)
    r1 = a - hi.astype(F32)
    mid = r1.astype(BF16)
    lo = (r1 - mid.astype(F32)).astype(BF16)
    return hi, mid, lo


def _mod_kernel(c_ref, w_ref, b_ref, o_ref):
    c = c_ref[...]
    s = c * _sigmoid(c)
    s_hi, s_mid, _ = _split3(s)
    w = w_ref[...]
    w_hi = w.astype(BF16)
    w_mid = (w - w_hi.astype(F32)).astype(BF16)
    dot = functools.partial(jnp.dot, preferred_element_type=F32)
    acc = dot(s_mid, w_hi) + dot(s_hi, w_mid)
    acc = acc + dot(s_hi, w_hi)
    o_ref[...] = acc + b_ref[...]


def _modulation(c_all, w_ada, b_ada):
    depth = w_ada.shape[0]
    tn = 1024
    return pl.pallas_call(
        _mod_kernel,
        out_shape=jax.ShapeDtypeStruct((depth, MOD_ROWS, 6 * D_MODEL), F32),
        grid=(depth, 6 * D_MODEL // tn),
        in_specs=[
            pl.BlockSpec((MOD_ROWS, D_MODEL), lambda l, j: (0, 0)),
            pl.BlockSpec((None, D_MODEL, tn), lambda l, j: (l, 0, j)),
            pl.BlockSpec((None, 1, tn), lambda l, j: (l, 0, j)),
        ],
        out_specs=pl.BlockSpec((None, MOD_ROWS, tn), lambda l, j: (l, 0, j)),
        name="adaln_mod",
    )(c_all, w_ada, b_ada.reshape(depth, 1, 6 * D_MODEL))


def _segment_ms(x, lo_mask):
    x2 = x * x
    s_lo = jnp.sum(jnp.where(lo_mask, x2, 0.0), axis=-1, keepdims=True)
    s_hi = jnp.sum(jnp.where(lo_mask, 0.0, x2), axis=-1, keepdims=True)
    return jnp.where(lo_mask, s_lo, s_hi) * (1.0 / HEAD_DIM)


def _inproj_kernel(*refs, ctx_tiles, split):
    if split:
        ctx_ref, x_ref = refs[0:2]
        refs = refs[2:]
        from_ctx = pl.program_id(1) < ctx_tiles
        load = lambda bb: jnp.where(from_ctx, ctx_ref[bb], x_ref[bb])
    else:
        x_ref = refs[0]
        refs = refs[1:]
        load = lambda bb: x_ref[bb]
    (mod_ref, nm_ref, w_ref, wt_ref, bg_ref, qn_ref, kn_ref, cos_ref, sa_ref, sb_ref,
     q_ref, k_ref, v_ref, mk_ref, mo_ref, pp_ref, mqt_ref, mvt_ref, g_ref) = refs
    nb, rows = q_ref.shape[0], q_ref.shape[1]

    hs = []
    for bb in range(nb):
        x = load(bb)
        ms = jnp.mean(x * x, axis=-1, keepdims=True)
        gain = nm_ref[...] * (1.0 + mod_ref[bb, 1])
        hs.append((x * lax.rsqrt(ms + EPS) * gain + mod_ref[bb, 0]).astype(BF16))
    accs = [jnp.dot(h, w_ref[...], preferred_element_type=F32) for h in hs]
    accs_t = [lax.dot_general(wt_ref[...], h, (((1,), (1,)), ((), ())), preferred_element_type=F32)
              for h in hs]

    lo_mask = lax.broadcasted_iota(jnp.int32, (rows, LANES), 1) < HEAD_DIM
    cos, sa, sb = cos_ref[...], sa_ref[...], sb_ref[...]

    def norm_rope(xc, gain):
        yc = xc * lax.rsqrt(_segment_ms(xc, lo_mask) + EPS) * gain
        return yc * cos + pltpu.roll(yc, LANES - 16, 1) * sa + pltpu.roll(yc, 16, 1) * sb

    for bb in range(nb):
        acc, acc_t = accs[bb], accs_t[bb]
        blk = lambda off, width: acc[:, off:off + width]
        for c in range(ATT_WIDTH // LANES):
            qc = norm_rope(blk(OFF_Q + c * LANES, LANES), qn_ref[...])
            q_ref[bb, :, c * LANES:(c + 1) * LANES] = (qc * Q_SCALE).astype(BF16)
        k_ref[bb] = norm_rope(blk(OFF_K, LANES), kn_ref[...]).astype(BF16)
        v_ref[bb] = blk(OFF_V, KV_WIDTH).astype(BF16)
        mk_ref[bb] = blk(OFF_MK, ML_WIDTH).astype(BF16)
        mo_ref[bb] = blk(OFF_MO, ML_WIDTH).astype(BF16)
        pp_ref[bb] = blk(OFF_PP, POOL_WIDTH).astype(BF16)
        mqt_ref[bb] = acc_t[ROW_MQ:ROW_MQ + ML_WIDTH, :].astype(BF16)
        mvt_ref[bb] = acc_t[ROW_MV:ROW_MV + ML_WIDTH, :].astype(BF16)
        g_ref[bb] = acc_t[ROW_G:ROW_G + N_GATES, :] + bg_ref[...]


def _mod_spec(layer, nb, ctx_tiles, ctx_block, first=0):
    def idx(b, i):
        return (layer, jnp.where(i + first < ctx_tiles, ctx_block, b), 0, 0, 0)
    return pl.BlockSpec((None, nb, 6, 1, D_MODEL), idx)


def _token_specs(x_in, nb, ctx_tiles, first=0):
    if isinstance(x_in, tuple):
        ctx, x = x_in
        specs = [pl.BlockSpec((nb, ROW_TILE, D_MODEL), lambda b, i: (b, jnp.minimum(i, ctx_tiles - 1), 0)),
                 pl.BlockSpec((nb, ROW_TILE, D_MODEL), lambda b, i: (b, jnp.maximum(i - ctx_tiles, 0), 0))]
        return specs, [ctx, x], ctx.shape[0], ctx.shape[1] + x.shape[1]
    specs = [pl.BlockSpec((nb, ROW_TILE, D_MODEL), lambda b, i: (b, i + first, 0))]
    return specs, [x_in], x_in.shape[0], x_in.shape[1]


def _in_proj(x_in, mod5, layer, norm_mix, w_in_r, w_in_t, b_gates_b, qn, kn, cos_t, sa_t, sb_t, ctx_tiles):
    nb = INPROJ_ROWS
    x_specs, x_ops, B, TA = _token_specs(x_in, nb, ctx_tiles)
    nt = TA // ROW_TILE
    row = lambda w: pl.BlockSpec((nb, ROW_TILE, w), lambda b, i: (b, i, 0))
    col = lambda r: pl.BlockSpec((nb, r, ROW_TILE), lambda b, i: (b, 0, i))
    vec = lambda w: pl.BlockSpec((None, 1, w), lambda b, i: (layer, 0, 0))
    full = lambda r, c: pl.BlockSpec((None, r, c), lambda b, i: (layer, 0, 0))
    tab = pl.BlockSpec((ROW_TILE, LANES), lambda b, i: (i, 0))
    row_widths = (ATT_WIDTH, KV_WIDTH, KV_WIDTH, ML_WIDTH, ML_WIDTH, POOL_WIDTH)
    col_heights = (ML_WIDTH, ML_WIDTH)
    out_shapes = [jax.ShapeDtypeStruct((B, TA, w), BF16) for w in row_widths]
    out_shapes += [jax.ShapeDtypeStruct((B, r, TA), BF16) for r in col_heights]
    out_shapes.append(jax.ShapeDtypeStruct((B, N_GATES, TA), F32))
    out_specs = [row(w) for w in row_widths] + [col(r) for r in col_heights] + [col(N_GATES)]
    kern = functools.partial(_inproj_kernel, ctx_tiles=ctx_tiles, split=len(x_ops) == 2)
    return pl.pallas_call(
        kern,
        out_shape=out_shapes,
        grid=(B // nb, nt),
        in_specs=x_specs + [_mod_spec(layer, nb, ctx_tiles, B // nb), vec(D_MODEL), full(D_MODEL, IN_COLS),
                            full(T_ROWS, D_MODEL), full(N_GATES, ROW_TILE), vec(LANES), vec(LANES), tab, tab, tab],
        out_specs=out_specs,
        compiler_params=pltpu.CompilerParams(vmem_limit_bytes=VMEM_LIMIT),
        name="in_proj",
    )(*x_ops, mod5, norm_mix, w_in_r, w_in_t, b_gates_b, qn, kn, cos_t, sa_t, sb_t)


def _attn_kernel(q_ref, k_ref, v_ref, o_ref, k0_ref, k1_ref, va_ref, *, ctx_len, first_tile):
    i = pl.program_id(1)
    nb, total = k_ref.shape[0], k_ref.shape[1]

    @pl.when(i == 0)
    def _():
        for bb in range(nb):
            k = k_ref[bb]
            lo = lax.broadcasted_iota(jnp.int32, k.shape, 1) < HEAD_DIM
            zero = jnp.zeros_like(k)
            k0_ref[bb] = jnp.where(lo, k, zero)
            k1_ref[bb] = jnp.where(lo, zero, k)
            va_ref[bb, :, 0:LANES] = v_ref[bb]
            va_ref[bb, :, LANES:2 * LANES] = jnp.ones((total, LANES), BF16)

    rows = q_ref.shape[1]
    lo_mask = lax.broadcasted_iota(jnp.int32, (rows, LANES), 1) < HEAD_DIM

    def block(nk):
        chunks = list(range(ATT_WIDTH // LANES))
        for bb in range(nb):
            for g in range(0, len(chunks), ATT_GROUP):
                cs = chunks[g:g + ATT_GROUP]
                ss = [lax.dot_general(q_ref[bb, :, c * LANES:(c + 1) * LANES], kr[bb, 0:nk, :], (((1,), (1,)), ((), ())),
                                      preferred_element_type=F32) for c in cs for kr in (k0_ref, k1_ref)]
                ms = [jnp.max(s, axis=-1, keepdims=True) for s in ss]
                ps = [jnp.exp2(s - m).astype(BF16) for s, m in zip(ss, ms)]
                acs = [jnp.dot(p, va_ref[bb, 0:nk, :], preferred_element_type=F32) for p in ps]
                outs = [a[:, 0:LANES] / a[:, LANES:2 * LANES] for a in acs]
                for n, c in enumerate(cs):
                    o_ref[bb, :, c * LANES:(c + 1) * LANES] = jnp.where(lo_mask, outs[2 * n], outs[2 * n + 1]).astype(BF16)

    if first_tile == 0:
        pl.when(i == 0)(lambda: block(ctx_len))
        pl.when(i > 0)(lambda: block(total))
    else:
        block(total)


def _attention(q, k, v, ctx_len, with_ctx):
    B, TA, _ = q.shape
    first = 0 if with_ctx else ctx_len // ROW_TILE
    nt = TA // ROW_TILE - first
    nb = ATT_ROWS if B % ATT_ROWS == 0 else 1
    kern = functools.partial(_attn_kernel, ctx_len=ctx_len, first_tile=first)
    return pl.pallas_call(
        kern,
        out_shape=jax.ShapeDtypeStruct((B, nt * ROW_TILE, ATT_WIDTH), BF16),
        grid=(B // nb, nt),
        in_specs=[pl.BlockSpec((nb, ROW_TILE, ATT_WIDTH), lambda b, i: (b, i + first, 0)),
                  pl.BlockSpec((nb, TA, KV_WIDTH), lambda b, i: (b, 0, 0)),
                  pl.BlockSpec((nb, TA, KV_WIDTH), lambda b, i: (b, 0, 0))],
        out_specs=pl.BlockSpec((nb, ROW_TILE, ATT_WIDTH), lambda b, i: (b, i, 0)),
        scratch_shapes=[pltpu.VMEM((nb, TA, KV_WIDTH), BF16), pltpu.VMEM((nb, TA, KV_WIDTH), BF16),
                        pltpu.VMEM((nb, TA, 2 * LANES), BF16)],
        compiler_params=pltpu.CompilerParams(vmem_limit_bytes=VMEM_LIMIT),
        name="attention",
    )(q, k, v)


def _mlstm_kernel(qf_ref, kf_ref, vf_ref, gf_ref, qb_ref, kb_ref, vb_ref, gb_ref, tri_ref, hf_ref, hb_ref,
                  st_ref, m_ref, r_ref, side_work=None):
    i = pl.program_id(1)

    @pl.when(i == 0)
    def _():
        st_ref[...] = jnp.zeros_like(st_ref)
        m_ref[...] = jnp.zeros_like(m_ref)
        r_ref[...] = jnp.zeros_like(r_ref)

    if side_work is not None:
        side_work()
    L = CHUNK
    s_idx = lax.broadcasted_iota(jnp.int32, (L, L), 0)
    t_idx = lax.broadcasted_iota(jnp.int32, (L, L), 1)
    first_head_rows = lax.broadcasted_iota(jnp.int32, (LANES, L), 0) < HEAD_DIM
    ones_blk = jnp.where(lax.broadcasted_iota(jnp.int32, (VA_ROWS - HEAD_DIM, L), 0) == 0, 1.0, 0.0).astype(BF16)

    dirs = ((qf_ref, kf_ref, vf_ref, gf_ref, hf_ref), (qb_ref, kb_ref, vb_ref, gb_ref, hb_ref))
    chains = [divmod(bd, 2) for bd in range(2 * hf_ref.shape[0])]
    refs = [tuple(ref.at[bb] for ref in dirs[d]) for bb, d in chains]

    heads = range(ML_HEADS)
    qm, kp, va, st, s_raw, a1 = {}, {}, {}, {}, {}, {}
    b8, inter8, a8, wk8, ct = {}, {}, {}, {}, {}
    dmat, m_t, s_t, a2, upd, hts = {}, {}, {}, {}, {}, {}

    def stage_operands(bds):
        for bd in bds:
            q_r, k_r, v_r = refs[bd][0:3]
            for hd in heads:
                pair, second = divmod(hd, 2)
                kp[bd, hd] = k_r[:, pair * LANES:(pair + 1) * LANES]
                q_pair = q_r[pair * LANES:(pair + 1) * LANES, :]
                keep = jnp.logical_not(first_head_rows) if second else first_head_rows
                qm[bd, hd] = jnp.where(keep, q_pair, jnp.zeros_like(q_pair))
                va[bd, hd] = jnp.concatenate([v_r[hd * HEAD_DIM:(hd + 1) * HEAD_DIM, :], ones_blk], axis=0)
                st[bd, hd] = st_ref[bd * ML_HEADS + hd]
                s_raw[bd, hd] = jnp.dot(kp[bd, hd], qm[bd, hd], preferred_element_type=F32)
                a1[bd, hd] = jnp.dot(st[bd, hd].astype(BF16), qm[bd, hd], preferred_element_type=F32)

    def stage_gates(bds):
        g8 = {bd: refs[bd][3][8 * chains[bd][1]:8 * chains[bd][1] + 8, :] for bd in bds}
        lf8 = {bd: jnp.minimum(g8[bd], 0.0) - jnp.log1p(jnp.exp(-jnp.abs(g8[bd]))) for bd in bds}
        parts = {bd: jnp.concatenate([p.astype(F32) for p in _split3(lf8[bd])], axis=0).astype(BF16) for bd in bds}
        cum = {bd: jnp.dot(parts[bd], tri_ref[chains[bd][1]], preferred_element_type=F32) for bd in bds}
        cum = {bd: c[16:24] + c[8:16] + c[0:8] for bd, c in cum.items()}
        li8 = {bd: pltpu.roll(g8[bd], ML_HEADS, 0) for bd in bds}
        for bd in bds:
            b8[bd] = cum[bd][:, 0:L]
            r_ref[bd, 0:8, :] = li8[bd] - b8[bd]
        for bd in bds:
            ct[bd] = r_ref[bd].T
        for bd in bds:
            btot = cum[bd][:, L:2 * L]
            m_prev = m_ref[bd]
            gg = btot - b8[bd] + li8[bd]
            m_new = jnp.maximum(btot + m_prev, jnp.max(gg, axis=-1, keepdims=True))
            a8[bd] = jnp.exp(btot + m_prev - m_new)
            wk8[bd] = jnp.exp(gg - m_new)
            inter8[bd] = b8[bd] + m_prev
            m_ref[bd] = m_new

    def stage_decay(bds):
        for bd in bds:
            causal = (s_idx <= t_idx) if chains[bd][1] == 0 else (s_idx >= t_idx)
            for hd in heads:
                r = ML_HEADS + hd
                dmat[bd, hd] = jnp.where(causal, ct[bd][:, r:r + 1] + b8[bd][r:r + 1], -jnp.inf)
                m_t[bd, hd] = jnp.maximum(inter8[bd][r:r + 1], jnp.max(dmat[bd, hd], axis=0, keepdims=True))

    def stage_scores(bds):
        for bd in bds:
            for hd in heads:
                s_t[bd, hd] = (s_raw[bd, hd] * jnp.exp(dmat[bd, hd] - m_t[bd, hd])).astype(BF16)

    def stage_values(bds):
        for bd in bds:
            for hd in heads:
                r = ML_HEADS + hd
                a2[bd, hd] = jnp.dot(va[bd, hd], s_t[bd, hd], preferred_element_type=F32)
                vw = (va[bd, hd].astype(F32) * wk8[bd][r:r + 1]).astype(BF16)
                upd[bd, hd] = jnp.dot(vw, kp[bd, hd], preferred_element_type=F32)

    def stage_finish(bds):
        for bd in bds:
            for hd in heads:
                r = ML_HEADS + hd
                inter = inter8[bd][r:r + 1]
                nd = jnp.exp(inter - m_t[bd, hd]) * a1[bd, hd] + a2[bd, hd]
                den = nd[HEAD_DIM:HEAD_DIM + 1, :]
                hts[bd, hd] = nd[0:HEAD_DIM, :] / jnp.maximum(jnp.abs(den), jnp.exp(-m_t[bd, hd]))
                st_ref[bd * ML_HEADS + hd] = a8[bd][r:r + 1] * st[bd, hd] + upd[bd, hd]
        for bd in bds:
            refs[bd][4][...] = jnp.concatenate([hts[bd, hd] for hd in heads], axis=0).T.astype(BF16)

    stages = (stage_gates, stage_operands, stage_decay, stage_scores, stage_values, stage_finish)
    waves = [list(range(w, min(w + MLSTM_WAVE, len(chains)))) for w in range(0, len(chains), MLSTM_WAVE)]
    for step in range(len(stages) + len(waves) - 1):
        for w, bds in enumerate(waves):
            if 0 <= step - w < len(stages):
                stages[step - w](bds)


def _mlstm_pool_kernel(*refs, ctx_len):
    ml_in, (pp_ref, ic_ref, bd_ref, ps_ref) = refs[0:9], refs[9:13]
    hf_ref, hb_ref, pool_ref = refs[13:16]
    st_ref, m_ref, r_ref, p_ref, s2_ref, s4_ref, s8_ref = refs[16:23]
    pool = functools.partial(_pool_kernel, pp_ref, ic_ref, bd_ref, ps_ref, pool_ref, p_ref, s2_ref, s4_ref, s8_ref,
                             ctx_len=ctx_len)
    _mlstm_kernel(*ml_in, hf_ref, hb_ref, st_ref, m_ref, r_ref, side_work=pool)


def _mlstm_pool(mqt, mk, mvt, gates, tri, pp, inv_cnt, pool_bd, pool_scale, layer, ctx_len):
    B, TA, _ = mk.shape
    nc = TA // CHUNK
    ncc = ctx_len // CHUNK

    def fwd(i):
        return i

    def bwd(i):
        return jnp.where(i < ncc, ncc - 1 - i, nc - 1 - (i - ncc))

    nb = MLSTM_ROWS if B % MLSTM_ROWS == 0 and MLSTM_ROWS <= nc else 1
    tok = lambda f: pl.BlockSpec((nb, CHUNK, ML_WIDTH), lambda b, i: (b, f(i), 0))
    feat = lambda f, r: pl.BlockSpec((nb, r, CHUNK), lambda b, i: (b, 0, f(i)))
    pool_row = pl.BlockSpec((None, TA, POOL_WIDTH), lambda b, i: (b * nb + jnp.minimum(i, nb - 1), 0, 0))
    n_units = 2 * nb * ML_HEADS
    rows = TA + 3 * POOL_PAD
    return pl.pallas_call(
        functools.partial(_mlstm_pool_kernel, ctx_len=ctx_len),
        out_shape=[jax.ShapeDtypeStruct((B, TA, ML_WIDTH), BF16)] * 2 + [jax.ShapeDtypeStruct((B, TA, POOL_WIDTH), BF16)],
        grid=(B // nb, nc),
        in_specs=[feat(fwd, ML_WIDTH), tok(fwd), feat(fwd, ML_WIDTH), feat(fwd, N_GATES),
                  feat(bwd, ML_WIDTH), tok(bwd), feat(bwd, ML_WIDTH), feat(bwd, N_GATES),
                  pl.BlockSpec((2, CHUNK, 2 * CHUNK), lambda b, i: (0, 0, 0)),
                  pool_row, pl.BlockSpec((TA, POOL_WIDTH), lambda b, i: (0, 0)),
                  pl.BlockSpec((None, POOL_WIDTH, POOL_WIDTH), lambda b, i: (layer, 0, 0)),
                  pl.BlockSpec((None, 1, POOL_WIDTH), lambda b, i: (layer, 0, 0))],
        out_specs=[tok(fwd), tok(bwd), pool_row],
        scratch_shapes=[pltpu.VMEM((n_units, VA_ROWS, LANES), F32), pltpu.VMEM((2 * nb, 8, LANES), F32),
                        pltpu.VMEM((2 * nb, CHUNK, LANES), F32)]
        + [pltpu.VMEM((rows, POOL_WIDTH), F32)] * 3 + [pltpu.VMEM((rows, LANES), F32)],
        compiler_params=pltpu.CompilerParams(vmem_limit_bytes=VMEM_LIMIT),
        name="mlstm_pool",
    )(mqt, mk, mvt, gates, mqt, mk, mvt, gates, tri, pp, inv_cnt, pool_bd, pool_scale)


def _pool_kernel(pp_ref, ic_ref, bd_ref, ps_ref, o_ref, p_ref, s2_ref, s4_ref, s8_ref, *, ctx_len):
    total = pp_ref.shape[0]
    seqs = ((0, ctx_len, POOL_PAD), (ctx_len, total - ctx_len, 2 * POOL_PAD + ctx_len))
    rows = p_ref.shape[0]
    upper = slice(LANES, 2 * LANES)
    for ref in (p_ref, s2_ref, s4_ref, s8_ref):
        ref[...] = jnp.zeros_like(ref)
    for src, n, dst in seqs:
        p_ref[dst:dst + n, :] = pp_ref[src:src + n, :].astype(F32)
    lo, m = 8, rows - 16
    s2_ref[lo:lo + m, :] = p_ref[lo - 1:lo - 1 + m, :] + p_ref[lo:lo + m, :]
    s4_ref[lo:lo + m, :] = s2_ref[lo - 1:lo - 1 + m, :] + s2_ref[lo + 1:lo + 1 + m, :]
    s8_ref[lo:lo + m, :] = s4_ref[lo - 2:lo - 2 + m, upper] + s4_ref[lo + 2:lo + 2 + m, upper]
    for src, n, dst in seqs:
        first_group = lax.broadcasted_iota(jnp.int32, (n, LANES), 1) < POOL_GROUP_DIM
        s16 = s8_ref[dst - 4:dst - 4 + n, :] + s8_ref[dst + 4:dst + 4 + n, :]
        wsum = jnp.concatenate([jnp.where(first_group, s2_ref[dst:dst + n, 0:LANES], s4_ref[dst:dst + n, 0:LANES]),
                                jnp.where(first_group, s8_ref[dst:dst + n, :], s16)], axis=-1)
        p = wsum * ic_ref[src:src + n, :] - p_ref[dst:dst + n, :]
        y = jnp.dot(p.astype(BF16), bd_ref[...], preferred_element_type=F32) * ps_ref[...]
        o_ref[src:src + n, :] = y.astype(BF16)


def _inverse_window_counts(ctx_len, t_latent):
    half = np.repeat(np.array(POOL_WINDOWS) // 2, POOL_GROUP_DIM)[None, :]
    tables = []
    for n in (ctx_len, t_latent):
        t = np.arange(n)[:, None]
        tables.append(1.0 / (np.minimum(t + half, n) - np.maximum(t - half, 0)))
    return jnp.asarray(np.concatenate(tables, axis=0), dtype=F32)


def _mixffn_kernel(*refs, ctx_tiles, split, final):
    if split:
        ctx_ref, x_ref = refs[0:2]
        refs = refs[2:]
        from_ctx = pl.program_id(1) < ctx_tiles
        load = lambda bb: jnp.where(from_ctx, ctx_ref[bb], x_ref[bb])
    else:
        x_ref = refs[0]
        refs = refs[1:]
        load = lambda bb: x_ref[bb]
    (att_ref, hf_ref, hb_ref, mo_ref, pool_ref, mod_ref, mln_ref, wo_ref, nf_ref,
     wg_ref, wu_ref, wd_ref, fn_ref, o_ref) = refs
    nb, rows = o_ref.shape[0], o_ref.shape[1]
    lo_mask = lax.broadcasted_iota(jnp.int32, (rows, LANES), 1) < HEAD_DIM

    cats = []
    for bb in range(nb):
        ml = []
        for c in range(ML_WIDTH // LANES):
            sl = slice(c * LANES, (c + 1) * LANES)
            hs = hf_ref[bb, :, sl].astype(F32) + hb_ref[bb, :, sl].astype(F32)
            hn = hs * lax.rsqrt(_segment_ms(hs, lo_mask) + EPS) * mln_ref[:, sl]
            ml.append((hn * _sigmoid(mo_ref[bb, :, sl].astype(F32))).astype(BF16))
        cats.append(jnp.concatenate([att_ref[bb]] + ml + [pool_ref[bb]], axis=-1))
    mixes = [jnp.dot(cat, wo_ref[...], preferred_element_type=F32) for cat in cats]

    x1s, h2s = [], []
    for bb in range(nb):
        x1 = load(bb) + mod_ref[bb, 2] * mixes[bb]
        ms = jnp.mean(x1 * x1, axis=-1, keepdims=True)
        gain = nf_ref[...] * (1.0 + mod_ref[bb, 4])
        h2s.append((x1 * lax.rsqrt(ms + EPS) * gain + mod_ref[bb, 3]).astype(BF16))
        x1s.append(x1)
    downs = []
    for h2 in h2s:
        down = None
        for lo, hi in FF_SPLITS:
            gate = jnp.dot(h2, wg_ref[:, lo:hi], preferred_element_type=F32)
            up = jnp.dot(h2, wu_ref[:, lo:hi], preferred_element_type=F32)
            act = (gate * _sigmoid(gate) * up).astype(BF16)
            part = jnp.dot(act, wd_ref[lo:hi, :], preferred_element_type=F32)
            down = part if down is None else down + part
        downs.append(down)
    for bb in range(nb):
        out = x1s[bb] + mod_ref[bb, 5] * downs[bb]
        if final:
            ms = jnp.mean(out * out, axis=-1, keepdims=True)
            out = out * lax.rsqrt(ms + EPS) * fn_ref[...]
        o_ref[bb] = out


def _mix_ffn(x_in, att, hf, hb, mo, pool, mod5, layer, ml_norm, w_out_r, norm_ffn, wg, wu, wd, final_norm,
             ctx_tiles, final):
    nb = STEP_ROWS
    first = ctx_tiles if final else 0
    x_specs, x_ops, B, TA = _token_specs(x_in, nb, ctx_tiles, first)
    nt = TA // ROW_TILE - first
    row = lambda w: pl.BlockSpec((nb, ROW_TILE, w), lambda b, i: (b, i + first, 0))
    vec = lambda w: pl.BlockSpec((None, 1, w), lambda b, i: (layer, 0, 0))
    wspec = lambda r, c: pl.BlockSpec((None, r, c), lambda b, i: (layer, 0, 0), pipeline_mode=pl.Buffered(1))

    kern = functools.partial(_mixffn_kernel, ctx_tiles=ctx_tiles, split=len(x_ops) == 2, final=final)
    return pl.pallas_call(
        kern,
        out_shape=jax.ShapeDtypeStruct((B, nt * ROW_TILE, D_MODEL), F32),
        grid=(B // nb, nt),
        in_specs=x_specs + [pl.BlockSpec((nb, ROW_TILE, ATT_WIDTH), lambda b, i: (b, i, 0)),
                            row(ML_WIDTH), row(ML_WIDTH), row(ML_WIDTH), row(POOL_WIDTH),
                            _mod_spec(layer, nb, ctx_tiles, B // nb, first), vec(ML_WIDTH),
                            wspec(D_MODEL, D_MODEL), vec(D_MODEL), wspec(D_MODEL, D_FF), wspec(D_MODEL, D_FF),
                            wspec(D_FF, D_MODEL), pl.BlockSpec((1, D_MODEL), lambda b, i: (0, 0))],
        out_specs=pl.BlockSpec((nb, ROW_TILE, D_MODEL), lambda b, i: (b, i, 0)),
        compiler_params=pltpu.CompilerParams(vmem_limit_bytes=VMEM_LIMIT),
        name="mix_ffn",
    )(*x_ops, att, hf, hb, mo, pool, mod5, ml_norm, w_out_r, norm_ffn, wg, wu, wd, final_norm)


def _rope_tables(t_latent, ctx_len):
    f32 = np.float32
    half = HEAD_DIM // 2
    inv_freq = (f32(1.0) / np.power(f32(ROPE_THETA), np.arange(0, half, 2, dtype=f32) / f32(half))).astype(f32)
    rows = t_latent // GRID_W
    row = np.repeat(np.arange(rows, dtype=f32), GRID_W)
    col = np.tile(np.arange(GRID_W, dtype=f32), rows)
    a_row = row[:, None] * inv_freq
    a_col = col[:, None] * inv_freq
    ang = np.concatenate([a_row, a_row, a_col, a_col], -1).astype(f32)
    cos, sin = np.cos(ang), np.sin(ang)
    first_half = (np.arange(HEAD_DIM) % 32) < 16
    sa = np.where(first_half, -sin, 0.0)
    sb = np.where(first_half, 0.0, sin)
    pad = lambda a, fill: jnp.asarray(np.tile(np.concatenate([np.full((ctx_len, HEAD_DIM), fill, f32), a], 0), (1, 2)),
                                      dtype=F32)
    return pad(cos, 1.0), pad(sa, 0.0), pad(sb, 0.0)


_IN = np.arange(2064)
COLS_TOKEN = np.concatenate([HEAD_PERM, _IN[512:768], _IN[1024:1280], _IN[1536:1792], _IN[1808:2064]])
COLS_FEATURE = np.concatenate([_IN[768:1024], _IN[1280:1536], _IN[1792:1808]])
IN_SCALE = np.where((_IN >= 1024) & (_IN < 1280), HEAD_DIM ** -0.5, 1.0).astype(np.float32)
ROWS_OUT = np.concatenate([HEAD_PERM, np.arange(ATT_WIDTH, D_MODEL)])


def _take_runs(a, index, axis):
    cuts = [0] + [n for n in range(1, len(index)) if index[n] != index[n - 1] + 1] + [len(index)]
    runs = [lax.slice_in_dim(a, int(index[s]), int(index[e - 1]) + 1, axis=axis) for s, e in zip(cuts[:-1], cuts[1:])]
    return jnp.concatenate(runs, axis=axis)


def _prepare_params(w_in, b_gates, q_norm, k_norm, pool_w, w_out, w_ffn_gate, w_ffn_up, w_ffn_down):
    depth = w_in.shape[0]
    groups = len(POOL_WINDOWS)
    eye = jnp.eye(groups, dtype=F32)
    w_in_b = (w_in * IN_SCALE).astype(BF16)
    return dict(
        w_rows=_take_runs(w_in_b, COLS_TOKEN, 2),
        w_feat=jnp.swapaxes(_take_runs(w_in_b, COLS_FEATURE, 2), 1, 2),
        b_gates=jnp.broadcast_to(b_gates[:, :, None], (depth, N_GATES, ROW_TILE)),
        qn=jnp.tile(q_norm, (1, 2))[:, None, :], kn=jnp.tile(k_norm, (1, 2))[:, None, :],
        pool_bd=(pool_w[:, :, :, None, :] * eye[None, :, None, :, None]).reshape(depth, POOL_WIDTH, POOL_WIDTH).astype(BF16),
        w_out=_take_runs(w_out.astype(BF16), ROWS_OUT, 1),
        wg=w_ffn_gate.astype(BF16), wu=w_ffn_up.astype(BF16), wd=w_ffn_down.astype(BF16))


def _cumsum_matrices():
    s = np.arange(CHUNK)[:, None]
    t = np.arange(CHUNK)[None, :]
    ones = np.ones((CHUNK, CHUNK), np.float32)
    prefix = np.concatenate([(s <= t).astype(np.float32), ones], axis=1)
    suffix = np.concatenate([(s >= t).astype(np.float32), ones], axis=1)
    return jnp.asarray(np.stack([prefix, suffix]), dtype=BF16)


def kernel(x, c, ctx, c_ctx, w_ada, b_ada, norm_mix, w_in, b_gates, q_norm, k_norm, ml_norm, pool_w, pool_scale,
           w_out, norm_ffn, w_ffn_gate, w_ffn_up, w_ffn_down, final_norm):
    B, T, _ = x.shape
    ctx_len = ctx.shape[1]
    depth = w_ada.shape[0]
    n_rep = max(STEP_ROWS, INPROJ_ROWS)
    assert ctx_len % ROW_TILE == 0 and T % ROW_TILE == 0 and B % n_rep == 0 and B + n_rep <= MOD_ROWS
    ctx_tiles = ctx_len // ROW_TILE

    c_rep = jnp.broadcast_to(c_ctx[None, :], (n_rep, D_MODEL))
    c_all = jnp.concatenate([c, c_rep, jnp.zeros((MOD_ROWS - B - n_rep, D_MODEL), F32)], 0)
    mod = _modulation(c_all, w_ada, b_ada)
    mod5 = mod.reshape(depth, MOD_ROWS, 6, 1, D_MODEL)
    cos_t, sa_t, sb_t = _rope_tables(T, ctx_len)
    tri = _cumsum_matrices()
    inv_cnt = _inverse_window_counts(ctx_len, T)
    p = _prepare_params(w_in, b_gates, q_norm, k_norm, pool_w, w_out, w_ffn_gate, w_ffn_up, w_ffn_down)
    stack = lambda a: a[:, None, :]

    x_all = (ctx, x)
    for l in range(depth):
        final = l == depth - 1
        q, k, v, mk, mo, pp, mqt, mvt, gates = _in_proj(
            x_all, mod5, l, stack(norm_mix), p["w_rows"], p["w_feat"], p["b_gates"], p["qn"], p["kn"],
            cos_t, sa_t, sb_t, ctx_tiles)
        att = _attention(q, k, v, ctx_len, with_ctx=not final)
        hf, hb, pool = _mlstm_pool(mqt, mk, mvt, gates, tri, pp, inv_cnt, p["pool_bd"], stack(pool_scale), l, ctx_len)
        x_all = _mix_ffn(x_all, att, hf, hb, mo, pool, mod5, l, stack(ml_norm), p["w_out"], stack(norm_ffn),
                         p["wg"], p["wu"], p["wd"], final_norm.reshape(1, -1), ctx_tiles, final)
    return x_all
```
